```python
import jax, jax.numpy as jnp
from jax import lax
import numpy as np

D_MODEL = 1024
BATCH = 8
SEQ = 2048
DEPTH = 2
DEC_BATCH = 128
DEC_SEQ = 4
PAST_LEN = 16384
PAGE_SIZE = 128

CONV_DIM = D_MODEL // 2
CONV_WIDTH = 31
RET_DIM = D_MODEL - CONV_DIM
RET_HEADS = 4
RET_HEAD_DIM = RET_DIM // RET_HEADS
RET_CHUNK = 128
D_FF = 4 * D_MODEL
ROPE_BASE = 10000.0
EPS = 1e-6
IN_COLS = 2 * CONV_DIM + 4 * RET_DIM

kernel_name = "hymba_conformerconv_retention_decode_step"


def rmsnorm(x, g):
    xf = x.astype(jnp.float32)
    y = xf * lax.rsqrt(jnp.mean(xf * xf, axis=-1, keepdims=True) + EPS)
    return (y * g.astype(jnp.float32)).astype(x.dtype)


def rotary(t, pos):
    half = t.shape[-1] // 2
    inv = ROPE_BASE ** (-jnp.arange(half, dtype=jnp.float32) / half)
    ang = pos[:, None] * inv[None, :]
    cos, sin = jnp.cos(ang), jnp.sin(ang)
    t1, t2 = t[..., :half], t[..., half:]
    return jnp.concatenate([t1 * cos - t2 * sin, t2 * cos + t1 * sin], axis=-1)


def retention_log_gamma():
    return jnp.log(1.0 - jnp.exp2(-5.0 - jnp.arange(RET_HEADS, dtype=jnp.float32)))


def retention_chunk(state, q, k, v, log_gamma):
    L = q.shape[2]
    idx = jnp.arange(L, dtype=jnp.float32)
    diff = idx[:, None] - idx[None, :]
    decay = jnp.where(diff[None] >= 0,
                      jnp.exp(log_gamma[:, None, None] * jnp.maximum(diff, 0.0)[None]), 0.0)
    scores = jnp.einsum('bhid,bhjd->bhij', q, k) * decay
    inner = jnp.einsum('bhij,bhjv->bhiv', scores, v)
    cross = jnp.einsum('bhid,bhdv->bhiv', q, state) * jnp.exp(log_gamma[:, None] * (idx + 1.0))[None, :, :, None]
    k_dec = k * jnp.exp(log_gamma[:, None] * (L - 1.0 - idx))[None, :, :, None]
    new_state = state * jnp.exp(log_gamma * L)[None, :, None, None] + jnp.einsum('bhjd,bhjv->bhdv', k_dec, v)
    return new_state, inner + cross


def retention(q, k, v, state):
    B, H, T, d = q.shape
    C = RET_CHUNK if T % RET_CHUNK == 0 else T
    n = T // C
    log_gamma = retention_log_gamma()

    def to_chunks(t):
        return t.reshape(B, H, n, C, t.shape[-1]).transpose(2, 0, 1, 3, 4)

    def step(s, qkv):
        return retention_chunk(s, qkv[0], qkv[1], qkv[2], log_gamma)

    final, outs = lax.scan(step, state, (to_chunks(q), to_chunks(k), to_chunks(v)))
    o = outs.transpose(1, 2, 0, 3, 4).reshape(B, H, T, v.shape[-1])
    return o, final


def mixer(h, conv_buf, ret_state, pos0, w_in, conv_w, conv_b, conv_ln_g, conv_ln_b, ret_gn_g, w_out):
    B, T, _ = h.shape
    proj = h @ w_in
    c0 = 2 * CONV_DIM
    a, b, q, k, v, g = jnp.split(proj, [CONV_DIM, c0, c0 + RET_DIM, c0 + 2 * RET_DIM, c0 + 3 * RET_DIM], axis=-1)

    glu = a * jax.nn.sigmoid(b)
    xp = jnp.concatenate([conv_buf.astype(glu.dtype), glu], axis=1)
    c = lax.conv_general_dilated(xp, conv_w[:, None, :].astype(xp.dtype), (1,), 'VALID',
                                 dimension_numbers=('NWC', 'WIO', 'NWC'),
                                 feature_group_count=CONV_DIM) + conv_b.astype(xp.dtype)
    new_buf = xp[:, -(CONV_WIDTH - 1):]
    cf = c.astype(jnp.float32)
    mu = jnp.mean(cf, axis=-1, keepdims=True)
    var = jnp.mean(jnp.square(cf - mu), axis=-1, keepdims=True)
    cn = (cf - mu) * lax.rsqrt(var + EPS) * conv_ln_g.astype(jnp.float32) + conv_ln_b.astype(jnp.float32)
    conv_out = jax.nn.silu(cn).astype(h.dtype)

    pos = pos0 + jnp.arange(T, dtype=jnp.float32)

    def heads(t):
        return t.reshape(B, T, RET_HEADS, RET_HEAD_DIM).transpose(0, 2, 1, 3).astype(jnp.float32)

    qh = rotary(heads(q), pos)
    kh = rotary(heads(k), pos) * (RET_HEAD_DIM ** -0.5)
    vh = heads(v)
    o, new_state = retention(qh, kh, vh, ret_state.astype(jnp.float32))
    omu = jnp.mean(o, axis=-1, keepdims=True)
    ovar = jnp.mean(jnp.square(o - omu), axis=-1, keepdims=True)
    on = ((o - omu) * lax.rsqrt(ovar + EPS)).transpose(0, 2, 1, 3).reshape(B, T, RET_DIM)
    ret_out = (on * ret_gn_g.astype(jnp.float32) * jax.nn.silu(g.astype(jnp.float32))).astype(h.dtype)

    y = jnp.concatenate([conv_out, ret_out], axis=-1) @ w_out
    return y, new_buf, new_state.astype(ret_state.dtype)


def trunk(x, conv_bufs, ret_states, pos0, norm1_g, w_in, conv_w, conv_b, conv_ln_g, conv_ln_b,
          ret_gn_g, w_out, norm2_g, w_up, w_down, final_norm_g):
    new_bufs, new_states = [], []
    for l in range(DEPTH):
        y, nb, ns = mixer(rmsnorm(x, norm1_g[l]), conv_bufs[l], ret_states[l], pos0, w_in[l], conv_w[l],
                          conv_b[l], conv_ln_g[l], conv_ln_b[l], ret_gn_g[l], w_out[l])
        x = x + y
        hf = rmsnorm(x, norm2_g[l]) @ w_up[l]
        x = x + jnp.square(jax.nn.relu(hf)) @ w_down[l]
        new_bufs.append(nb)
        new_states.append(ns)
    return rmsnorm(x, final_norm_g), jnp.stack(new_bufs), jnp.stack(new_states)


def setup_inputs(seed: int = 0) -> dict:
    key = jax.random.key(seed)
    ks = jax.random.split(key, 20)
    f32 = jnp.float32
    nrm = lambda k, s, sc: jax.random.normal(k, s, f32) * sc
    return {
        "x_prompt": nrm(ks[0], (BATCH, SEQ, D_MODEL), 1.0),
        "x_sample": nrm(ks[1], (DEC_BATCH, DEC_SEQ, D_MODEL), 1.0),
        "cache_conv": nrm(ks[2], (DEPTH, DEC_BATCH, CONV_WIDTH - 1, CONV_DIM), 0.5),
        "state_ret": nrm(ks[3], (DEPTH, DEC_BATCH, RET_HEADS, RET_HEAD_DIM, RET_HEAD_DIM), 0.5),
        "norm1_g": 1.0 + nrm(ks[4], (DEPTH, D_MODEL), 0.02),
        "w_in": nrm(ks[5], (DEPTH, D_MODEL, IN_COLS), D_MODEL ** -0.5),
        "conv_w": nrm(ks[6], (DEPTH, CONV_WIDTH, CONV_DIM), CONV_WIDTH ** -0.5),
        "conv_b": nrm(ks[7], (DEPTH, CONV_DIM), 0.02),
        "conv_ln_g": 1.0 + nrm(ks[8], (DEPTH, CONV_DIM), 0.02),
        "conv_ln_b": nrm(ks[9], (DEPTH, CONV_DIM), 0.02),
        "ret_gn_g": 1.0 + nrm(ks[10], (DEPTH, RET_DIM), 0.02),
        "w_out": nrm(ks[11], (DEPTH, D_MODEL, D_MODEL), D_MODEL ** -0.5),
        "norm2_g": 1.0 + nrm(ks[12], (DEPTH, D_MODEL), 0.02),
        "w_up": nrm(ks[13], (DEPTH, D_MODEL, D_FF), D_MODEL ** -0.5),
        "w_down": nrm(ks[14], (DEPTH, D_FF, D_MODEL), D_FF ** -0.5),
        "final_norm_g": 1.0 + nrm(ks[15], (D_MODEL,), 0.02),
    }


def reference(x_prompt, x_sample, cache_conv, state_ret, norm1_g, w_in, conv_w, conv_b, conv_ln_g,
              conv_ln_b, ret_gn_g, w_out, norm2_g, w_up, w_down, final_norm_g):
    conv0 = jnp.zeros((DEPTH, x_prompt.shape[0], CONV_WIDTH - 1, CONV_DIM), x_prompt.dtype)
    ret0 = jnp.zeros((DEPTH, x_prompt.shape[0], RET_HEADS, RET_HEAD_DIM, RET_HEAD_DIM), state_ret.dtype)
    y_prompt, new_conv_prompt, new_ret_prompt = trunk(
        x_prompt, conv0, ret0, 0.0, norm1_g, w_in, conv_w, conv_b, conv_ln_g, conv_ln_b,
        ret_gn_g, w_out, norm2_g, w_up, w_down, final_norm_g)
    y_sample, new_conv_sample, new_ret_sample = trunk(
        x_sample, cache_conv, state_ret, float(PAST_LEN), norm1_g, w_in, conv_w, conv_b, conv_ln_g,
        conv_ln_b, ret_gn_g, w_out, norm2_g, w_up, w_down, final_norm_g)
    return (y_prompt, y_sample, new_conv_prompt, new_ret_prompt, new_conv_sample, new_ret_sample)
```

```python
import functools
import math

import numpy as np
import jax
import jax.numpy as jnp
from jax import lax
from jax.experimental import pallas as pl
from jax.experimental.pallas import tpu as pltpu

F32 = jnp.float32
BF16 = jnp.bfloat16

EPS = 1e-6
ROPE_BASE = 10000.0
PAST_LEN = 16384
LANES = 128
SUBLANES = 8
VMEM_LIMIT = 56 * 1024 * 1024

PROMPT_CHUNK = 256
POST_ROWS = 256
SAMPLE_SEQS = 8


def _log_gamma(h):
    return math.log(1.0 - 2.0 ** (-5.0 - h))


def _rmsnorm(x, g):
    ms = jnp.mean(x * x, axis=-1, keepdims=True)
    return x * lax.rsqrt(ms + EPS) * g


def _layernorm(x):
    mu = jnp.mean(x, axis=-1, keepdims=True)
    xc = x - mu
    var = jnp.mean(xc * xc, axis=-1, keepdims=True)
    return xc * lax.rsqrt(var + EPS)


def _silu(x):
    return x * jax.nn.sigmoid(x)


def _rotary(t, cos2, sin2):
    half = t.shape[-1] // 2
    return t * cos2 + pltpu.roll(t, half, t.ndim - 1) * sin2


def _const_spec(shape):
    nd = len(shape)
    return pl.BlockSpec(shape, lambda *_: (0,) * nd, pipeline_mode=pl.Buffered(1))


def _prompt_mix_kernel(x_ref, g1_ref, win_ref, cw_ref, cvec_ref, gn_ref, cos_ref, sin_ref,
                       dec_ref, rdec_ref, kdec_ref,
                       mix_ref, nconv_ref, nret_ref,
                       glu_s, conv_s, q_s, k_s, v_s, g_s, state_s,
                       *, tq, heads, hd, cdim, cwidth, hist):
    c = pl.program_id(1)
    last = pl.num_programs(1) - 1
    rdim = heads * hd

    @pl.when(c == 0)
    def _():
        glu_s[0:hist, :] = jnp.zeros((hist, cdim), F32)
        state_s[...] = jnp.zeros_like(state_s)

    h = _rmsnorm(x_ref[0], g1_ref[...]).astype(BF16)

    def proj(lo, n):
        return jnp.dot(h, win_ref[:, lo:lo + n], preferred_element_type=F32)

    glu_s[hist:hist + tq, :] = proj(0, cdim) * jax.nn.sigmoid(proj(cdim, cdim))
    off = hist - (cwidth - 1)
    for lb in range(cdim // LANES):
        ls = slice(lb * LANES, (lb + 1) * LANES)
        acc = glu_s[off:off + tq, ls] * cw_ref[0:1, ls]
        for w in range(1, cwidth):
            acc = acc + glu_s[off + w:off + w + tq, ls] * cw_ref[w:w + 1, ls]
        conv_s[:, ls] = acc + cvec_ref[0:1, ls]
    cn = _layernorm(conv_s[...]) * cvec_ref[1:2, :] + cvec_ref[2:3, :]
    mix_ref[0, :, 0:cdim] = _silu(cn).astype(BF16)

    @pl.when(c == last)
    def _():
        nconv_ref[0] = glu_s[hist + tq - (cwidth - 1):hist + tq, :]

    glu_s[0:hist, :] = glu_s[tq:tq + hist, :]

    c0 = 2 * cdim
    qf = proj(c0, rdim)
    kf = proj(c0 + rdim, rdim)
    cos2 = cos_ref[...]
    sin2 = sin_ref[...]
    scale = hd ** -0.5
    for hh in range(heads):
        hs = slice(hh * hd, (hh + 1) * hd)
        q_s[:, hs] = _rotary(qf[:, hs], cos2, sin2).astype(BF16)
        k_s[:, hs] = _rotary(kf[:, hs], cos2, sin2) * scale
    v_s[...] = proj(c0 + 2 * rdim, rdim).astype(BF16)
    g_s[...] = _silu(proj(c0 + 3 * rdim, rdim))

    for hh in range(heads):
        hs = slice(hh * hd, (hh + 1) * hd)
        qh = q_s[:, hs]
        kh = k_s[:, hs]
        vh = v_s[:, hs]
        st = state_s[hh]
        scores = lax.dot_general(qh, kh.astype(BF16), (((1,), (1,)), ((), ())),
                                 preferred_element_type=F32) * dec_ref[hh]
        inner = jnp.dot(scores.astype(BF16), vh, preferred_element_type=F32)
        cross = jnp.dot(qh, st.astype(BF16), preferred_element_type=F32) * rdec_ref[hh]
        kd = (kh * kdec_ref[hh]).astype(BF16)
        upd = lax.dot_general(kd, vh, (((0,), (0,)), ((), ())), preferred_element_type=F32)
        state_s[hh] = st * math.exp(_log_gamma(hh) * tq) + upd
        on = _layernorm(inner + cross)
        mix_ref[0, :, cdim + hh * hd:cdim + (hh + 1) * hd] = (on * gn_ref[0:1, hs] * g_s[:, hs]).astype(BF16)

    @pl.when(c == last)
    def _():
        nret_ref[0] = state_s[...]


def _prompt_mix(x, g1, w_in, conv_w, cvec, gn_g, cos2, sin2, heads, hd):
    B, T, D = x.shape
    cwidth, cdim = conv_w.shape
    rdim = heads * hd
    tq = PROMPT_CHUNK
    hist = -(-(cwidth - 1) // SUBLANES) * SUBLANES
    assert T % tq == 0 and tq >= hist and hd == LANES

    idx = np.arange(tq, dtype=np.float64)
    diff = idx[:, None] - idx[None, :]
    lg = np.array([_log_gamma(h) for h in range(heads)])
    dec = np.where(diff[None] >= 0, np.exp(lg[:, None, None] * np.maximum(diff, 0.0)[None]), 0.0)
    rdec = np.broadcast_to(np.exp(lg[:, None] * (idx + 1.0))[:, :, None], (heads, tq, hd))
    kdec = np.broadcast_to(np.exp(lg[:, None] * (tq - 1.0 - idx))[:, :, None], (heads, tq, hd))
    dec, rdec, kdec = (jnp.asarray(t, F32) for t in (dec, rdec, kdec))

    kern = functools.partial(_prompt_mix_kernel, tq=tq, heads=heads, hd=hd, cdim=cdim,
                             cwidth=cwidth, hist=hist)
    return pl.pallas_call(
        kern,
        grid=(B, T // tq),
        in_specs=[
            pl.BlockSpec((1, tq, D), lambda b, c: (b, c, 0)),
            _const_spec((1, D)),
            _const_spec(w_in.shape),
            _const_spec(conv_w.shape),
            _const_spec(cvec.shape),
            _const_spec((1, rdim)),
            pl.BlockSpec((tq, hd), lambda b, c: (c, 0)),
            pl.BlockSpec((tq, hd), lambda b, c: (c, 0)),
            _const_spec(dec.shape),
            _const_spec(rdec.shape),
            _const_spec(kdec.shape),
        ],
        out_specs=[
            pl.BlockSpec((1, tq, D), lambda b, c: (b, c, 0)),
            pl.BlockSpec((1, cwidth - 1, cdim), lambda b, c: (b, 0, 0)),
            pl.BlockSpec((1, heads, hd, hd), lambda b, c: (b, 0, 0, 0)),
        ],
        out_shape=[
            jax.ShapeDtypeStruct((B, T, D), BF16),
            jax.ShapeDtypeStruct((B, cwidth - 1, cdim), F32),
            jax.ShapeDtypeStruct((B, heads, hd, hd), F32),
        ],
        scratch_shapes=[
            pltpu.VMEM((hist + tq, cdim), F32),
            pltpu.VMEM((tq, cdim), F32),
            pltpu.VMEM((tq, rdim), BF16),
            pltpu.VMEM((tq, rdim), F32),
            pltpu.VMEM((tq, rdim), BF16),
            pltpu.VMEM((tq, rdim), F32),
            pltpu.VMEM((heads, hd, hd), F32),
        ],
        compiler_params=pltpu.CompilerParams(
            dimension_semantics=("arbitrary", "arbitrary"), vmem_limit_bytes=VMEM_LIMIT),
        name="prompt_mix",
    )(x, g1, w_in, conv_w, cvec, gn_g, cos2, sin2, dec, rdec, kdec)


def _post_kernel(x_ref, mix_ref, wout_ref, g2_ref, wup_ref, wdn_ref, gf_ref, o_ref, *, final):
    y = x_ref[...] + jnp.dot(mix_ref[...], wout_ref[...], preferred_element_type=F32)
    h2 = _rmsnorm(y, g2_ref[...]).astype(BF16)
    hf = jnp.dot(h2, wup_ref[...], preferred_element_type=F32)
    act = jnp.square(jnp.maximum(hf, 0.0)).astype(BF16)
    out = y + jnp.dot(act, wdn_ref[...], preferred_element_type=F32)
    if final:
        out = _rmsnorm(out, gf_ref[...])
    o_ref[...] = out


def _post(x, mix, w_out, g2, w_up, w_down, gf, final):
    R, D = x.shape
    tr = min(POST_ROWS, R)
    assert R % tr == 0
    return pl.pallas_call(
        functools.partial(_post_kernel, final=final),
        grid=(R // tr,),
        in_specs=[
            pl.BlockSpec((tr, D), lambda i: (i, 0)),
            pl.BlockSpec((tr, D), lambda i: (i, 0)),
            _const_spec(w_out.shape),
            _const_spec((1, D)),
            _const_spec(w_up.shape),
            _const_spec(w_down.shape),
            _const_spec((1, D)),
        ],
        out_specs=pl.BlockSpec((tr, D), lambda i: (i, 0)),
        out_shape=jax.ShapeDtypeStruct((R, D), F32),
        compiler_params=pltpu.CompilerParams(
            dimension_semantics=("arbitrary",), vmem_limit_bytes=VMEM_LIMIT),
        name="post",
    )(x, mix, w_out, g2, w_up, w_down, gf)


def _in_proj_kernel(x_ref, g1_ref, w_ref, o_ref, h_s):
    @pl.when(pl.program_id(0) == 0)
    def _():
        h_s[...] = _rmsnorm(x_ref[...], g1_ref[...]).astype(BF16)

    o_ref[...] = jnp.dot(h_s[...], w_ref[...], preferred_element_type=F32)


def _in_proj(x, g1, w_in, ncol):
    R, D = x.shape
    C = w_in.shape[1]
    return pl.pallas_call(
        _in_proj_kernel,
        grid=(C // ncol,),
        in_specs=[
            _const_spec((R, D)),
            _const_spec((1, D)),
            pl.BlockSpec((D, ncol), lambda j: (0, j)),
        ],
        out_specs=pl.BlockSpec((R, ncol), lambda j: (0, j)),
        out_shape=jax.ShapeDtypeStruct((R, C), F32),
        scratch_shapes=[pltpu.VMEM((R, D), BF16)],
        compiler_params=pltpu.CompilerParams(
            dimension_semantics=("arbitrary",), vmem_limit_bytes=VMEM_LIMIT),
        name="sample_in_proj",
    )(x, g1, w_in)


def _sample_mix_kernel(proj_ref, cache_ref, st_ref, cw_ref, cvec_ref, gn_ref, cos_ref, sin_ref,
                       dec_ref, rdec_ref, kdec_ref,
                       mix_ref, nconv_ref, nst_ref, xp_s,
                       *, ts, heads, hd, cdim, cwidth):
    rdim = heads * hd
    nb = proj_ref.shape[0]

    glu = proj_ref[:, :, 0:cdim] * jax.nn.sigmoid(proj_ref[:, :, cdim:2 * cdim])
    xp_s[:, 0:cwidth - 1, :] = cache_ref[...]
    xp_s[:, cwidth - 1:cwidth - 1 + ts, :] = glu
    acc = xp_s[:, 0:ts, :] * cw_ref[0:1, :]
    for w in range(1, cwidth):
        acc = acc + xp_s[:, w:w + ts, :] * cw_ref[w:w + 1, :]
    cn = _layernorm(acc + cvec_ref[0:1, :]) * cvec_ref[1:2, :] + cvec_ref[2:3, :]
    mix_ref[:, :, 0:cdim] = _silu(cn).astype(BF16)
    nconv_ref[...] = xp_s[:, ts:ts + cwidth - 1, :]

    c0 = 2 * cdim
    cos2 = cos_ref[...]
    sin2 = sin_ref[...]
    scale = hd ** -0.5
    for hh in range(heads):
        lo = hh * hd
        q = _rotary(proj_ref[:, :, c0 + lo:c0 + lo + hd], cos2, sin2)
        k = _rotary(proj_ref[:, :, c0 + rdim + lo:c0 + rdim + lo + hd], cos2, sin2) * scale
        v = proj_ref[:, :, c0 + 2 * rdim + lo:c0 + 2 * rdim + lo + hd]
        g = proj_ref[:, :, c0 + 3 * rdim + lo:c0 + 3 * rdim + lo + hd]
        st = st_ref[:, hh]
        scores = jnp.einsum('btd,bjd->btj', q, k, preferred_element_type=F32) * dec_ref[hh]
        inner = jnp.einsum('btj,bjv->btv', scores, v, preferred_element_type=F32)
        cross = jnp.einsum('btd,bdv->btv', q, st, preferred_element_type=F32) * rdec_ref[hh]
        upd = jnp.einsum('bjd,bjv->bdv', k * kdec_ref[hh], v, preferred_element_type=F32)
        nst_ref[:, hh] = st * math.exp(_log_gamma(hh) * ts) + upd
        on = _layernorm(inner + cross)
        mix_ref[:, :, cdim + lo:cdim + lo + hd] = (on * gn_ref[0:1, lo:lo + hd] * _silu(g)).astype(BF16)


def _sample_mix(proj, cache, state, conv_w, cvec, gn_g, cos2, sin2):
    S, ts, C = proj.shape
    _, heads, hd, _ = state.shape
    cwidth, cdim = conv_w.shape
    rdim = heads * hd
    D = cdim + rdim
    nb = SAMPLE_SEQS
    assert S % nb == 0 and hd == LANES

    idx = np.arange(ts, dtype=np.float64)
    diff = idx[:, None] - idx[None, :]
    lg = np.array([_log_gamma(h) for h in range(heads)])
    dec = np.where(diff[None] >= 0, np.exp(lg[:, None, None] * np.maximum(diff, 0.0)[None]), 0.0)
    rdec = np.exp(lg[:, None] * (idx + 1.0))[:, :, None]
    kdec = np.exp(lg[:, None] * (ts - 1.0 - idx))[:, :, None]
    dec, rdec, kdec = (jnp.asarray(t, F32) for t in (dec, rdec, kdec))

    xp_rows = -(-(cwidth - 1 + ts) // SUBLANES) * SUBLANES
    kern = functools.partial(_sample_mix_kernel, ts=ts, heads=heads, hd=hd, cdim=cdim, cwidth=cwidth)
    return pl.pallas_call(
        kern,
        grid=(S // nb,),
        in_specs=[
            pl.BlockSpec((nb, ts, C), lambda i: (i, 0, 0)),
            pl.BlockSpec((nb, cwidth - 1, cdim), lambda i: (i, 0, 0)),
            pl.BlockSpec((nb, heads, hd, hd), lambda i: (i, 0, 0, 0)),
            _const_spec(conv_w.shape),
            _const_spec(cvec.shape),
            _const_spec((1, rdim)),
            _const_spec((ts, hd)),
            _const_spec((ts, hd)),
            _const_spec(dec.shape),
            _const_spec(rdec.shape),
            _const_spec(kdec.shape),
        ],
        out_specs=[
            pl.BlockSpec((nb, ts, D), lambda i: (i, 0, 0)),
            pl.BlockSpec((nb, cwidth - 1, cdim), lambda i: (i, 0, 0)),
            pl.BlockSpec((nb, heads, hd, hd), lambda i: (i, 0, 0, 0)),
        ],
        out_shape=[
            jax.ShapeDtypeStruct((S, ts, D), BF16),
            jax.ShapeDtypeStruct((S, cwidth - 1, cdim), F32),
            jax.ShapeDtypeStruct((S, heads, hd, hd), F32),
        ],
        scratch_shapes=[pltpu.VMEM((nb, xp_rows, cdim), F32)],
        compiler_params=pltpu.CompilerParams(
            dimension_semantics=("arbitrary",), vmem_limit_bytes=VMEM_LIMIT),
        name="sample_mix",
    )(proj, cache, state, conv_w, cvec, gn_g, cos2, sin2, dec, rdec, kdec)


def _rope_tables(pos0, T, hd):
    half = hd // 2
    pos = pos0 + jnp.arange(T, dtype=F32)
    inv = ROPE_BASE ** (-jnp.arange(half, dtype=F32) / half)
    ang = pos[:, None] * inv[None, :]
    cos, sin = jnp.cos(ang), jnp.sin(ang)
    return jnp.concatenate([cos, cos], axis=-1), jnp.concatenate([-sin, sin], axis=-1)


def kernel(x_prompt, x_sample, cache_conv, state_ret, norm1_g, w_in, conv_w, conv_b, conv_ln_g,
           conv_ln_b, ret_gn_g, w_out, norm2_g, w_up, w_down, final_norm_g):
    depth = w_in.shape[0]
    B, T, D = x_prompt.shape
    S, ts, _ = x_sample.shape
    heads, hd = state_ret.shape[2], state_ret.shape[3]

    w_in_b, w_out_b, w_up_b, w_dn_b = (w.astype(BF16) for w in (w_in, w_out, w_up, w_down))
    cvec = jnp.stack([conv_b, conv_ln_g, conv_ln_b], axis=1)
    gf = final_norm_g[None, :]
    cos_p, sin_p = _rope_tables(0.0, T, hd)
    cos_s, sin_s = _rope_tables(float(PAST_LEN), ts, hd)

    xp = x_prompt
    xs = x_sample.reshape(S * ts, D)
    conv_p, ret_p, conv_s, ret_s = [], [], [], []
    for l in range(depth):
        final = l == depth - 1
        g1, g2, gn = norm1_g[l][None, :], norm2_g[l][None, :], ret_gn_g[l][None, :]

        mix, nconv, nret = _prompt_mix(xp, g1, w_in_b[l], conv_w[l], cvec[l], gn, cos_p, sin_p, heads, hd)
        xp = _post(xp.reshape(B * T, D), mix.reshape(B * T, D), w_out_b[l], g2, w_up_b[l], w_dn_b[l],
                   gf, final).reshape(B, T, D)
        conv_p.append(nconv)
        ret_p.append(nret)

        proj = _in_proj(xs, g1, w_in_b[l], 512).reshape(S, ts, -1)
        mix, nconv, nret = _sample_mix(proj, cache_conv[l], state_ret[l], conv_w[l], cvec[l], gn,
                                       cos_s, sin_s)
        xs = _post(xs, mix.reshape(S * ts, D), w_out_b[l], g2, w_up_b[l], w_dn_b[l], gf, final)
        conv_s.append(nconv)
        ret_s.append(nret)

    return (xp, xs.reshape(S, ts, D), jnp.stack(conv_p), jnp.stack(ret_p),
            jnp.stack(conv_s), jnp.stack(ret_s))
```

```python
import functools
import math

import numpy as np
import jax
import jax.numpy as jnp
from jax import lax
from jax.experimental import pallas as pl
from jax.experimental.pallas import tpu as pltpu

F32 = jnp.float32
BF16 = jnp.bfloat16

EPS = 1e-6
ROPE_BASE = 10000.0
PAST_LEN = 16384
LANES = 128
SUBLANES = 8
VMEM_LIMIT = 56 * 1024 * 1024

PROMPT_CHUNK = 256
CONV_ROWS = 128
POST_ROWS = 256
SAMPLE_SEQS = 8


def _log_gamma(h):
    return math.log(1.0 - 2.0 ** (-5.0 - h))


def _rmsnorm(x, g):
    ms = jnp.mean(x * x, axis=-1, keepdims=True)
    return x * lax.rsqrt(ms + EPS) * g


def _layernorm(x):
    mu = jnp.mean(x, axis=-1, keepdims=True)
    xc = x - mu
    var = jnp.mean(xc * xc, axis=-1, keepdims=True)
    return xc * lax.rsqrt(var + EPS)


def _silu(x):
    return x * jax.nn.sigmoid(x)


def _rotary(t, cos2, sin2):
    half = t.shape[-1] // 2
    return t * cos2 + pltpu.roll(t, half, t.ndim - 1) * sin2


def _const_spec(shape):
    nd = len(shape)
    return pl.BlockSpec(shape, lambda *_: (0,) * nd, pipeline_mode=pl.Buffered(1))


def _prompt_mix_kernel(x_ref, g1_ref, win_ref, cw_ref, cvec_ref, gn_ref, cos_ref, sin_ref,
                       dec_ref, rdec_ref, kdec_ref,
                       mix_ref, nconv_ref, nret_ref,
                       glu_s, shift_s, conv_s, q_s, k_s, v_s, g_s, state_s,
                       *, tq, heads, hd, cdim, cwidth, hist):
    c = pl.program_id(1)
    last = pl.num_programs(1) - 1
    rdim = heads * hd

    @pl.when(c == 0)
    def _():
        glu_s[0:hist, :] = jnp.zeros((hist, cdim), F32)
        shift_s[:, 0:hist, :] = jnp.zeros((SUBLANES - 1, hist, cdim), F32)
        shift_s[:, tq:tq + hist, :] = jnp.zeros((SUBLANES - 1, hist, cdim), F32)
        state_s[...] = jnp.zeros_like(state_s)

    @pl.when(c > 0)
    def _():
        glu_s[0:hist, :] = glu_s[tq:tq + hist, :]
        shift_s[:, 0:hist, :] = shift_s[:, tq:tq + hist, :]

    h = _rmsnorm(x_ref[0], g1_ref[...]).astype(BF16)

    def proj(lo, n):
        return jnp.dot(h, win_ref[:, lo:lo + n], preferred_element_type=F32)

    glu = proj(0, cdim) * jax.nn.sigmoid(proj(cdim, cdim))
    glu_s[hist:hist + tq, :] = glu
    for ph in range(1, SUBLANES):
        shift_s[ph - 1, hist - ph:hist - ph + tq, :] = glu
    off = hist - (cwidth - 1)
    for lb in range(cdim // LANES):
        ls = slice(lb * LANES, (lb + 1) * LANES)
        for r0 in range(0, tq, CONV_ROWS):
            acc = None
            for w in range(cwidth):
                ph = (off + w) % SUBLANES
                a = r0 + (off + w) // SUBLANES * SUBLANES
                xs = glu_s[a:a + CONV_ROWS, ls] if ph == 0 else shift_s[ph - 1, a:a + CONV_ROWS, ls]
                term = xs * cw_ref[w:w + 1, ls]
                acc = term if acc is None else acc + term
            conv_s[r0:r0 + CONV_ROWS, ls] = acc + cvec_ref[0:1, ls]
    cn = _layernorm(conv_s[...]) * cvec_ref[1:2, :] + cvec_ref[2:3, :]
    mix_ref[0, :, 0:cdim] = _silu(cn).astype(BF16)

    c0 = 2 * cdim
    qf = proj(c0, rdim)
    kf = proj(c0 + rdim, rdim)
    cos2 = cos_ref[...]
    sin2 = sin_ref[...]
    scale = hd ** -0.5
    for hh in range(heads):
        hs = slice(hh * hd, (hh + 1) * hd)
        q_s[:, hs] = _rotary(qf[:, hs], cos2, sin2).astype(BF16)
        k_s[:, hs] = _rotary(kf[:, hs], cos2, sin2) * scale
    v_s[...] = proj(c0 + 2 * rdim, rdim).astype(BF16)
    g_s[...] = _silu(proj(c0 + 3 * rdim, rdim))

    for hh in range(heads):
        hs = slice(hh * hd, (hh + 1) * hd)
        qh = q_s[:, hs]
        kh = k_s[:, hs]
        vh = v_s[:, hs]
        st = state_s[hh]
        scores = lax.dot_general(qh, kh.astype(BF16), (((1,), (1,)), ((), ())),
                                 preferred_element_type=F32) * dec_ref[hh]
        inner = jnp.dot(scores.astype(BF16), vh, preferred_element_type=F32)
        cross = jnp.dot(qh, st.astype(BF16), preferred_element_type=F32) * rdec_ref[hh]
        kd = (kh * kdec_ref[hh]).astype(BF16)
        upd = lax.dot_general(kd, vh, (((0,), (0,)), ((), ())), preferred_element_type=F32)
        state_s[hh] = st * math.exp(_log_gamma(hh) * tq) + upd
        on = _layernorm(inner + cross)
        mix_ref[0, :, cdim + hh * hd:cdim + (hh + 1) * hd] = (on * gn_ref[0:1, hs] * g_s[:, hs]).astype(BF16)

    @pl.when(c == last)
    def _():
        nconv_ref[0] = glu_s[hist + tq - (cwidth - 1):hist + tq, :]
        nret_ref[0] = state_s[...]


def _prompt_mix(x, g1, w_in, conv_w, cvec, gn_g, cos2, sin2, heads, hd):
    B, T, D = x.shape
    cwidth, cdim = conv_w.shape
    rdim = heads * hd
    tq = PROMPT_CHUNK
    hist = -(-(cwidth - 1) // SUBLANES) * SUBLANES
    assert T % tq == 0 and tq >= hist and hd == LANES

    idx = np.arange(tq, dtype=np.float64)
    diff = idx[:, None] - idx[None, :]
    lg = np.array([_log_gamma(h) for h in range(heads)])
    dec = np.where(diff[None] >= 0, np.exp(lg[:, None, None] * np.maximum(diff, 0.0)[None]), 0.0)
    rdec = np.broadcast_to(np.exp(lg[:, None] * (idx + 1.0))[:, :, None], (heads, tq, hd))
    kdec = np.broadcast_to(np.exp(lg[:, None] * (tq - 1.0 - idx))[:, :, None], (heads, tq, hd))
    dec, rdec, kdec = (jnp.asarray(t, F32) for t in (dec, rdec, kdec))

    kern = functools.partial(_prompt_mix_kernel, tq=tq, heads=heads, hd=hd, cdim=cdim,
                             cwidth=cwidth, hist=hist)
    return pl.pallas_call(
        kern,
        grid=(B, T // tq),
        in_specs=[
            pl.BlockSpec((1, tq, D), lambda b, c: (b, c, 0)),
            _const_spec((1, D)),
            _const_spec(w_in.shape),
            _const_spec(conv_w.shape),
            _const_spec(cvec.shape),
            _const_spec((1, rdim)),
            pl.BlockSpec((tq, hd), lambda b, c: (c, 0)),
            pl.BlockSpec((tq, hd), lambda b, c: (c, 0)),
            _const_spec(dec.shape),
            _const_spec(rdec.shape),
            _const_spec(kdec.shape),
        ],
        out_specs=[
            pl.BlockSpec((1, tq, D), lambda b, c: (b, c, 0)),
            pl.BlockSpec((1, cwidth - 1, cdim), lambda b, c: (b, 0, 0)),
            pl.BlockSpec((1, heads, hd, hd), lambda b, c: (b, 0, 0, 0)),
        ],
        out_shape=[
            jax.ShapeDtypeStruct((B, T, D), BF16),
            jax.ShapeDtypeStruct((B, cwidth - 1, cdim), F32),
            jax.ShapeDtypeStruct((B, heads, hd, hd), F32),
        ],
        scratch_shapes=[
            pltpu.VMEM((hist + tq, cdim), F32),
            pltpu.VMEM((SUBLANES - 1, hist + tq, cdim), F32),
            pltpu.VMEM((tq, cdim), F32),
            pltpu.VMEM((tq, rdim), BF16),
            pltpu.VMEM((tq, rdim), F32),
            pltpu.VMEM((tq, rdim), BF16),
            pltpu.VMEM((tq, rdim), F32),
            pltpu.VMEM((heads, hd, hd), F32),
        ],
        compiler_params=pltpu.CompilerParams(
            dimension_semantics=("arbitrary", "arbitrary"), vmem_limit_bytes=VMEM_LIMIT),
        name="prompt_mix",
    )(x, g1, w_in, conv_w, cvec, gn_g, cos2, sin2, dec, rdec, kdec)


def _post_kernel(x_ref, mix_ref, wout_ref, g2_ref, wup_ref, wdn_ref, gf_ref, o_ref, *, final):
    y = x_ref[...] + jnp.dot(mix_ref[...], wout_ref[...], preferred_element_type=F32)
    h2 = _rmsnorm(y, g2_ref[...]).astype(BF16)
    hf = jnp.dot(h2, wup_ref[...], preferred_element_type=F32)
    act = jnp.square(jnp.maximum(hf, 0.0)).astype(BF16)
    out = y + jnp.dot(act, wdn_ref[...], preferred_element_type=F32)
    if final:
        out = _rmsnorm(out, gf_ref[...])
    o_ref[...] = out


def _post(x, mix, w_out, g2, w_up, w_down, gf, final):
    R, D = x.shape
    tr = min(POST_ROWS, R)
    assert R % tr == 0
    return pl.pallas_call(
        functools.partial(_post_kernel, final=final),
        grid=(R // tr,),
        in_specs=[
            pl.BlockSpec((tr, D), lambda i: (i, 0)),
            pl.BlockSpec((tr, D), lambda i: (i, 0)),
            _const_spec(w_out.shape),
            _const_spec((1, D)),
            _const_spec(w_up.shape),
            _const_spec(w_down.shape),
            _const_spec((1, D)),
        ],
        out_specs=pl.BlockSpec((tr, D), lambda i: (i, 0)),
        out_shape=jax.ShapeDtypeStruct((R, D), F32),
        compiler_params=pltpu.CompilerParams(
            dimension_semantics=("arbitrary",), vmem_limit_bytes=VMEM_LIMIT),
        name="post",
    )(x, mix, w_out, g2, w_up, w_down, gf)


def _in_proj_kernel(x_ref, g1_ref, w_ref, o_ref, h_s):
    @pl.when(pl.program_id(0) == 0)
    def _():
        h_s[...] = _rmsnorm(x_ref[...], g1_ref[...]).astype(BF16)

    o_ref[...] = jnp.dot(h_s[...], w_ref[...], preferred_element_type=F32)


def _in_proj(x, g1, w_in, ncol):
    R, D = x.shape
    C = w_in.shape[1]
    return pl.pallas_call(
        _in_proj_kernel,
        grid=(C // ncol,),
        in_specs=[
            _const_spec((R, D)),
            _const_spec((1, D)),
            pl.BlockSpec((D, ncol), lambda j: (0, j)),
        ],
        out_specs=pl.BlockSpec((R, ncol), lambda j: (0, j)),
        out_shape=jax.ShapeDtypeStruct((R, C), F32),
        scratch_shapes=[pltpu.VMEM((R, D), BF16)],
        compiler_params=pltpu.CompilerParams(
            dimension_semantics=("arbitrary",), vmem_limit_bytes=VMEM_LIMIT),
        name="sample_in_proj",
    )(x, g1, w_in)


def _sample_mix_kernel(proj_ref, cache_ref, st_ref, cw_ref, cvec_ref, gn_ref, cos_ref, sin_ref,
                       dec_ref, rdec_ref, kdec_ref, *rest, ts, heads, hd, cdim, cwidth):
    mix_ref, nconv_ref, nst_ref, xp_s = rest[-4:]
    rdim = heads * hd

    glu = proj_ref[:, :, 0:cdim] * jax.nn.sigmoid(proj_ref[:, :, cdim:2 * cdim])
    xp_s[:, 0:cwidth - 1, :] = cache_ref[...]
    xp_s[:, cwidth - 1:cwidth - 1 + ts, :] = glu
    acc = xp_s[:, 0:ts, :] * cw_ref[0:1, :]
    for w in range(1, cwidth):
        acc = acc + xp_s[:, w:w + ts, :] * cw_ref[w:w + 1, :]
    cn = _layernorm(acc + cvec_ref[0:1, :]) * cvec_ref[1:2, :] + cvec_ref[2:3, :]
    mix_ref[:, :, 0:cdim] = _silu(cn).astype(BF16)
    nconv_ref[...] = xp_s[:, ts:ts + cwidth - 1, :]

    c0 = 2 * cdim
    cos2 = cos_ref[...]
    sin2 = sin_ref[...]
    scale = hd ** -0.5
    for hh in range(heads):
        lo = hh * hd
        q = _rotary(proj_ref[:, :, c0 + lo:c0 + lo + hd], cos2, sin2)
        k = _rotary(proj_ref[:, :, c0 + rdim + lo:c0 + rdim + lo + hd], cos2, sin2) * scale
        v = proj_ref[:, :, c0 + 2 * rdim + lo:c0 + 2 * rdim + lo + hd]
        g = proj_ref[:, :, c0 + 3 * rdim + lo:c0 + 3 * rdim + lo + hd]
        st = st_ref[:, hh]
        scores = jnp.einsum('btd,bjd->btj', q, k, preferred_element_type=F32) * dec_ref[hh]
        inner = jnp.einsum('btj,bjv->btv', scores, v, preferred_element_type=F32)
        cross = jnp.einsum('btd,bdv->btv', q, st, preferred_element_type=F32) * rdec_ref[hh]
        upd = jnp.einsum('bjd,bjv->bdv', k * kdec_ref[hh], v, preferred_element_type=F32)
        nst_ref[:, hh] = st * math.exp(_log_gamma(hh) * ts) + upd
        on = _layernorm(inner + cross)
        mix_ref[:, :, cdim + lo:cdim + lo + hd] = (on * gn_ref[0:1, lo:lo + hd] * _silu(g)).astype(BF16)


def _sample_mix(proj, cache, state, layer, prev, conv_w, cvec, gn_g, cos2, sin2):
    S, ts, C = proj.shape
    depth, _, heads, hd, _ = state.shape
    cwidth, cdim = conv_w.shape
    rdim = heads * hd
    D = cdim + rdim
    nb = SAMPLE_SEQS
    assert S % nb == 0 and hd == LANES

    idx = np.arange(ts, dtype=np.float64)
    diff = idx[:, None] - idx[None, :]
    lg = np.array([_log_gamma(h) for h in range(heads)])
    dec = np.where(diff[None] >= 0, np.exp(lg[:, None, None] * np.maximum(diff, 0.0)[None]), 0.0)
    rdec = np.exp(lg[:, None] * (idx + 1.0))[:, :, None]
    kdec = np.exp(lg[:, None] * (ts - 1.0 - idx))[:, :, None]
    dec, rdec, kdec = (jnp.asarray(t, F32) for t in (dec, rdec, kdec))

    conv_spec = pl.BlockSpec((None, nb, cwidth - 1, cdim), lambda i: (layer, i, 0, 0))
    st_spec = pl.BlockSpec((None, nb, heads, hd, hd), lambda i: (layer, i, 0, 0, 0))
    args = [proj, cache, state, conv_w, cvec, gn_g, cos2, sin2, dec, rdec, kdec]
    in_specs = [
        pl.BlockSpec((nb, ts, C), lambda i: (i, 0, 0)),
        conv_spec,
        st_spec,
        _const_spec(conv_w.shape),
        _const_spec(cvec.shape),
        _const_spec((1, rdim)),
        _const_spec((ts, hd)),
        _const_spec((ts, hd)),
        _const_spec(dec.shape),
        _const_spec(rdec.shape),
        _const_spec(kdec.shape),
    ]
    aliases = {}
    if prev is not None:
        aliases = {len(args): 1, len(args) + 1: 2}
        args += list(prev)
        in_specs += [pl.BlockSpec(memory_space=pl.ANY)] * 2

    xp_rows = -(-(cwidth - 1 + ts) // SUBLANES) * SUBLANES
    kern = functools.partial(_sample_mix_kernel, ts=ts, heads=heads, hd=hd, cdim=cdim, cwidth=cwidth)
    return pl.pallas_call(
        kern,
        grid=(S // nb,),
        in_specs=in_specs,
        out_specs=[pl.BlockSpec((nb, ts, D), lambda i: (i, 0, 0)), conv_spec, st_spec],
        out_shape=[
            jax.ShapeDtypeStruct((S, ts, D), BF16),
            jax.ShapeDtypeStruct((depth, S, cwidth - 1, cdim), F32),
            jax.ShapeDtypeStruct((depth, S, heads, hd, hd), F32),
        ],
        scratch_shapes=[pltpu.VMEM((nb, xp_rows, cdim), F32)],
        input_output_aliases=aliases,
        compiler_params=pltpu.CompilerParams(
            dimension_semantics=("arbitrary",), vmem_limit_bytes=VMEM_LIMIT),
        name="sample_mix",
    )(*args)


def _rope_tables(pos0, T, hd):
    half = hd // 2
    pos = pos0 + jnp.arange(T, dtype=F32)
    inv = ROPE_BASE ** (-jnp.arange(half, dtype=F32) / half)
    ang = pos[:, None] * inv[None, :]
    cos, sin = jnp.cos(ang), jnp.sin(ang)
    return jnp.concatenate([cos, cos], axis=-1), jnp.concatenate([-sin, sin], axis=-1)


def kernel(x_prompt, x_sample, cache_conv, state_ret, norm1_g, w_in, conv_w, conv_b, conv_ln_g,
           conv_ln_b, ret_gn_g, w_out, norm2_g, w_up, w_down, final_norm_g):
    depth = w_in.shape[0]
    B, T, D = x_prompt.shape
    S, ts, _ = x_sample.shape
    heads, hd = state_ret.shape[2], state_ret.shape[3]

    w_in_b, w_out_b, w_up_b, w_dn_b = (w.astype(BF16) for w in (w_in, w_out, w_up, w_down))
    cvec = jnp.stack([conv_b, conv_ln_g, conv_ln_b], axis=1)
    gf = final_norm_g[None, :]
    cos_p, sin_p = _rope_tables(0.0, T, hd)
    cos_s, sin_s = _rope_tables(float(PAST_LEN), ts, hd)

    xp = x_prompt
    xs = x_sample.reshape(S * ts, D)
    conv_p, ret_p, sample_state = [], [], None
    for l in range(depth):
        final = l == depth - 1
        g1, g2, gn = norm1_g[l][None, :], norm2_g[l][None, :], ret_gn_g[l][None, :]

        mix, nconv, nret = _prompt_mix(xp, g1, w_in_b[l], conv_w[l], cvec[l], gn, cos_p, sin_p, heads, hd)
        xp = _post(xp.reshape(B * T, D), mix.reshape(B * T, D), w_out_b[l], g2, w_up_b[l], w_dn_b[l],
                   gf, final).reshape(B, T, D)
        conv_p.append(nconv)
        ret_p.append(nret)

        proj = _in_proj(xs, g1, w_in_b[l], 512).reshape(S, ts, -1)
        mix, *sample_state = _sample_mix(proj, cache_conv, state_ret, l, sample_state, conv_w[l], cvec[l],
                                         gn, cos_s, sin_s)
        xs = _post(xs, mix.reshape(S * ts, D), w_out_b[l], g2, w_up_b[l], w_dn_b[l], gf, final)

    return (xp, xs.reshape(S, ts, D), jnp.stack(conv_p), jnp.stack(ret_p), *sample_state)
```

```python
import functools
import math

import numpy as np
import jax
import jax.numpy as jnp
from jax import lax
from jax.experimental import pallas as pl
from jax.experimental.pallas import tpu as pltpu

F32 = jnp.float32
BF16 = jnp.bfloat16

EPS = 1e-6
ROPE_BASE = 10000.0
PAST_LEN = 16384
LANES = 128
SUBLANES = 8
VMEM_LIMIT = 56 * 1024 * 1024

PROMPT_CHUNK = 256
CONV_ROWS = 128
POST_ROWS = 256
SAMPLE_SEQS = 8


def _log_gamma(h):
    return math.log(1.0 - 2.0 ** (-5.0 - h))


def _rmsnorm(x, g):
    ms = jnp.mean(x * x, axis=-1, keepdims=True)
    return x * lax.rsqrt(ms + EPS) * g


def _layernorm(x):
    mu = jnp.mean(x, axis=-1, keepdims=True)
    xc = x - mu
    var = jnp.mean(xc * xc, axis=-1, keepdims=True)
    return xc * lax.rsqrt(var + EPS)


def _silu(x):
    return x * jax.nn.sigmoid(x)


def _rotary(t, cos2, sin2):
    half = t.shape[-1] // 2
    return t * cos2 + pltpu.roll(t, half, t.ndim - 1) * sin2


def _const_spec(shape):
    nd = len(shape)
    return pl.BlockSpec(shape, lambda *_: (0,) * nd, pipeline_mode=pl.Buffered(1))


def _decay_tables(n, heads):
    idx = np.arange(n, dtype=np.float64)
    diff = idx[:, None] - idx[None, :]
    lg = np.array([_log_gamma(h) for h in range(heads)])
    dec = np.where(diff[None] >= 0, np.exp(lg[:, None, None] * np.maximum(diff, 0.0)[None]), 0.0)
    rdec = np.exp(lg[:, None] * (idx + 1.0))[:, :, None]
    kdec = np.exp(lg[:, None] * (n - 1.0 - idx))[:, :, None]
    return dec, rdec, kdec


def _prompt_mix_kernel(x_ref, g1_ref, win_ref, cw_ref, cvec_ref, gn_ref, cos_ref, sin_ref,
                       dec_ref, rdec_ref, kdec_ref,
                       mix_ref, nconv_ref, nret_ref,
                       glu_s, shift_s, conv_s, q_s, k_s, v_s, g_s, state_s,
                       *, tq, heads, hd, cdim, cwidth, hist):
    c = pl.program_id(1)
    last = pl.num_programs(1) - 1
    rdim = heads * hd

    @pl.when(c == 0)
    def _():
        glu_s[0:hist, :] = jnp.zeros((hist, cdim), F32)
        shift_s[:, 0:hist, :] = jnp.zeros((SUBLANES - 1, hist, cdim), F32)
        shift_s[:, tq:tq + hist, :] = jnp.zeros((SUBLANES - 1, hist, cdim), F32)
        state_s[...] = jnp.zeros_like(state_s)

    @pl.when(c > 0)
    def _():
        glu_s[0:hist, :] = glu_s[tq:tq + hist, :]
        shift_s[:, 0:hist, :] = shift_s[:, tq:tq + hist, :]

    h = _rmsnorm(x_ref[0], g1_ref[...]).astype(BF16)

    def proj(lo, n):
        return jnp.dot(h, win_ref[:, lo:lo + n], preferred_element_type=F32)

    glu = proj(0, cdim) * jax.nn.sigmoid(proj(cdim, cdim))
    glu_s[hist:hist + tq, :] = glu
    for ph in range(1, SUBLANES):
        shift_s[ph - 1, hist - ph:hist - ph + tq, :] = glu
    off = hist - (cwidth - 1)
    for lb in range(cdim // LANES):
        ls = slice(lb * LANES, (lb + 1) * LANES)
        for r0 in range(0, tq, CONV_ROWS):
            acc = None
            for w in range(cwidth):
                ph = (off + w) % SUBLANES
                a = r0 + (off + w) // SUBLANES * SUBLANES
                xs = glu_s[a:a + CONV_ROWS, ls] if ph == 0 else shift_s[ph - 1, a:a + CONV_ROWS, ls]
                term = xs * cw_ref[w:w + 1, ls]
                acc = term if acc is None else acc + term
            conv_s[r0:r0 + CONV_ROWS, ls] = acc + cvec_ref[0:1, ls]
    cn = _layernorm(conv_s[...]) * cvec_ref[1:2, :] + cvec_ref[2:3, :]
    mix_ref[0, :, 0:cdim] = _silu(cn).astype(BF16)

    c0 = 2 * cdim
    qf = proj(c0, rdim)
    kf = proj(c0 + rdim, rdim)
    cos2 = cos_ref[...]
    sin2 = sin_ref[...]
    scale = hd ** -0.5
    for hh in range(heads):
        hs = slice(hh * hd, (hh + 1) * hd)
        q_s[:, hs] = _rotary(qf[:, hs], cos2, sin2).astype(BF16)
        k_s[:, hs] = _rotary(kf[:, hs], cos2, sin2) * scale
    v_s[...] = proj(c0 + 2 * rdim, rdim).astype(BF16)
    g_s[...] = _silu(proj(c0 + 3 * rdim, rdim))

    for hh in range(heads):
        hs = slice(hh * hd, (hh + 1) * hd)
        qh = q_s[:, hs]
        kh = k_s[:, hs]
        vh = v_s[:, hs]
        st = state_s[hh]
        scores = lax.dot_general(qh, kh.astype(BF16), (((1,), (1,)), ((), ())),
                                 preferred_element_type=F32) * dec_ref[hh]
        inner = jnp.dot(scores.astype(BF16), vh, preferred_element_type=F32)
        cross = jnp.dot(qh, st.astype(BF16), preferred_element_type=F32) * rdec_ref[hh]
        kd = (kh * kdec_ref[hh]).astype(BF16)
        upd = lax.dot_general(kd, vh, (((0,), (0,)), ((), ())), preferred_element_type=F32)
        state_s[hh] = st * math.exp(_log_gamma(hh) * tq) + upd
        on = _layernorm(inner + cross)
        mix_ref[0, :, cdim + hh * hd:cdim + (hh + 1) * hd] = (on * gn_ref[0:1, hs] * g_s[:, hs]).astype(BF16)

    @pl.when(c == last)
    def _():
        nconv_ref[0] = glu_s[hist + tq - (cwidth - 1):hist + tq, :]
        nret_ref[0] = state_s[...]


def _prompt_mix(x, g1, w_in, conv_w, cvec, gn_g, cos2, sin2, heads, hd):
    B, T, D = x.shape
    cwidth, cdim = conv_w.shape
    rdim = heads * hd
    tq = PROMPT_CHUNK
    hist = -(-(cwidth - 1) // SUBLANES) * SUBLANES
    assert T % tq == 0 and tq >= hist and tq % CONV_ROWS == 0 and hd == LANES

    dec, rdec, kdec = _decay_tables(tq, heads)
    dec = jnp.asarray(dec, F32)
    rdec = jnp.asarray(np.broadcast_to(rdec, (heads, tq, hd)), F32)
    kdec = jnp.asarray(np.broadcast_to(kdec, (heads, tq, hd)), F32)

    kern = functools.partial(_prompt_mix_kernel, tq=tq, heads=heads, hd=hd, cdim=cdim,
                             cwidth=cwidth, hist=hist)
    return pl.pallas_call(
        kern,
        grid=(B, T // tq),
        in_specs=[
            pl.BlockSpec((1, tq, D), lambda b, c: (b, c, 0)),
            _const_spec((1, D)),
            _const_spec(w_in.shape),
            _const_spec(conv_w.shape),
            _const_spec(cvec.shape),
            _const_spec((1, rdim)),
            pl.BlockSpec((tq, hd), lambda b, c: (c, 0)),
            pl.BlockSpec((tq, hd), lambda b, c: (c, 0)),
            _const_spec(dec.shape),
            _const_spec(rdec.shape),
            _const_spec(kdec.shape),
        ],
        out_specs=[
            pl.BlockSpec((1, tq, D), lambda b, c: (b, c, 0)),
            pl.BlockSpec((1, cwidth - 1, cdim), lambda b, c: (b, 0, 0)),
            pl.BlockSpec((1, heads, hd, hd), lambda b, c: (b, 0, 0, 0)),
        ],
        out_shape=[
            jax.ShapeDtypeStruct((B, T, D), BF16),
            jax.ShapeDtypeStruct((B, cwidth - 1, cdim), F32),
            jax.ShapeDtypeStruct((B, heads, hd, hd), F32),
        ],
        scratch_shapes=[
            pltpu.VMEM((hist + tq, cdim), F32),
            pltpu.VMEM((SUBLANES - 1, hist + tq, cdim), F32),
            pltpu.VMEM((tq, cdim), F32),
            pltpu.VMEM((tq, rdim), BF16),
            pltpu.VMEM((tq, rdim), F32),
            pltpu.VMEM((tq, rdim), BF16),
            pltpu.VMEM((tq, rdim), F32),
            pltpu.VMEM((heads, hd, hd), F32),
        ],
        compiler_params=pltpu.CompilerParams(
            dimension_semantics=("arbitrary", "arbitrary"), vmem_limit_bytes=VMEM_LIMIT),
        name="prompt_mix",
    )(x, g1, w_in, conv_w, cvec, gn_g, cos2, sin2, dec, rdec, kdec)


def _post_kernel(x_ref, mix_ref, wout_ref, g2_ref, wup_ref, wdn_ref, gf_ref, o_ref, *, final):
    y = x_ref[...] + jnp.dot(mix_ref[...], wout_ref[...], preferred_element_type=F32)
    h2 = _rmsnorm(y, g2_ref[...]).astype(BF16)
    hf = jnp.dot(h2, wup_ref[...], preferred_element_type=F32)
    act = jnp.square(jnp.maximum(hf, 0.0)).astype(BF16)
    out = y + jnp.dot(act, wdn_ref[...], preferred_element_type=F32)
    if final:
        out = _rmsnorm(out, gf_ref[...])
    o_ref[...] = out


def _post(x, mix, w_out, g2, w_up, w_down, gf, final):
    R, D = x.shape
    tr = min(POST_ROWS, R)
    assert R % tr == 0
    return pl.pallas_call(
        functools.partial(_post_kernel, final=final),
        grid=(R // tr,),
        in_specs=[
            pl.BlockSpec((tr, D), lambda i: (i, 0)),
            pl.BlockSpec((tr, D), lambda i: (i, 0)),
            _const_spec(w_out.shape),
            _const_spec((1, D)),
            _const_spec(w_up.shape),
            _const_spec(w_down.shape),
            _const_spec((1, D)),
        ],
        out_specs=pl.BlockSpec((tr, D), lambda i: (i, 0)),
        out_shape=jax.ShapeDtypeStruct((R, D), F32),
        compiler_params=pltpu.CompilerParams(
            dimension_semantics=("arbitrary",), vmem_limit_bytes=VMEM_LIMIT),
        name="post",
    )(x, mix, w_out, g2, w_up, w_down, gf)


def _in_proj_kernel(x_ref, g1_ref, w_ref, o_ref, h_s):
    @pl.when(pl.program_id(0) == 0)
    def _():
        h_s[...] = _rmsnorm(x_ref[...], g1_ref[...]).astype(BF16)

    o_ref[...] = jnp.dot(h_s[...], w_ref[...], preferred_element_type=F32)


def _in_proj(x, g1, w_in, ncol):
    R, D = x.shape
    C = w_in.shape[1]
    return pl.pallas_call(
        _in_proj_kernel,
        grid=(C // ncol,),
        in_specs=[
            _const_spec((R, D)),
            _const_spec((1, D)),
            pl.BlockSpec((D, ncol), lambda j: (0, j)),
        ],
        out_specs=pl.BlockSpec((R, ncol), lambda j: (0, j)),
        out_shape=jax.ShapeDtypeStruct((R, C), F32),
        scratch_shapes=[pltpu.VMEM((R, D), BF16)],
        compiler_params=pltpu.CompilerParams(
            dimension_semantics=("arbitrary",), vmem_limit_bytes=VMEM_LIMIT),
        name="sample_in_proj",
    )(x, g1, w_in)


def _sample_mix_kernel(proj_ref, cache_ref, st_ref, cw_ref, cvec_ref, gn_ref, cos_ref, sin_ref,
                       dec_ref, rdec_ref, kdec_ref, *rest, ts, nb, heads, hd, cdim, cwidth):
    mix_ref, ncache_ref, nst_ref = rest[-3:]
    rdim = heads * hd

    glu = [proj_ref[t, :, 0:cdim] * jax.nn.sigmoid(proj_ref[t, :, cdim:2 * cdim]) for t in range(ts)]

    def window(r):
        return cache_ref[r] if r < cwidth - 1 else glu[r - (cwidth - 1)]

    acc = [None] * ts
    for r in range(cwidth - 1 + ts):
        x = window(r)
        for t in range(max(0, r - cwidth + 1), min(ts, r + 1)):
            term = x * cw_ref[r - t:r - t + 1, :]
            acc[t] = term if acc[t] is None else acc[t] + term
    for t in range(ts):
        cn = _layernorm(acc[t] + cvec_ref[0:1, :]) * cvec_ref[1:2, :] + cvec_ref[2:3, :]
        mix_ref[t, :, 0:cdim] = _silu(cn).astype(BF16)
    for r in range(cwidth - 1):
        ncache_ref[r] = window(r + ts)

    c0 = 2 * cdim
    cos2 = cos_ref[...]
    sin2 = sin_ref[...]
    scale = hd ** -0.5
    for hh in range(heads):
        lo = hh * hd

        def per_seq(base):
            return jnp.stack([proj_ref[:, s, base + lo:base + lo + hd] for s in range(nb)], axis=0)

        q = _rotary(per_seq(c0), cos2, sin2)
        k = _rotary(per_seq(c0 + rdim), cos2, sin2) * scale
        v = per_seq(c0 + 2 * rdim)
        g = per_seq(c0 + 3 * rdim)
        st = st_ref[:, hh]
        scores = jnp.einsum('btd,bjd->btj', q, k, preferred_element_type=F32) * dec_ref[hh]
        inner = jnp.einsum('btj,bjv->btv', scores, v, preferred_element_type=F32)
        cross = jnp.einsum('btd,bdv->btv', q, st, preferred_element_type=F32) * rdec_ref[hh]
        upd = jnp.einsum('bjd,bjv->bdv', k * kdec_ref[hh], v, preferred_element_type=F32)
        nst_ref[:, hh] = st * math.exp(_log_gamma(hh) * ts) + upd
        on = _layernorm(inner + cross)
        out = (on * gn_ref[0:1, lo:lo + hd] * _silu(g)).astype(BF16)
        for s in range(nb):
            mix_ref[:, s, cdim + lo:cdim + lo + hd] = out[s]


def _sample_mix(proj, cache, state, layer, prev, conv_w, cvec, gn_g, cos2, sin2):
    ts, S, C = proj.shape
    depth, _, heads, hd, _ = state.shape
    cwidth, cdim = conv_w.shape
    rdim = heads * hd
    D = cdim + rdim
    nb = SAMPLE_SEQS
    assert S % nb == 0 and hd == LANES

    dec, rdec, kdec = (jnp.asarray(t, F32) for t in _decay_tables(ts, heads))

    conv_spec = pl.BlockSpec((None, cwidth - 1, nb, cdim), lambda i: (layer, 0, i, 0))
    st_spec = pl.BlockSpec((None, nb, heads, hd, hd), lambda i: (layer, i, 0, 0, 0))
    args = [proj, cache, state, conv_w, cvec, gn_g, cos2, sin2, dec, rdec, kdec]
    in_specs = [
        pl.BlockSpec((ts, nb, C), lambda i: (0, i, 0)),
        conv_spec,
        st_spec,
        _const_spec(conv_w.shape),
        _const_spec(cvec.shape),
        _const_spec((1, rdim)),
        _const_spec((ts, hd)),
        _const_spec((ts, hd)),
        _const_spec(dec.shape),
        _const_spec(rdec.shape),
        _const_spec(kdec.shape),
    ]
    aliases = {}
    if prev is not None:
        aliases = {len(args): 1, len(args) + 1: 2}
        args += list(prev)
        in_specs += [pl.BlockSpec(memory_space=pl.ANY)] * 2

    kern = functools.partial(_sample_mix_kernel, ts=ts, nb=nb, heads=heads, hd=hd, cdim=cdim, cwidth=cwidth)
    return pl.pallas_call(
        kern,
        grid=(S // nb,),
        in_specs=in_specs,
        out_specs=[pl.BlockSpec((ts, nb, D), lambda i: (0, i, 0)), conv_spec, st_spec],
        out_shape=[
            jax.ShapeDtypeStruct((ts, S, D), BF16),
            jax.ShapeDtypeStruct((depth, cwidth - 1, S, cdim), F32),
            jax.ShapeDtypeStruct((depth, S, heads, hd, hd), F32),
        ],
        input_output_aliases=aliases,
        compiler_params=pltpu.CompilerParams(
            dimension_semantics=("arbitrary",), vmem_limit_bytes=VMEM_LIMIT),
        name="sample_mix",
    )(*args)


def _rope_tables(pos0, T, hd):
    half = hd // 2
    pos = pos0 + jnp.arange(T, dtype=F32)
    inv = ROPE_BASE ** (-jnp.arange(half, dtype=F32) / half)
    ang = pos[:, None] * inv[None, :]
    cos, sin = jnp.cos(ang), jnp.sin(ang)
    return jnp.concatenate([cos, cos], axis=-1), jnp.concatenate([-sin, sin], axis=-1)


def kernel(x_prompt, x_sample, cache_conv, state_ret, norm1_g, w_in, conv_w, conv_b, conv_ln_g,
           conv_ln_b, ret_gn_g, w_out, norm2_g, w_up, w_down, final_norm_g):
    depth = w_in.shape[0]
    B, T, D = x_prompt.shape
    S, ts, _ = x_sample.shape
    heads, hd = state_ret.shape[2], state_ret.shape[3]

    w_in_b, w_out_b, w_up_b, w_dn_b = (w.astype(BF16) for w in (w_in, w_out, w_up, w_down))
    cvec = jnp.stack([conv_b, conv_ln_g, conv_ln_b], axis=1)
    gf = final_norm_g[None, :]
    cos_p, sin_p = _rope_tables(0.0, T, hd)
    cos_s, sin_s = _rope_tables(float(PAST_LEN), ts, hd)

    xp = x_prompt
    xs = x_sample.transpose(1, 0, 2).reshape(ts * S, D)
    cache_t = cache_conv.transpose(0, 2, 1, 3)
    conv_p, ret_p, sample_state = [], [], None
    for l in range(depth):
        final = l == depth - 1
        g1, g2, gn = norm1_g[l][None, :], norm2_g[l][None, :], ret_gn_g[l][None, :]

        mix, nconv, nret = _prompt_mix(xp, g1, w_in_b[l], conv_w[l], cvec[l], gn, cos_p, sin_p, heads, hd)
        xp = _post(xp.reshape(B * T, D), mix.reshape(B * T, D), w_out_b[l], g2, w_up_b[l], w_dn_b[l],
                   gf, final).reshape(B, T, D)
        conv_p.append(nconv)
        ret_p.append(nret)

        proj = _in_proj(xs, g1, w_in_b[l], 512).reshape(ts, S, -1)
        mix, *sample_state = _sample_mix(proj, cache_t, state_ret, l, sample_state, conv_w[l], cvec[l],
                                         gn, cos_s, sin_s)
        xs = _post(xs, mix.reshape(ts * S, D), w_out_b[l], g2, w_up_b[l], w_dn_b[l], gf, final)

    new_cache_t, new_state = sample_state
    return (xp, xs.reshape(ts, S, D).transpose(1, 0, 2), jnp.stack(conv_p), jnp.stack(ret_p),
            new_cache_t.transpose(0, 2, 1, 3), new_state)
```

```python
import functools
import math

import numpy as np
import jax
import jax.numpy as jnp
from jax import lax
from jax.experimental import pallas as pl
from jax.experimental.pallas import tpu as pltpu

F32 = jnp.float32
BF16 = jnp.bfloat16

EPS = 1e-6
ROPE_BASE = 10000.0
PAST_LEN = 16384
LANES = 128
SUBLANES = 8
VMEM_LIMIT = 56 * 1024 * 1024

PROMPT_CHUNK = 512
CONV_ROWS = 128
GLU_LANE_BLOCKS = 2
POST_ROWS = 512
SAMPLE_SEQS = 8


def _log_gamma(h):
    return math.log(1.0 - 2.0 ** (-5.0 - h))


def _rmsnorm(x, g):
    ms = jnp.mean(x * x, axis=-1, keepdims=True)
    return x * lax.rsqrt(ms + EPS) * g


def _layernorm(x):
    mu = jnp.mean(x, axis=-1, keepdims=True)
    xc = x - mu
    var = jnp.mean(xc * xc, axis=-1, keepdims=True)
    return xc * lax.rsqrt(var + EPS)


def _silu(x):
    return x * jax.nn.sigmoid(x)


def _rotary(t, cos2, sin2):
    half = t.shape[-1] // 2
    return t * cos2 + pltpu.roll(t, half, t.ndim - 1) * sin2


def _const_spec(shape):
    nd = len(shape)
    return pl.BlockSpec(shape, lambda *_: (0,) * nd, pipeline_mode=pl.Buffered(1))


def _decay_tables(n, heads):
    idx = np.arange(n, dtype=np.float64)
    diff = idx[:, None] - idx[None, :]
    lg = np.array([_log_gamma(h) for h in range(heads)])
    dec = np.where(diff[None] >= 0, np.exp(lg[:, None, None] * np.maximum(diff, 0.0)[None]), 0.0)
    rdec = np.exp(lg[:, None] * (idx + 1.0))[:, :, None]
    kdec = np.exp(lg[:, None] * (n - 1.0 - idx))[:, :, None]
    return dec, rdec, kdec


def _prompt_mix_kernel(x_ref, g1_ref, win_ref, cw_ref, cvec_ref, gn_ref, cos_ref, sin_ref,
                       dec_ref, rdec_ref, kdec_ref,
                       mix_ref, nconv_ref, nret_ref,
                       glu_s, shift_s, conv_s, q_s, k_s, v_s, g_s, state_s,
                       *, tq, heads, hd, cdim, cwidth, hist):
    c = pl.program_id(1)
    last = pl.num_programs(1) - 1
    rdim = heads * hd

    @pl.when(c == 0)
    def _():
        glu_s[0:hist, :] = jnp.zeros((hist, cdim), F32)
        shift_s[:, 0:hist, :] = jnp.zeros((SUBLANES - 1, hist, cdim), F32)
        shift_s[:, tq:tq + hist, :] = jnp.zeros((SUBLANES - 1, hist, cdim), F32)
        state_s[...] = jnp.zeros_like(state_s)

    @pl.when(c > 0)
    def _():
        glu_s[0:hist, :] = glu_s[tq:tq + hist, :]
        shift_s[:, 0:hist, :] = shift_s[:, tq:tq + hist, :]

    h = _rmsnorm(x_ref[0], g1_ref[...]).astype(BF16)

    def proj(lo, n):
        return jnp.dot(h, win_ref[:, lo:lo + n], preferred_element_type=F32)

    off = hist - (cwidth - 1)
    for lb in range(cdim // LANES):
        ls = slice(lb * LANES, (lb + 1) * LANES)
        if lb % GLU_LANE_BLOCKS == 0:
            n = GLU_LANE_BLOCKS * LANES
            cs = slice(lb * LANES, lb * LANES + n)
            glu = proj(lb * LANES, n) * jax.nn.sigmoid(proj(cdim + lb * LANES, n))
            glu_s[hist:hist + tq, cs] = glu
            for ph in range(1, SUBLANES):
                shift_s[ph - 1, hist - ph:hist - ph + tq, cs] = glu
        for r0 in range(0, tq, CONV_ROWS):
            acc = None
            for w in range(cwidth):
                ph = (off + w) % SUBLANES
                a = r0 + (off + w) // SUBLANES * SUBLANES
                xs = glu_s[a:a + CONV_ROWS, ls] if ph == 0 else shift_s[ph - 1, a:a + CONV_ROWS, ls]
                term = xs * cw_ref[w:w + 1, ls]
                acc = term if acc is None else acc + term
            conv_s[r0:r0 + CONV_ROWS, ls] = acc + cvec_ref[0:1, ls]
    cn = _layernorm(conv_s[...]) * cvec_ref[1:2, :] + cvec_ref[2:3, :]
    mix_ref[0, :, 0:cdim] = _silu(cn).astype(BF16)

    c0 = 2 * cdim
    qf = proj(c0, rdim)
    kf = proj(c0 + rdim, rdim)
    cos2 = cos_ref[...]
    sin2 = sin_ref[...]
    scale = hd ** -0.5
    for hh in range(heads):
        hs = slice(hh * hd, (hh + 1) * hd)
        q_s[:, hs] = _rotary(qf[:, hs], cos2, sin2).astype(BF16)
        k_s[:, hs] = _rotary(kf[:, hs], cos2, sin2) * scale
    v_s[...] = proj(c0 + 2 * rdim, rdim).astype(BF16)
    g_s[...] = _silu(proj(c0 + 3 * rdim, rdim))

    for hh in range(heads):
        hs = slice(hh * hd, (hh + 1) * hd)
        qh = q_s[:, hs]
        kh = k_s[:, hs]
        vh = v_s[:, hs]
        st = state_s[hh]
        scores = lax.dot_general(qh, kh.astype(BF16), (((1,), (1,)), ((), ())),
                                 preferred_element_type=F32) * dec_ref[hh]
        inner = jnp.dot(scores.astype(BF16), vh, preferred_element_type=F32)
        cross = jnp.dot(qh, st.astype(BF16), preferred_element_type=F32) * rdec_ref[hh]
        kd = (kh * kdec_ref[hh]).astype(BF16)
        upd = lax.dot_general(kd, vh, (((0,), (0,)), ((), ())), preferred_element_type=F32)
        state_s[hh] = st * math.exp(_log_gamma(hh) * tq) + upd
        on = _layernorm(inner + cross)
        mix_ref[0, :, cdim + hh * hd:cdim + (hh + 1) * hd] = (on * gn_ref[0:1, hs] * g_s[:, hs]).astype(BF16)

    @pl.when(c == last)
    def _():
        nconv_ref[0] = glu_s[hist + tq - (cwidth - 1):hist + tq, :]
        nret_ref[0] = state_s[...]


def _prompt_mix(x, g1, w_in, conv_w, cvec, gn_g, cos2, sin2, heads, hd):
    B, T, D = x.shape
    cwidth, cdim = conv_w.shape
    rdim = heads * hd
    tq = PROMPT_CHUNK
    hist = -(-(cwidth - 1) // SUBLANES) * SUBLANES
    assert T % tq == 0 and tq >= hist and tq % CONV_ROWS == 0 and hd == LANES

    dec, rdec, kdec = _decay_tables(tq, heads)
    dec = jnp.asarray(dec, F32)
    rdec = jnp.asarray(np.broadcast_to(rdec, (heads, tq, hd)), F32)
    kdec = jnp.asarray(np.broadcast_to(kdec, (heads, tq, hd)), F32)

    kern = functools.partial(_prompt_mix_kernel, tq=tq, heads=heads, hd=hd, cdim=cdim,
                             cwidth=cwidth, hist=hist)
    return pl.pallas_call(
        kern,
        grid=(B, T // tq),
        in_specs=[
            pl.BlockSpec((1, tq, D), lambda b, c: (b, c, 0)),
            _const_spec((1, D)),
            _const_spec(w_in.shape),
            _const_spec(conv_w.shape),
            _const_spec(cvec.shape),
            _const_spec((1, rdim)),
            pl.BlockSpec((tq, hd), lambda b, c: (c, 0)),
            pl.BlockSpec((tq, hd), lambda b, c: (c, 0)),
            _const_spec(dec.shape),
            _const_spec(rdec.shape),
            _const_spec(kdec.shape),
        ],
        out_specs=[
            pl.BlockSpec((1, tq, D), lambda b, c: (b, c, 0)),
            pl.BlockSpec((1, cwidth - 1, cdim), lambda b, c: (b, 0, 0)),
            pl.BlockSpec((1, heads, hd, hd), lambda b, c: (b, 0, 0, 0)),
        ],
        out_shape=[
            jax.ShapeDtypeStruct((B, T, D), BF16),
            jax.ShapeDtypeStruct((B, cwidth - 1, cdim), F32),
            jax.ShapeDtypeStruct((B, heads, hd, hd), F32),
        ],
        scratch_shapes=[
            pltpu.VMEM((hist + tq, cdim), F32),
            pltpu.VMEM((SUBLANES - 1, hist + tq, cdim), F32),
            pltpu.VMEM((tq, cdim), F32),
            pltpu.VMEM((tq, rdim), BF16),
            pltpu.VMEM((tq, rdim), F32),
            pltpu.VMEM((tq, rdim), BF16),
            pltpu.VMEM((tq, rdim), F32),
            pltpu.VMEM((heads, hd, hd), F32),
        ],
        compiler_params=pltpu.CompilerParams(
            dimension_semantics=("arbitrary", "arbitrary"), vmem_limit_bytes=VMEM_LIMIT),
        name="prompt_mix",
    )(x, g1, w_in, conv_w, cvec, gn_g, cos2, sin2, dec, rdec, kdec)


def _post_kernel(x_ref, mix_ref, wout_ref, g2_ref, wup_ref, wdn_ref, gf_ref, o_ref, *, final):
    y = x_ref[...] + jnp.dot(mix_ref[...], wout_ref[...], preferred_element_type=F32)
    h2 = _rmsnorm(y, g2_ref[...]).astype(BF16)
    hf = jnp.dot(h2, wup_ref[...], preferred_element_type=F32)
    act = jnp.square(jnp.maximum(hf, 0.0)).astype(BF16)
    out = y + jnp.dot(act, wdn_ref[...], preferred_element_type=F32)
    if final:
        out = _rmsnorm(out, gf_ref[...])
    o_ref[...] = out


def _post(x, mix, w_out, g2, w_up, w_down, gf, final):
    R, D = x.shape
    tr = min(POST_ROWS, R)
    assert R % tr == 0
    return pl.pallas_call(
        functools.partial(_post_kernel, final=final),
        grid=(R // tr,),
        in_specs=[
            pl.BlockSpec((tr, D), lambda i: (i, 0)),
            pl.BlockSpec((tr, D), lambda i: (i, 0)),
            _const_spec(w_out.shape),
            _const_spec((1, D)),
            _const_spec(w_up.shape),
            _const_spec(w_down.shape),
            _const_spec((1, D)),
        ],
        out_specs=pl.BlockSpec((tr, D), lambda i: (i, 0)),
        out_shape=jax.ShapeDtypeStruct((R, D), F32),
        compiler_params=pltpu.CompilerParams(
            dimension_semantics=("arbitrary",), vmem_limit_bytes=VMEM_LIMIT),
        name="post",
    )(x, mix, w_out, g2, w_up, w_down, gf)


def _in_proj_kernel(x_ref, g1_ref, w_ref, o_ref, h_s):
    @pl.when(pl.program_id(0) == 0)
    def _():
        h_s[...] = _rmsnorm(x_ref[...], g1_ref[...]).astype(BF16)

    o_ref[...] = jnp.dot(h_s[...], w_ref[...], preferred_element_type=F32)


def _in_proj(x, g1, w_in, ncol):
    R, D = x.shape
    C = w_in.shape[1]
    return pl.pallas_call(
        _in_proj_kernel,
        grid=(C // ncol,),
        in_specs=[
            _const_spec((R, D)),
            _const_spec((1, D)),
            pl.BlockSpec((D, ncol), lambda j: (0, j)),
        ],
        out_specs=pl.BlockSpec((R, ncol), lambda j: (0, j)),
        out_shape=jax.ShapeDtypeStruct((R, C), F32),
        scratch_shapes=[pltpu.VMEM((R, D), BF16)],
        compiler_params=pltpu.CompilerParams(
            dimension_semantics=("arbitrary",), vmem_limit_bytes=VMEM_LIMIT),
        name="sample_in_proj",
    )(x, g1, w_in)


def _sample_mix_kernel(proj_ref, cache_ref, st_ref, cw_ref, cvec_ref, gn_ref, cos_ref, sin_ref,
                       dec_ref, rdec_ref, kdec_ref, *rest, ts, nb, heads, hd, cdim, cwidth):
    mix_ref, ncache_ref, nst_ref = rest[-3:]
    rdim = heads * hd

    glu = [proj_ref[t, :, 0:cdim] * jax.nn.sigmoid(proj_ref[t, :, cdim:2 * cdim]) for t in range(ts)]

    def window(r):
        return cache_ref[r] if r < cwidth - 1 else glu[r - (cwidth - 1)]

    acc = [None] * ts
    for r in range(cwidth - 1 + ts):
        x = window(r)
        for t in range(max(0, r - cwidth + 1), min(ts, r + 1)):
            term = x * cw_ref[r - t:r - t + 1, :]
            acc[t] = term if acc[t] is None else acc[t] + term
    for t in range(ts):
        cn = _layernorm(acc[t] + cvec_ref[0:1, :]) * cvec_ref[1:2, :] + cvec_ref[2:3, :]
        mix_ref[t, :, 0:cdim] = _silu(cn).astype(BF16)
    for r in range(cwidth - 1):
        ncache_ref[r] = window(r + ts)

    c0 = 2 * cdim
    cos2 = cos_ref[...]
    sin2 = sin_ref[...]
    scale = hd ** -0.5
    for hh in range(heads):
        lo = hh * hd

        def per_seq(base):
            return jnp.stack([proj_ref[:, s, base + lo:base + lo + hd] for s in range(nb)], axis=0)

        q = _rotary(per_seq(c0), cos2, sin2)
        k = _rotary(per_seq(c0 + rdim), cos2, sin2) * scale
        v = per_seq(c0 + 2 * rdim)
        g = per_seq(c0 + 3 * rdim)
        st = st_ref[:, hh]
        scores = jnp.einsum('btd,bjd->btj', q, k, preferred_element_type=F32) * dec_ref[hh]
        inner = jnp.einsum('btj,bjv->btv', scores, v, preferred_element_type=F32)
        cross = jnp.einsum('btd,bdv->btv', q, st, preferred_element_type=F32) * rdec_ref[hh]
        upd = jnp.einsum('bjd,bjv->bdv', k * kdec_ref[hh], v, preferred_element_type=F32)
        nst_ref[:, hh] = st * math.exp(_log_gamma(hh) * ts) + upd
        on = _layernorm(inner + cross)
        out = (on * gn_ref[0:1, lo:lo + hd] * _silu(g)).astype(BF16)
        for s in range(nb):
            mix_ref[:, s, cdim + lo:cdim + lo + hd] = out[s]


def _sample_mix(proj, cache, state, layer, prev, conv_w, cvec, gn_g, cos2, sin2):
    ts, S, C = proj.shape
    depth, _, heads, hd, _ = state.shape
    cwidth, cdim = conv_w.shape
    rdim = heads * hd
    D = cdim + rdim
    nb = SAMPLE_SEQS
    assert S % nb == 0 and hd == LANES

    dec, rdec, kdec = (jnp.asarray(t, F32) for t in _decay_tables(ts, heads))

    conv_spec = pl.BlockSpec((None, cwidth - 1, nb, cdim), lambda i: (layer, 0, i, 0))
    st_spec = pl.BlockSpec((None, nb, heads, hd, hd), lambda i: (layer, i, 0, 0, 0))
    args = [proj, cache, state, conv_w, cvec, gn_g, cos2, sin2, dec, rdec, kdec]
    in_specs = [
        pl.BlockSpec((ts, nb, C), lambda i: (0, i, 0)),
        conv_spec,
        st_spec,
        _const_spec(conv_w.shape),
        _const_spec(cvec.shape),
        _const_spec((1, rdim)),
        _const_spec((ts, hd)),
        _const_spec((ts, hd)),
        _const_spec(dec.shape),
        _const_spec(rdec.shape),
        _const_spec(kdec.shape),
    ]
    aliases = {}
    if prev is not None:
        aliases = {len(args): 1, len(args) + 1: 2}
        args += list(prev)
        in_specs += [pl.BlockSpec(memory_space=pl.ANY)] * 2

    kern = functools.partial(_sample_mix_kernel, ts=ts, nb=nb, heads=heads, hd=hd, cdim=cdim, cwidth=cwidth)
    return pl.pallas_call(
        kern,
        grid=(S // nb,),
        in_specs=in_specs,
        out_specs=[pl.BlockSpec((ts, nb, D), lambda i: (0, i, 0)), conv_spec, st_spec],
        out_shape=[
            jax.ShapeDtypeStruct((ts, S, D), BF16),
            jax.ShapeDtypeStruct((depth, cwidth - 1, S, cdim), F32),
            jax.ShapeDtypeStruct((depth, S, heads, hd, hd), F32),
        ],
        input_output_aliases=aliases,
        compiler_params=pltpu.CompilerParams(
            dimension_semantics=("arbitrary",), vmem_limit_bytes=VMEM_LIMIT),
        name="sample_mix",
    )(*args)


def _rope_tables(pos0, T, hd):
    half = hd // 2
    pos = pos0 + jnp.arange(T, dtype=F32)
    inv = ROPE_BASE ** (-jnp.arange(half, dtype=F32) / half)
    ang = pos[:, None] * inv[None, :]
    cos, sin = jnp.cos(ang), jnp.sin(ang)
    return jnp.concatenate([cos, cos], axis=-1), jnp.concatenate([-sin, sin], axis=-1)


def kernel(x_prompt, x_sample, cache_conv, state_ret, norm1_g, w_in, conv_w, conv_b, conv_ln_g,
           conv_ln_b, ret_gn_g, w_out, norm2_g, w_up, w_down, final_norm_g):
    depth = w_in.shape[0]
    B, T, D = x_prompt.shape
    S, ts, _ = x_sample.shape
    heads, hd = state_ret.shape[2], state_ret.shape[3]

    w_in_b, w_out_b, w_up_b, w_dn_b = (w.astype(BF16) for w in (w_in, w_out, w_up, w_down))
    cvec = jnp.stack([conv_b, conv_ln_g, conv_ln_b], axis=1)
    gf = final_norm_g[None, :]
    cos_p, sin_p = _rope_tables(0.0, T, hd)
    cos_s, sin_s = _rope_tables(float(PAST_LEN), ts, hd)

    xp = x_prompt
    xs = x_sample.transpose(1, 0, 2).reshape(ts * S, D)
    cache_t = cache_conv.transpose(0, 2, 1, 3)
    conv_p, ret_p, sample_state = [], [], None
    for l in range(depth):
        final = l == depth - 1
        g1, g2, gn = norm1_g[l][None, :], norm2_g[l][None, :], ret_gn_g[l][None, :]

        mix, nconv, nret = _prompt_mix(xp, g1, w_in_b[l], conv_w[l], cvec[l], gn, cos_p, sin_p, heads, hd)
        xp = _post(xp.reshape(B * T, D), mix.reshape(B * T, D), w_out_b[l], g2, w_up_b[l], w_dn_b[l],
                   gf, final).reshape(B, T, D)
        conv_p.append(nconv)
        ret_p.append(nret)

        proj = _in_proj(xs, g1, w_in_b[l], 512).reshape(ts, S, -1)
        mix, *sample_state = _sample_mix(proj, cache_t, state_ret, l, sample_state, conv_w[l], cvec[l],
                                         gn, cos_s, sin_s)
        xs = _post(xs, mix.reshape(ts * S, D), w_out_b[l], g2, w_up_b[l], w_dn_b[l], gf, final)

    new_cache_t, new_state = sample_state
    return (xp, xs.reshape(ts, S, D).transpose(1, 0, 2), jnp.stack(conv_p), jnp.stack(ret_p),
            new_cache_t.transpose(0, 2, 1, 3), new_state)
```

```python
import functools
import math

import numpy as np
import jax
import jax.numpy as jnp
from jax import lax
from jax.experimental import pallas as pl
from jax.experimental.pallas import tpu as pltpu

F32 = jnp.float32
BF16 = jnp.bfloat16

EPS = 1e-6
ROPE_BASE = 10000.0
PAST_LEN = 16384
LANES = 128
SUBLANES = 8
VMEM_LIMIT = 56 * 1024 * 1024

PROMPT_CHUNK = 512
CONV_ROWS = 128
GLU_LANE_BLOCKS = 2
POST_ROWS = 512
SAMPLE_SEQS = 16


def _log_gamma(h):
    return math.log(1.0 - 2.0 ** (-5.0 - h))


def _rmsnorm(x, g):
    ms = jnp.mean(x * x, axis=-1, keepdims=True)
    return x * lax.rsqrt(ms + EPS) * g


def _layernorm(x):
    mu = jnp.mean(x, axis=-1, keepdims=True)
    xc = x - mu
    var = jnp.mean(xc * xc, axis=-1, keepdims=True)
    return xc * lax.rsqrt(var + EPS)


def _silu(x):
    return x * jax.nn.sigmoid(x)


def _rotary(t, cos2, sin2):
    half = t.shape[-1] // 2
    return t * cos2 + pltpu.roll(t, half, t.ndim - 1) * sin2


def _const_spec(shape):
    nd = len(shape)
    return pl.BlockSpec(shape, lambda *_: (0,) * nd, pipeline_mode=pl.Buffered(1))


def _decay_tables(n, heads):
    idx = np.arange(n, dtype=np.float64)
    diff = idx[:, None] - idx[None, :]
    lg = np.array([_log_gamma(h) for h in range(heads)])
    dec = np.where(diff[None] >= 0, np.exp(lg[:, None, None] * np.maximum(diff, 0.0)[None]), 0.0)
    rdec = np.exp(lg[:, None] * (idx + 1.0))[:, :, None]
    kdec = np.exp(lg[:, None] * (n - 1.0 - idx))[:, :, None]
    return dec, rdec, kdec


def _prompt_mix_kernel(x_ref, g1_ref, win_ref, cw_ref, cvec_ref, gn_ref, cos_ref, sin_ref,
                       dec_ref, rdec_ref, kdec_ref,
                       mix_ref, nconv_ref, nret_ref,
                       glu_s, shift_s, conv_s, q_s, k_s, v_s, g_s, state_s,
                       *, tq, heads, hd, cdim, cwidth, hist):
    c = pl.program_id(1)
    last = pl.num_programs(1) - 1
    rdim = heads * hd

    @pl.when(c == 0)
    def _():
        glu_s[0:hist, :] = jnp.zeros((hist, cdim), F32)
        shift_s[:, 0:hist, :] = jnp.zeros((SUBLANES - 1, hist, cdim), F32)
        shift_s[:, tq:tq + hist, :] = jnp.zeros((SUBLANES - 1, hist, cdim), F32)
        state_s[...] = jnp.zeros_like(state_s)

    @pl.when(c > 0)
    def _():
        glu_s[0:hist, :] = glu_s[tq:tq + hist, :]
        shift_s[:, 0:hist, :] = shift_s[:, tq:tq + hist, :]

    h = _rmsnorm(x_ref[0], g1_ref[...]).astype(BF16)

    def proj(lo, n):
        return jnp.dot(h, win_ref[:, lo:lo + n], preferred_element_type=F32)

    off = hist - (cwidth - 1)
    for lb in range(cdim // LANES):
        ls = slice(lb * LANES, (lb + 1) * LANES)
        if lb % GLU_LANE_BLOCKS == 0:
            n = GLU_LANE_BLOCKS * LANES
            cs = slice(lb * LANES, lb * LANES + n)
            glu = proj(lb * LANES, n) * jax.nn.sigmoid(proj(cdim + lb * LANES, n))
            glu_s[hist:hist + tq, cs] = glu
            for ph in range(1, SUBLANES):
                shift_s[ph - 1, hist - ph:hist - ph + tq, cs] = glu
        for r0 in range(0, tq, CONV_ROWS):
            acc = None
            for w in range(cwidth):
                ph = (off + w) % SUBLANES
                a = r0 + (off + w) // SUBLANES * SUBLANES
                xs = glu_s[a:a + CONV_ROWS, ls] if ph == 0 else shift_s[ph - 1, a:a + CONV_ROWS, ls]
                term = xs * cw_ref[w:w + 1, ls]
                acc = term if acc is None else acc + term
            conv_s[r0:r0 + CONV_ROWS, ls] = acc + cvec_ref[0:1, ls]
    cn = _layernorm(conv_s[...]) * cvec_ref[1:2, :] + cvec_ref[2:3, :]
    mix_ref[0, :, 0:cdim] = _silu(cn).astype(BF16)

    c0 = 2 * cdim
    qf = proj(c0, rdim)
    kf = proj(c0 + rdim, rdim)
    cos2 = cos_ref[...]
    sin2 = sin_ref[...]
    scale = hd ** -0.5
    for hh in range(heads):
        hs = slice(hh * hd, (hh + 1) * hd)
        q_s[:, hs] = _rotary(qf[:, hs], cos2, sin2).astype(BF16)
        k_s[:, hs] = _rotary(kf[:, hs], cos2, sin2) * scale
    v_s[...] = proj(c0 + 2 * rdim, rdim).astype(BF16)
    g_s[...] = _silu(proj(c0 + 3 * rdim, rdim))

    for hh in range(heads):
        hs = slice(hh * hd, (hh + 1) * hd)
        qh = q_s[:, hs]
        kh = k_s[:, hs]
        vh = v_s[:, hs]
        st = state_s[hh]
        scores = lax.dot_general(qh, kh.astype(BF16), (((1,), (1,)), ((), ())),
                                 preferred_element_type=F32) * dec_ref[hh]
        inner = jnp.dot(scores.astype(BF16), vh, preferred_element_type=F32)
        cross = jnp.dot(qh, st.astype(BF16), preferred_element_type=F32) * rdec_ref[hh]
        kd = (kh * kdec_ref[hh]).astype(BF16)
        upd = lax.dot_general(kd, vh, (((0,), (0,)), ((), ())), preferred_element_type=F32)
        state_s[hh] = st * math.exp(_log_gamma(hh) * tq) + upd
        on = _layernorm(inner + cross)
        mix_ref[0, :, cdim + hh * hd:cdim + (hh + 1) * hd] = (on * gn_ref[0:1, hs] * g_s[:, hs]).astype(BF16)

    @pl.when(c == last)
    def _():
        nconv_ref[0] = glu_s[hist + tq - (cwidth - 1):hist + tq, :]
        nret_ref[0] = state_s[...]


def _prompt_mix(x, g1, w_in, layer, conv_w, cvec, gn_g, cos2, sin2, heads, hd):
    B, T, D = x.shape
    cwidth, cdim = conv_w.shape
    rdim = heads * hd
    tq = PROMPT_CHUNK
    hist = -(-(cwidth - 1) // SUBLANES) * SUBLANES
    assert T % tq == 0 and tq >= hist and tq % CONV_ROWS == 0 and hd == LANES

    dec, rdec, kdec = _decay_tables(tq, heads)
    dec = jnp.asarray(dec, F32)
    rdec = jnp.asarray(np.broadcast_to(rdec, (heads, tq, hd)), F32)
    kdec = jnp.asarray(np.broadcast_to(kdec, (heads, tq, hd)), F32)

    kern = functools.partial(_prompt_mix_kernel, tq=tq, heads=heads, hd=hd, cdim=cdim,
                             cwidth=cwidth, hist=hist)
    return pl.pallas_call(
        kern,
        grid=(B, T // tq),
        in_specs=[
            pl.BlockSpec((1, tq, D), lambda b, c: (b, c, 0)),
            _const_spec((1, D)),
            pl.BlockSpec((None,) + w_in.shape[1:], lambda b, c: (layer, 0, 0), pipeline_mode=pl.Buffered(1)),
            _const_spec(conv_w.shape),
            _const_spec(cvec.shape),
            _const_spec((1, rdim)),
            pl.BlockSpec((tq, hd), lambda b, c: (c, 0)),
            pl.BlockSpec((tq, hd), lambda b, c: (c, 0)),
            _const_spec(dec.shape),
            _const_spec(rdec.shape),
            _const_spec(kdec.shape),
        ],
        out_specs=[
            pl.BlockSpec((1, tq, D), lambda b, c: (b, c, 0)),
            pl.BlockSpec((1, cwidth - 1, cdim), lambda b, c: (b, 0, 0)),
            pl.BlockSpec((1, heads, hd, hd), lambda b, c: (b, 0, 0, 0)),
        ],
        out_shape=[
            jax.ShapeDtypeStruct((B, T, D), BF16),
            jax.ShapeDtypeStruct((B, cwidth - 1, cdim), F32),
            jax.ShapeDtypeStruct((B, heads, hd, hd), F32),
        ],
        scratch_shapes=[
            pltpu.VMEM((hist + tq, cdim), F32),
            pltpu.VMEM((SUBLANES - 1, hist + tq, cdim), F32),
            pltpu.VMEM((tq, cdim), F32),
            pltpu.VMEM((tq, rdim), BF16),
            pltpu.VMEM((tq, rdim), F32),
            pltpu.VMEM((tq, rdim), BF16),
            pltpu.VMEM((tq, rdim), F32),
            pltpu.VMEM((heads, hd, hd), F32),
        ],
        compiler_params=pltpu.CompilerParams(
            dimension_semantics=("arbitrary", "arbitrary"), vmem_limit_bytes=VMEM_LIMIT),
        name="prompt_mix",
    )(x, g1, w_in, conv_w, cvec, gn_g, cos2, sin2, dec, rdec, kdec)


def _post_kernel(xa_ref, mixa_ref, xb_ref, mixb_ref, wout_ref, g2_ref, wup_ref, wdn_ref, gf_ref,
                 oa_ref, ob_ref, *, final, steps_a):
    def tile(x_ref, mix_ref, o_ref):
        y = x_ref[...] + jnp.dot(mix_ref[...], wout_ref[...], preferred_element_type=F32)
        h2 = _rmsnorm(y, g2_ref[...]).astype(BF16)
        hf = jnp.dot(h2, wup_ref[...], preferred_element_type=F32)
        act = jnp.square(jnp.maximum(hf, 0.0)).astype(BF16)
        out = y + jnp.dot(act, wdn_ref[...], preferred_element_type=F32)
        if final:
            out = _rmsnorm(out, gf_ref[...])
        o_ref[...] = out

    i = pl.program_id(0)

    @pl.when(i < steps_a)
    def _():
        tile(xa_ref, mixa_ref, oa_ref)

    @pl.when(i >= steps_a)
    def _():
        tile(xb_ref, mixb_ref, ob_ref)


def _post(xa, mixa, xb, mixb, layer, w_out, g2, w_up, w_down, gf, final):
    Ra, D = xa.shape
    Rb = xb.shape[0]
    tr = POST_ROWS
    assert Ra % tr == 0 and Rb % tr == 0
    na, nb = Ra // tr, Rb // tr
    rows_a = pl.BlockSpec((tr, D), lambda i: (jnp.minimum(i, na - 1), 0))
    rows_b = pl.BlockSpec((tr, D), lambda i: (jnp.maximum(i - na, 0), 0))

    def layer_spec(w):
        return pl.BlockSpec((None,) + w.shape[1:], lambda i: (layer, 0, 0), pipeline_mode=pl.Buffered(1))

    return pl.pallas_call(
        functools.partial(_post_kernel, final=final, steps_a=na),
        grid=(na + nb,),
        in_specs=[
            rows_a, rows_a, rows_b, rows_b,
            layer_spec(w_out),
            _const_spec((1, D)),
            layer_spec(w_up),
            layer_spec(w_down),
            _const_spec((1, D)),
        ],
        out_specs=[rows_a, rows_b],
        out_shape=[jax.ShapeDtypeStruct((Ra, D), F32), jax.ShapeDtypeStruct((Rb, D), F32)],
        compiler_params=pltpu.CompilerParams(
            dimension_semantics=("arbitrary",), vmem_limit_bytes=VMEM_LIMIT),
        name="post",
    )(xa, mixa, xb, mixb, w_out, g2, w_up, w_down, gf)


def _in_proj_kernel(x_ref, g1_ref, w_ref, o_ref, h_s):
    @pl.when(pl.program_id(0) == 0)
    def _():
        h_s[...] = _rmsnorm(x_ref[...], g1_ref[...]).astype(BF16)

    o_ref[...] = jnp.dot(h_s[...], w_ref[...], preferred_element_type=F32)


def _in_proj(x, g1, w_in, layer, ncol):
    R, D = x.shape
    C = w_in.shape[2]
    return pl.pallas_call(
        _in_proj_kernel,
        grid=(C // ncol,),
        in_specs=[
            _const_spec((R, D)),
            _const_spec((1, D)),
            pl.BlockSpec((None, D, ncol), lambda j: (layer, 0, j)),
        ],
        out_specs=pl.BlockSpec((R, ncol), lambda j: (0, j)),
        out_shape=jax.ShapeDtypeStruct((R, C), F32),
        scratch_shapes=[pltpu.VMEM((R, D), BF16)],
        compiler_params=pltpu.CompilerParams(
            dimension_semantics=("arbitrary",), vmem_limit_bytes=VMEM_LIMIT),
        name="sample_in_proj",
    )(x, g1, w_in)


def _sample_mix_kernel(proj_ref, cache_ref, st_ref, cw_ref, cvec_ref, gn_ref, cos_ref, sin_ref,
                       dec_ref, rdec_ref, kdec_ref, *rest, ts, nb, heads, hd, cdim, cwidth):
    mix_ref, ncache_ref, nst_ref = rest[-3:]
    rdim = heads * hd

    glu = [proj_ref[t, :, 0:cdim] * jax.nn.sigmoid(proj_ref[t, :, cdim:2 * cdim]) for t in range(ts)]

    def window(r):
        return cache_ref[r] if r < cwidth - 1 else glu[r - (cwidth - 1)]

    acc = [None] * ts
    for r in range(cwidth - 1 + ts):
        x = window(r)
        for t in range(max(0, r - cwidth + 1), min(ts, r + 1)):
            term = x * cw_ref[r - t:r - t + 1, :]
            acc[t] = term if acc[t] is None else acc[t] + term
    for t in range(ts):
        cn = _layernorm(acc[t] + cvec_ref[0:1, :]) * cvec_ref[1:2, :] + cvec_ref[2:3, :]
        mix_ref[t, :, 0:cdim] = _silu(cn).astype(BF16)
    for r in range(cwidth - 1):
        ncache_ref[r] = window(r + ts)

    c0 = 2 * cdim
    cos2 = cos_ref[...]
    sin2 = sin_ref[...]
    scale = hd ** -0.5
    for hh in range(heads):
        lo = hh * hd

        def per_seq(base):
            return jnp.stack([proj_ref[:, s, base + lo:base + lo + hd] for s in range(nb)], axis=0)

        q = _rotary(per_seq(c0), cos2, sin2)
        k = _rotary(per_seq(c0 + rdim), cos2, sin2) * scale
        v = per_seq(c0 + 2 * rdim)
        g = per_seq(c0 + 3 * rdim)
        st = st_ref[:, hh]
        scores = jnp.einsum('btd,bjd->btj', q, k, preferred_element_type=F32) * dec_ref[hh]
        inner = jnp.einsum('btj,bjv->btv', scores, v, preferred_element_type=F32)
        cross = jnp.einsum('btd,bdv->btv', q, st, preferred_element_type=F32) * rdec_ref[hh]
        upd = jnp.einsum('bjd,bjv->bdv', k * kdec_ref[hh], v, preferred_element_type=F32)
        nst_ref[:, hh] = st * math.exp(_log_gamma(hh) * ts) + upd
        on = _layernorm(inner + cross)
        out = (on * gn_ref[0:1, lo:lo + hd] * _silu(g)).astype(BF16)
        for s in range(nb):
            mix_ref[:, s, cdim + lo:cdim + lo + hd] = out[s]


def _sample_mix(proj, cache, state, layer, prev, conv_w, cvec, gn_g, cos2, sin2):
    ts, S, C = proj.shape
    depth, _, heads, hd, _ = state.shape
    cwidth, cdim = conv_w.shape
    rdim = heads * hd
    D = cdim + rdim
    nb = SAMPLE_SEQS
    assert S % nb == 0 and hd == LANES

    dec, rdec, kdec = (jnp.asarray(t, F32) for t in _decay_tables(ts, heads))

    conv_spec = pl.BlockSpec((None, cwidth - 1, nb, cdim), lambda i: (layer, 0, i, 0))
    st_spec = pl.BlockSpec((None, nb, heads, hd, hd), lambda i: (layer, i, 0, 0, 0))
    args = [proj, cache, state, conv_w, cvec, gn_g, cos2, sin2, dec, rdec, kdec]
    in_specs = [
        pl.BlockSpec((ts, nb, C), lambda i: (0, i, 0)),
        conv_spec,
        st_spec,
        _const_spec(conv_w.shape),
        _const_spec(cvec.shape),
        _const_spec((1, rdim)),
        _const_spec((ts, hd)),
        _const_spec((ts, hd)),
        _const_spec(dec.shape),
        _const_spec(rdec.shape),
        _const_spec(kdec.shape),
    ]
    aliases = {}
    if prev is not None:
        aliases = {len(args): 1, len(args) + 1: 2}
        args += list(prev)
        in_specs += [pl.BlockSpec(memory_space=pl.ANY)] * 2

    kern = functools.partial(_sample_mix_kernel, ts=ts, nb=nb, heads=heads, hd=hd, cdim=cdim, cwidth=cwidth)
    return pl.pallas_call(
        kern,
        grid=(S // nb,),
        in_specs=in_specs,
        out_specs=[pl.BlockSpec((ts, nb, D), lambda i: (0, i, 0)), conv_spec, st_spec],
        out_shape=[
            jax.ShapeDtypeStruct((ts, S, D), BF16),
            jax.ShapeDtypeStruct((depth, cwidth - 1, S, cdim), F32),
            jax.ShapeDtypeStruct((depth, S, heads, hd, hd), F32),
        ],
        input_output_aliases=aliases,
        compiler_params=pltpu.CompilerParams(
            dimension_semantics=("arbitrary",), vmem_limit_bytes=VMEM_LIMIT),
        name="sample_mix",
    )(*args)


def _rope_tables(pos0, T, hd):
    half = hd // 2
    pos = pos0 + jnp.arange(T, dtype=F32)
    inv = ROPE_BASE ** (-jnp.arange(half, dtype=F32) / half)
    ang = pos[:, None] * inv[None, :]
    cos, sin = jnp.cos(ang), jnp.sin(ang)
    return jnp.concatenate([cos, cos], axis=-1), jnp.concatenate([-sin, sin], axis=-1)


def kernel(x_prompt, x_sample, cache_conv, state_ret, norm1_g, w_in, conv_w, conv_b, conv_ln_g,
           conv_ln_b, ret_gn_g, w_out, norm2_g, w_up, w_down, final_norm_g):
    depth = w_in.shape[0]
    B, T, D = x_prompt.shape
    S, ts, _ = x_sample.shape
    heads, hd = state_ret.shape[2], state_ret.shape[3]

    w_in_b, w_out_b, w_up_b, w_dn_b = (w.astype(BF16) for w in (w_in, w_out, w_up, w_down))
    cvec = jnp.stack([conv_b, conv_ln_g, conv_ln_b], axis=1)
    gf = final_norm_g[None, :]
    cos_p, sin_p = _rope_tables(0.0, T, hd)
    cos_s, sin_s = _rope_tables(float(PAST_LEN), ts, hd)

    xp = x_prompt.reshape(B * T, D)
    xs = x_sample.transpose(1, 0, 2).reshape(ts * S, D)
    cache_t = cache_conv.transpose(0, 2, 1, 3)
    conv_p, ret_p, sample_state = [], [], None
    for l in range(depth):
        final = l == depth - 1
        g1, g2, gn = norm1_g[l][None, :], norm2_g[l][None, :], ret_gn_g[l][None, :]

        mix_p, nconv, nret = _prompt_mix(xp.reshape(B, T, D), g1, w_in_b, l, conv_w[l], cvec[l], gn,
                                         cos_p, sin_p, heads, hd)
        conv_p.append(nconv)
        ret_p.append(nret)

        proj = _in_proj(xs, g1, w_in_b, l, 512).reshape(ts, S, -1)
        mix_s, *sample_state = _sample_mix(proj, cache_t, state_ret, l, sample_state, conv_w[l], cvec[l],
                                           gn, cos_s, sin_s)

        xp, xs = _post(xp, mix_p.reshape(B * T, D), xs, mix_s.reshape(ts * S, D), l, w_out_b, g2,
                       w_up_b, w_dn_b, gf, final)

    new_cache_t, new_state = sample_state
    return (xp.reshape(B, T, D), xs.reshape(ts, S, D).transpose(1, 0, 2), jnp.stack(conv_p),
            jnp.stack(ret_p), new_cache_t.transpose(0, 2, 1, 3), new_state)
```

```python
import functools
import math

import numpy as np
import jax
import jax.numpy as jnp
from jax import lax
from jax.experimental import pallas as pl
from jax.experimental.pallas import tpu as pltpu

F32 = jnp.float32
BF16 = jnp.bfloat16

EPS = 1e-6
ROPE_BASE = 10000.0
PAST_LEN = 16384
LANES = 128
SUBLANES = 8
VMEM_LIMIT = 56 * 1024 * 1024

PROMPT_CHUNK = 512
CONV_ROWS = 128
GLU_LANE_BLOCKS = 2
POST_ROWS = 512
POST_WEIGHT_STEPS = 8
SAMPLE_SEQS = 16
CAST_COLS = 512


def _log_gamma(h):
    return math.log(1.0 - 2.0 ** (-5.0 - h))


def _rmsnorm(x, g):
    ms = jnp.mean(x * x, axis=-1, keepdims=True)
    return x * lax.rsqrt(ms + EPS) * g


def _layernorm(x):
    mu = jnp.mean(x, axis=-1, keepdims=True)
    xc = x - mu
    var = jnp.mean(xc * xc, axis=-1, keepdims=True)
    return xc * lax.rsqrt(var + EPS)


def _silu(x):
    return x * jax.nn.sigmoid(x)


def _rotary(t, cos2, sin2):
    half = t.shape[-1] // 2
    return t * cos2 + pltpu.roll(t, half, t.ndim - 1) * sin2


def _const_spec(shape):
    nd = len(shape)
    return pl.BlockSpec(shape, lambda *_: (0,) * nd, pipeline_mode=pl.Buffered(1))


def _decay_tables(n, heads):
    idx = np.arange(n, dtype=np.float64)
    diff = idx[:, None] - idx[None, :]
    lg = np.array([_log_gamma(h) for h in range(heads)])
    dec = np.where(diff[None] >= 0, np.exp(lg[:, None, None] * np.maximum(diff, 0.0)[None]), 0.0)
    rdec = np.exp(lg[:, None] * (idx + 1.0))[:, :, None]
    kdec = np.exp(lg[:, None] * (n - 1.0 - idx))[:, :, None]
    return dec, rdec, kdec


def _prompt_mix_kernel(x_ref, g1_ref, win_ref, cw_ref, cvec_ref, gn_ref, cos_ref, sin_ref,
                       dec_ref, rdec_ref, kdec_ref,
                       mix_ref, nconv_ref, nret_ref,
                       w_s, glu_s, shift_s, conv_s, q_s, k_s, v_s, g_s, state_s,
                       *, tq, heads, hd, cdim, cwidth, hist):
    b = pl.program_id(0)
    c = pl.program_id(1)
    last = pl.num_programs(1) - 1
    rdim = heads * hd

    @pl.when((b == 0) & (c == 0))
    def _():
        for lo in range(0, w_s.shape[1], CAST_COLS):
            w_s[:, lo:lo + CAST_COLS] = win_ref[:, lo:lo + CAST_COLS].astype(BF16)

    @pl.when(c == 0)
    def _():
        glu_s[0:hist, :] = jnp.zeros((hist, cdim), F32)
        shift_s[:, 0:hist, :] = jnp.zeros((SUBLANES - 1, hist, cdim), F32)
        shift_s[:, tq:tq + hist, :] = jnp.zeros((SUBLANES - 1, hist, cdim), F32)
        state_s[...] = jnp.zeros_like(state_s)

    @pl.when(c > 0)
    def _():
        glu_s[0:hist, :] = glu_s[tq:tq + hist, :]
        shift_s[:, 0:hist, :] = shift_s[:, tq:tq + hist, :]

    h = _rmsnorm(x_ref[0], g1_ref[...]).astype(BF16)

    def proj(lo, n):
        return jnp.dot(h, w_s[:, lo:lo + n], preferred_element_type=F32)

    off = hist - (cwidth - 1)
    for lb in range(cdim // LANES):
        ls = slice(lb * LANES, (lb + 1) * LANES)
        if lb % GLU_LANE_BLOCKS == 0:
            n = GLU_LANE_BLOCKS * LANES
            cs = slice(lb * LANES, lb * LANES + n)
            glu = proj(lb * LANES, n) * jax.nn.sigmoid(proj(cdim + lb * LANES, n))
            glu_s[hist:hist + tq, cs] = glu
            for ph in range(1, SUBLANES):
                shift_s[ph - 1, hist - ph:hist - ph + tq, cs] = glu
        for r0 in range(0, tq, CONV_ROWS):
            acc = None
            for w in range(cwidth):
                ph = (off + w) % SUBLANES
                a = r0 + (off + w) // SUBLANES * SUBLANES
                xs = glu_s[a:a + CONV_ROWS, ls] if ph == 0 else shift_s[ph - 1, a:a + CONV_ROWS, ls]
                term = xs * cw_ref[w:w + 1, ls]
                acc = term if acc is None else acc + term
            conv_s[r0:r0 + CONV_ROWS, ls] = acc + cvec_ref[0:1, ls]
    cn = _layernorm(conv_s[...]) * cvec_ref[1:2, :] + cvec_ref[2:3, :]
    mix_ref[0, :, 0:cdim] = _silu(cn).astype(BF16)

    c0 = 2 * cdim
    qf = proj(c0, rdim)
    kf = proj(c0 + rdim, rdim)
    cos2 = cos_ref[...]
    sin2 = sin_ref[...]
    scale = hd ** -0.5
    for hh in range(heads):
        hs = slice(hh * hd, (hh + 1) * hd)
        q_s[:, hs] = _rotary(qf[:, hs], cos2, sin2).astype(BF16)
        k_s[:, hs] = _rotary(kf[:, hs], cos2, sin2) * scale
    v_s[...] = proj(c0 + 2 * rdim, rdim).astype(BF16)
    g_s[...] = _silu(proj(c0 + 3 * rdim, rdim))

    for hh in range(heads):
        hs = slice(hh * hd, (hh + 1) * hd)
        qh = q_s[:, hs]
        kh = k_s[:, hs]
        vh = v_s[:, hs]
        st = state_s[hh]
        scores = lax.dot_general(qh, kh.astype(BF16), (((1,), (1,)), ((), ())),
                                 preferred_element_type=F32) * dec_ref[hh]
        inner = jnp.dot(scores.astype(BF16), vh, preferred_element_type=F32)
        cross = jnp.dot(qh, st.astype(BF16), preferred_element_type=F32) * rdec_ref[hh]
        kd = (kh * kdec_ref[hh]).astype(BF16)
        upd = lax.dot_general(kd, vh, (((0,), (0,)), ((), ())), preferred_element_type=F32)
        state_s[hh] = st * math.exp(_log_gamma(hh) * tq) + upd
        on = _layernorm(inner + cross)
        mix_ref[0, :, cdim + hh * hd:cdim + (hh + 1) * hd] = (on * gn_ref[0:1, hs] * g_s[:, hs]).astype(BF16)

    @pl.when(c == last)
    def _():
        nconv_ref[0] = glu_s[hist + tq - (cwidth - 1):hist + tq, :]
        nret_ref[0] = state_s[...]


def _prompt_mix(x, g1, w_in, layer, conv_w, cvec, gn_g, cos2, sin2, heads, hd):
    B, T, D = x.shape
    cwidth, cdim = conv_w.shape
    rdim = heads * hd
    tq = PROMPT_CHUNK
    hist = -(-(cwidth - 1) // SUBLANES) * SUBLANES
    assert T % tq == 0 and tq >= hist and tq % CONV_ROWS == 0 and hd == LANES
    assert w_in.shape[2] % CAST_COLS == 0

    dec, rdec, kdec = _decay_tables(tq, heads)
    dec = jnp.asarray(dec, F32)
    rdec = jnp.asarray(np.broadcast_to(rdec, (heads, tq, hd)), F32)
    kdec = jnp.asarray(np.broadcast_to(kdec, (heads, tq, hd)), F32)

    kern = functools.partial(_prompt_mix_kernel, tq=tq, heads=heads, hd=hd, cdim=cdim,
                             cwidth=cwidth, hist=hist)
    return pl.pallas_call(
        kern,
        grid=(B, T // tq),
        in_specs=[
            pl.BlockSpec((1, tq, D), lambda b, c: (b, c, 0)),
            _const_spec((1, D)),
            pl.BlockSpec((None,) + w_in.shape[1:], lambda b, c: (layer, 0, 0), pipeline_mode=pl.Buffered(1)),
            _const_spec(conv_w.shape),
            _const_spec(cvec.shape),
            _const_spec((1, rdim)),
            pl.BlockSpec((tq, hd), lambda b, c: (c, 0)),
            pl.BlockSpec((tq, hd), lambda b, c: (c, 0)),
            _const_spec(dec.shape),
            _const_spec(rdec.shape),
            _const_spec(kdec.shape),
        ],
        out_specs=[
            pl.BlockSpec((1, tq, D), lambda b, c: (b, c, 0)),
            pl.BlockSpec((1, cwidth - 1, cdim), lambda b, c: (b, 0, 0)),
            pl.BlockSpec((1, heads, hd, hd), lambda b, c: (b, 0, 0, 0)),
        ],
        out_shape=[
            jax.ShapeDtypeStruct((B, T, D), BF16),
            jax.ShapeDtypeStruct((B, cwidth - 1, cdim), F32),
            jax.ShapeDtypeStruct((B, heads, hd, hd), F32),
        ],
        scratch_shapes=[
            pltpu.VMEM(w_in.shape[1:], BF16),
            pltpu.VMEM((hist + tq, cdim), F32),
            pltpu.VMEM((SUBLANES - 1, hist + tq, cdim), F32),
            pltpu.VMEM((tq, cdim), F32),
            pltpu.VMEM((tq, rdim), BF16),
            pltpu.VMEM((tq, rdim), F32),
            pltpu.VMEM((tq, rdim), BF16),
            pltpu.VMEM((tq, rdim), F32),
            pltpu.VMEM((heads, hd, hd), F32),
        ],
        compiler_params=pltpu.CompilerParams(
            dimension_semantics=("arbitrary", "arbitrary"), vmem_limit_bytes=VMEM_LIMIT),
        name="prompt_mix",
    )(x, g1, w_in, conv_w, cvec, gn_g, cos2, sin2, dec, rdec, kdec)


def _post_kernel(xa_ref, mixa_ref, xb_ref, mixb_ref, wout_ref, g2_ref, wup_ref, wdn_ref, gf_ref,
                 oa_ref, ob_ref, wout_s, wup_s, wdn_s, *, final, steps_w, steps_a):
    def tile(x_ref, mix_ref, o_ref):
        y = x_ref[...] + jnp.dot(mix_ref[...], wout_s[...], preferred_element_type=F32)
        h2 = _rmsnorm(y, g2_ref[...]).astype(BF16)
        hf = jnp.dot(h2, wup_s[...], preferred_element_type=F32)
        act = jnp.square(jnp.maximum(hf, 0.0)).astype(BF16)
        out = y + jnp.dot(act, wdn_s[...], preferred_element_type=F32)
        if final:
            out = _rmsnorm(out, gf_ref[...])
        o_ref[...] = out

    i = pl.program_id(0)

    @pl.when(i < steps_w)
    def _():
        for src, dst in ((wout_ref, wout_s), (wup_ref, wup_s), (wdn_ref, wdn_s)):
            rows = src.shape[0]
            dst[pl.ds(pl.multiple_of(i * rows, rows), rows), :] = src[...].astype(BF16)

    @pl.when((i >= steps_w) & (i < steps_w + steps_a))
    def _():
        tile(xa_ref, mixa_ref, oa_ref)

    @pl.when(i >= steps_w + steps_a)
    def _():
        tile(xb_ref, mixb_ref, ob_ref)


def _post(xa, mixa, xb, mixb, layer, w_out, g2, w_up, w_down, gf, final):
    Ra, D = xa.shape
    Rb = xb.shape[0]
    tr = POST_ROWS
    nw = POST_WEIGHT_STEPS
    assert Ra % tr == 0 and Rb % tr == 0
    na, nb = Ra // tr, Rb // tr
    rows_a = pl.BlockSpec((tr, D), lambda i: (jnp.clip(i - nw, 0, na - 1), 0))
    rows_b = pl.BlockSpec((tr, D), lambda i: (jnp.clip(i - nw - na, 0, nb - 1), 0))

    def slab_spec(w):
        assert w.shape[1] % (nw * 2 * SUBLANES) == 0
        return pl.BlockSpec((None, w.shape[1] // nw, w.shape[2]), lambda i: (layer, jnp.minimum(i, nw - 1), 0),
                            pipeline_mode=pl.Buffered(1))

    return pl.pallas_call(
        functools.partial(_post_kernel, final=final, steps_w=nw, steps_a=na),
        grid=(nw + na + nb,),
        in_specs=[
            rows_a, rows_a, rows_b, rows_b,
            slab_spec(w_out),
            _const_spec((1, D)),
            slab_spec(w_up),
            slab_spec(w_down),
            _const_spec((1, D)),
        ],
        out_specs=[rows_a, rows_b],
        out_shape=[jax.ShapeDtypeStruct((Ra, D), F32), jax.ShapeDtypeStruct((Rb, D), F32)],
        scratch_shapes=[pltpu.VMEM(w.shape[1:], BF16) for w in (w_out, w_up, w_down)],
        compiler_params=pltpu.CompilerParams(
            dimension_semantics=("arbitrary",), vmem_limit_bytes=VMEM_LIMIT),
        name="post",
    )(xa, mixa, xb, mixb, w_out, g2, w_up, w_down, gf)


def _in_proj_kernel(x_ref, g1_ref, w_ref, o_ref, h_s):
    @pl.when(pl.program_id(0) == 0)
    def _():
        h_s[...] = _rmsnorm(x_ref[...], g1_ref[...]).astype(BF16)

    o_ref[...] = jnp.dot(h_s[...], w_ref[...].astype(BF16), preferred_element_type=F32)


def _in_proj(x, g1, w_in, layer, ncol):
    R, D = x.shape
    C = w_in.shape[2]
    return pl.pallas_call(
        _in_proj_kernel,
        grid=(C // ncol,),
        in_specs=[
            _const_spec((R, D)),
            _const_spec((1, D)),
            pl.BlockSpec((None, D, ncol), lambda j: (layer, 0, j)),
        ],
        out_specs=pl.BlockSpec((R, ncol), lambda j: (0, j)),
        out_shape=jax.ShapeDtypeStruct((R, C), F32),
        scratch_shapes=[pltpu.VMEM((R, D), BF16)],
        compiler_params=pltpu.CompilerParams(
            dimension_semantics=("arbitrary",), vmem_limit_bytes=VMEM_LIMIT),
        name="sample_in_proj",
    )(x, g1, w_in)


def _sample_mix_kernel(proj_ref, cache_ref, st_ref, cw_ref, cvec_ref, gn_ref, cos_ref, sin_ref,
                       dec_ref, rdec_ref, kdec_ref, *rest, ts, nb, heads, hd, cdim, cwidth):
    mix_ref, ncache_ref, nst_ref = rest[-3:]
    rdim = heads * hd

    glu = [proj_ref[t, :, 0:cdim] * jax.nn.sigmoid(proj_ref[t, :, cdim:2 * cdim]) for t in range(ts)]

    def window(r):
        return cache_ref[r] if r < cwidth - 1 else glu[r - (cwidth - 1)]

    acc = [None] * ts
    for r in range(cwidth - 1 + ts):
        x = window(r)
        for t in range(max(0, r - cwidth + 1), min(ts, r + 1)):
            term = x * cw_ref[r - t:r - t + 1, :]
            acc[t] = term if acc[t] is None else acc[t] + term
    for t in range(ts):
        cn = _layernorm(acc[t] + cvec_ref[0:1, :]) * cvec_ref[1:2, :] + cvec_ref[2:3, :]
        mix_ref[t, :, 0:cdim] = _silu(cn).astype(BF16)
    for r in range(cwidth - 1):
        ncache_ref[r] = window(r + ts)

    c0 = 2 * cdim
    cos2 = cos_ref[...]
    sin2 = sin_ref[...]
    scale = hd ** -0.5
    for hh in range(heads):
        lo = hh * hd

        def per_seq(base):
            return jnp.stack([proj_ref[:, s, base + lo:base + lo + hd] for s in range(nb)], axis=0)

        q = _rotary(per_seq(c0), cos2, sin2)
        k = _rotary(per_seq(c0 + rdim), cos2, sin2) * scale
        v = per_seq(c0 + 2 * rdim)
        g = per_seq(c0 + 3 * rdim)
        st = st_ref[:, hh]
        scores = jnp.einsum('btd,bjd->btj', q, k, preferred_element_type=F32) * dec_ref[hh]
        inner = jnp.einsum('btj,bjv->btv', scores, v, preferred_element_type=F32)
        cross = jnp.einsum('btd,bdv->btv', q, st, preferred_element_type=F32) * rdec_ref[hh]
        upd = jnp.einsum('bjd,bjv->bdv', k * kdec_ref[hh], v, preferred_element_type=F32)
        nst_ref[:, hh] = st * math.exp(_log_gamma(hh) * ts) + upd
        on = _layernorm(inner + cross)
        out = (on * gn_ref[0:1, lo:lo + hd] * _silu(g)).astype(BF16)
        for s in range(nb):
            mix_ref[:, s, cdim + lo:cdim + lo + hd] = out[s]


def _sample_mix(proj, cache, state, layer, prev, conv_w, cvec, gn_g, cos2, sin2):
    ts, S, C = proj.shape
    depth, _, heads, hd, _ = state.shape
    cwidth, cdim = conv_w.shape
    rdim = heads * hd
    D = cdim + rdim
    nb = SAMPLE_SEQS
    assert S % nb == 0 and hd == LANES

    dec, rdec, kdec = (jnp.asarray(t, F32) for t in _decay_tables(ts, heads))

    conv_spec = pl.BlockSpec((None, cwidth - 1, nb, cdim), lambda i: (layer, 0, i, 0))
    st_spec = pl.BlockSpec((None, nb, heads, hd, hd), lambda i: (layer, i, 0, 0, 0))
    args = [proj, cache, state, conv_w, cvec, gn_g, cos2, sin2, dec, rdec, kdec]
    in_specs = [
        pl.BlockSpec((ts, nb, C), lambda i: (0, i, 0)),
        conv_spec,
        st_spec,
        _const_spec(conv_w.shape),
        _const_spec(cvec.shape),
        _const_spec((1, rdim)),
        _const_spec((ts, hd)),
        _const_spec((ts, hd)),
        _const_spec(dec.shape),
        _const_spec(rdec.shape),
        _const_spec(kdec.shape),
    ]
    aliases = {}
    if prev is not None:
        aliases = {len(args): 1, len(args) + 1: 2}
        args += list(prev)
        in_specs += [pl.BlockSpec(memory_space=pl.ANY)] * 2

    kern = functools.partial(_sample_mix_kernel, ts=ts, nb=nb, heads=heads, hd=hd, cdim=cdim, cwidth=cwidth)
    return pl.pallas_call(
        kern,
        grid=(S // nb,),
        in_specs=in_specs,
        out_specs=[pl.BlockSpec((ts, nb, D), lambda i: (0, i, 0)), conv_spec, st_spec],
        out_shape=[
            jax.ShapeDtypeStruct((ts, S, D), BF16),
            jax.ShapeDtypeStruct((depth, cwidth - 1, S, cdim), F32),
            jax.ShapeDtypeStruct((depth, S, heads, hd, hd), F32),
        ],
        input_output_aliases=aliases,
        compiler_params=pltpu.CompilerParams(
            dimension_semantics=("arbitrary",), vmem_limit_bytes=VMEM_LIMIT),
        name="sample_mix",
    )(*args)


def _rope_tables(pos0, T, hd):
    half = hd // 2
    pos = pos0 + jnp.arange(T, dtype=F32)
    inv = ROPE_BASE ** (-jnp.arange(half, dtype=F32) / half)
    ang = pos[:, None] * inv[None, :]
    cos, sin = jnp.cos(ang), jnp.sin(ang)
    return jnp.concatenate([cos, cos], axis=-1), jnp.concatenate([-sin, sin], axis=-1)


def kernel(x_prompt, x_sample, cache_conv, state_ret, norm1_g, w_in, conv_w, conv_b, conv_ln_g,
           conv_ln_b, ret_gn_g, w_out, norm2_g, w_up, w_down, final_norm_g):
    depth = w_in.shape[0]
    B, T, D = x_prompt.shape
    S, ts, _ = x_sample.shape
    heads, hd = state_ret.shape[2], state_ret.shape[3]

    cvec = jnp.stack([conv_b, conv_ln_g, conv_ln_b], axis=1)
    gf = final_norm_g[None, :]
    cos_p, sin_p = _rope_tables(0.0, T, hd)
    cos_s, sin_s = _rope_tables(float(PAST_LEN), ts, hd)

    xp = x_prompt.reshape(B * T, D)
    xs = x_sample.transpose(1, 0, 2).reshape(ts * S, D)
    cache_t = cache_conv.transpose(0, 2, 1, 3)
    conv_p, ret_p, sample_state = [], [], None
    for l in range(depth):
        final = l == depth - 1
        g1, g2, gn = norm1_g[l][None, :], norm2_g[l][None, :], ret_gn_g[l][None, :]

        mix_p, nconv, nret = _prompt_mix(xp.reshape(B, T, D), g1, w_in, l, conv_w[l], cvec[l], gn,
                                         cos_p, sin_p, heads, hd)
        conv_p.append(nconv)
        ret_p.append(nret)

        proj = _in_proj(xs, g1, w_in, l, 512).reshape(ts, S, -1)
        mix_s, *sample_state = _sample_mix(proj, cache_t, state_ret, l, sample_state, conv_w[l], cvec[l],
                                           gn, cos_s, sin_s)

        xp, xs = _post(xp, mix_p.reshape(B * T, D), xs, mix_s.reshape(ts * S, D), l, w_out, g2,
                       w_up, w_down, gf, final)

    new_cache_t, new_state = sample_state
    return (xp.reshape(B, T, D), xs.reshape(ts, S, D).transpose(1, 0, 2), jnp.stack(conv_p),
            jnp.stack(ret_p), new_cache_t.transpose(0, 2, 1, 3), new_state)
```

```python
import functools
import math

import numpy as np
import jax
import jax.numpy as jnp
from jax import lax
from jax.experimental import pallas as pl
from jax.experimental.pallas import tpu as pltpu

F32 = jnp.float32
BF16 = jnp.bfloat16

EPS = 1e-6
ROPE_BASE = 10000.0
PAST_LEN = 16384
LANES = 128
SUBLANES = 8
VMEM_LIMIT = 56 * 1024 * 1024

PROMPT_CHUNK = 512
CONV_ROWS = 128
GLU_LANE_BLOCKS = 2
POST_ROWS = 512
SAMPLE_SEQS = 16
CAST_COLS = 512


def _log_gamma(h):
    return math.log(1.0 - 2.0 ** (-5.0 - h))


def _rmsnorm(x, g):
    ms = jnp.mean(x * x, axis=-1, keepdims=True)
    return x * lax.rsqrt(ms + EPS) * g


def _layernorm(x):
    mu = jnp.mean(x, axis=-1, keepdims=True)
    xc = x - mu
    var = jnp.mean(xc * xc, axis=-1, keepdims=True)
    return xc * lax.rsqrt(var + EPS)


def _silu(x):
    return x * jax.nn.sigmoid(x)


def _rotary(t, cos2, sin2):
    half = t.shape[-1] // 2
    return t * cos2 + pltpu.roll(t, half, t.ndim - 1) * sin2


def _const_spec(shape):
    nd = len(shape)
    return pl.BlockSpec(shape, lambda *_: (0,) * nd, pipeline_mode=pl.Buffered(1))


def _decay_tables(n, heads):
    idx = np.arange(n, dtype=np.float64)
    diff = idx[:, None] - idx[None, :]
    lg = np.array([_log_gamma(h) for h in range(heads)])
    dec = np.where(diff[None] >= 0, np.exp(lg[:, None, None] * np.maximum(diff, 0.0)[None]), 0.0)
    rdec = np.exp(lg[:, None] * (idx + 1.0))[:, :, None]
    kdec = np.exp(lg[:, None] * (n - 1.0 - idx))[:, :, None]
    return dec, rdec, kdec


def _prompt_mix_kernel(x_ref, g1_ref, win_ref, cw_ref, cvec_ref, gn_ref, cos_ref, sin_ref,
                       dec_ref, rdec_ref, kdec_ref, wo_ref, wu_ref, wd_ref,
                       mix_ref, nconv_ref, nret_ref, wob_ref, wub_ref, wdb_ref,
                       w_s, glu_s, shift_s, conv_s, q_s, k_s, v_s, g_s, state_s,
                       *, tq, heads, hd, cdim, cwidth, hist):
    b = pl.program_id(0)
    c = pl.program_id(1)
    last = pl.num_programs(1) - 1
    rdim = heads * hd

    @pl.when((b == 0) & (c == 0))
    def _():
        for lo in range(0, w_s.shape[1], CAST_COLS):
            w_s[:, lo:lo + CAST_COLS] = win_ref[:, lo:lo + CAST_COLS].astype(BF16)

    @pl.when(c == 0)
    def _():
        glu_s[0:hist, :] = jnp.zeros((hist, cdim), F32)
        shift_s[:, 0:hist, :] = jnp.zeros((SUBLANES - 1, hist, cdim), F32)
        shift_s[:, tq:tq + hist, :] = jnp.zeros((SUBLANES - 1, hist, cdim), F32)
        state_s[...] = jnp.zeros_like(state_s)

    @pl.when(c > 0)
    def _():
        glu_s[0:hist, :] = glu_s[tq:tq + hist, :]
        shift_s[:, 0:hist, :] = shift_s[:, tq:tq + hist, :]

    wob_ref[...] = wo_ref[...].astype(BF16)
    wub_ref[...] = wu_ref[...].astype(BF16)
    wdb_ref[...] = wd_ref[...].astype(BF16)

    h = _rmsnorm(x_ref[0], g1_ref[...]).astype(BF16)

    def proj(lo, n):
        return jnp.dot(h, w_s[:, lo:lo + n], preferred_element_type=F32)

    off = hist - (cwidth - 1)
    for lb in range(cdim // LANES):
        ls = slice(lb * LANES, (lb + 1) * LANES)
        if lb % GLU_LANE_BLOCKS == 0:
            n = GLU_LANE_BLOCKS * LANES
            cs = slice(lb * LANES, lb * LANES + n)
            glu = proj(lb * LANES, n) * jax.nn.sigmoid(proj(cdim + lb * LANES, n))
            glu_s[hist:hist + tq, cs] = glu
            for ph in range(1, SUBLANES):
                shift_s[ph - 1, hist - ph:hist - ph + tq, cs] = glu
        for r0 in range(0, tq, CONV_ROWS):
            acc = None
            for w in range(cwidth):
                ph = (off + w) % SUBLANES
                a = r0 + (off + w) // SUBLANES * SUBLANES
                xs = glu_s[a:a + CONV_ROWS, ls] if ph == 0 else shift_s[ph - 1, a:a + CONV_ROWS, ls]
                term = xs * cw_ref[w:w + 1, ls]
                acc = term if acc is None else acc + term
            conv_s[r0:r0 + CONV_ROWS, ls] = acc + cvec_ref[0:1, ls]
    cn = _layernorm(conv_s[...]) * cvec_ref[1:2, :] + cvec_ref[2:3, :]
    mix_ref[0, :, 0:cdim] = _silu(cn).astype(BF16)

    c0 = 2 * cdim
    qf = proj(c0, rdim)
    kf = proj(c0 + rdim, rdim)
    cos2 = cos_ref[...]
    sin2 = sin_ref[...]
    scale = hd ** -0.5
    for hh in range(heads):
        hs = slice(hh * hd, (hh + 1) * hd)
        q_s[:, hs] = _rotary(qf[:, hs], cos2, sin2).astype(BF16)
        k_s[:, hs] = _rotary(kf[:, hs], cos2, sin2) * scale
    v_s[...] = proj(c0 + 2 * rdim, rdim).astype(BF16)
    g_s[...] = _silu(proj(c0 + 3 * rdim, rdim))

    for hh in range(heads):
        hs = slice(hh * hd, (hh + 1) * hd)
        qh = q_s[:, hs]
        kh = k_s[:, hs]
        vh = v_s[:, hs]
        st = state_s[hh]
        scores = lax.dot_general(qh, kh.astype(BF16), (((1,), (1,)), ((), ())),
                                 preferred_element_type=F32) * dec_ref[hh]
        inner = jnp.dot(scores.astype(BF16), vh, preferred_element_type=F32)
        cross = jnp.dot(qh, st.astype(BF16), preferred_element_type=F32) * rdec_ref[hh]
        kd = (kh * kdec_ref[hh]).astype(BF16)
        upd = lax.dot_general(kd, vh, (((0,), (0,)), ((), ())), preferred_element_type=F32)
        state_s[hh] = st * math.exp(_log_gamma(hh) * tq) + upd
        on = _layernorm(inner + cross)
        mix_ref[0, :, cdim + hh * hd:cdim + (hh + 1) * hd] = (on * gn_ref[0:1, hs] * g_s[:, hs]).astype(BF16)

    @pl.when(c == last)
    def _():
        nconv_ref[0] = glu_s[hist + tq - (cwidth - 1):hist + tq, :]
        nret_ref[0] = state_s[...]


def _prompt_mix(x, g1, w_in, layer, conv_w, cvec, gn_g, cos2, sin2, heads, hd, post_weights):
    B, T, D = x.shape
    cwidth, cdim = conv_w.shape
    rdim = heads * hd
    tq = PROMPT_CHUNK
    hist = -(-(cwidth - 1) // SUBLANES) * SUBLANES
    assert T % tq == 0 and tq >= hist and tq % CONV_ROWS == 0 and hd == LANES
    assert w_in.shape[2] % CAST_COLS == 0

    dec, rdec, kdec = _decay_tables(tq, heads)
    dec = jnp.asarray(dec, F32)
    rdec = jnp.asarray(np.broadcast_to(rdec, (heads, tq, hd)), F32)
    kdec = jnp.asarray(np.broadcast_to(kdec, (heads, tq, hd)), F32)

    chunks = T // tq
    steps = B * chunks
    for w in post_weights:
        assert w.shape[1] % (steps * 2 * SUBLANES) == 0
    slab_in = [pl.BlockSpec((None, w.shape[1] // steps, w.shape[2]), lambda b, c: (layer, b * chunks + c, 0))
               for w in post_weights]
    slab_out = [pl.BlockSpec((w.shape[1] // steps, w.shape[2]), lambda b, c: (b * chunks + c, 0))
                for w in post_weights]

    kern = functools.partial(_prompt_mix_kernel, tq=tq, heads=heads, hd=hd, cdim=cdim,
                             cwidth=cwidth, hist=hist)
    return pl.pallas_call(
        kern,
        grid=(B, chunks),
        in_specs=[
            pl.BlockSpec((1, tq, D), lambda b, c: (b, c, 0)),
            _const_spec((1, D)),
            pl.BlockSpec((None,) + w_in.shape[1:], lambda b, c: (layer, 0, 0), pipeline_mode=pl.Buffered(1)),
            _const_spec(conv_w.shape),
            _const_spec(cvec.shape),
            _const_spec((1, rdim)),
            pl.BlockSpec((tq, hd), lambda b, c: (c, 0)),
            pl.BlockSpec((tq, hd), lambda b, c: (c, 0)),
            _const_spec(dec.shape),
            _const_spec(rdec.shape),
            _const_spec(kdec.shape),
            *slab_in,
        ],
        out_specs=[
            pl.BlockSpec((1, tq, D), lambda b, c: (b, c, 0)),
            pl.BlockSpec((1, cwidth - 1, cdim), lambda b, c: (b, 0, 0)),
            pl.BlockSpec((1, heads, hd, hd), lambda b, c: (b, 0, 0, 0)),
            *slab_out,
        ],
        out_shape=[
            jax.ShapeDtypeStruct((B, T, D), BF16),
            jax.ShapeDtypeStruct((B, cwidth - 1, cdim), F32),
            jax.ShapeDtypeStruct((B, heads, hd, hd), F32),
            *[jax.ShapeDtypeStruct(w.shape[1:], BF16) for w in post_weights],
        ],
        scratch_shapes=[
            pltpu.VMEM(w_in.shape[1:], BF16),
            pltpu.VMEM((hist + tq, cdim), F32),
            pltpu.VMEM((SUBLANES - 1, hist + tq, cdim), F32),
            pltpu.VMEM((tq, cdim), F32),
            pltpu.VMEM((tq, rdim), BF16),
            pltpu.VMEM((tq, rdim), F32),
            pltpu.VMEM((tq, rdim), BF16),
            pltpu.VMEM((tq, rdim), F32),
            pltpu.VMEM((heads, hd, hd), F32),
        ],
        compiler_params=pltpu.CompilerParams(
            dimension_semantics=("arbitrary", "arbitrary"), vmem_limit_bytes=VMEM_LIMIT),
        name="prompt_mix",
    )(x, g1, w_in, conv_w, cvec, gn_g, cos2, sin2, dec, rdec, kdec, *post_weights)


def _post_kernel(xa_ref, mixa_ref, xb_ref, mixb_ref, wout_ref, g2_ref, wup_ref, wdn_ref, gf_ref,
                 oa_ref, ob_ref, *, final, steps_a):
    def tile(x_ref, mix_ref, o_ref):
        y = x_ref[...] + jnp.dot(mix_ref[...], wout_ref[...], preferred_element_type=F32)
        h2 = _rmsnorm(y, g2_ref[...]).astype(BF16)
        hf = jnp.dot(h2, wup_ref[...], preferred_element_type=F32)
        act = jnp.square(jnp.maximum(hf, 0.0)).astype(BF16)
        out = y + jnp.dot(act, wdn_ref[...], preferred_element_type=F32)
        if final:
            out = _rmsnorm(out, gf_ref[...])
        o_ref[...] = out

    i = pl.program_id(0)

    @pl.when(i < steps_a)
    def _():
        tile(xa_ref, mixa_ref, oa_ref)

    @pl.when(i >= steps_a)
    def _():
        tile(xb_ref, mixb_ref, ob_ref)


def _post(xa, mixa, xb, mixb, w_out, g2, w_up, w_down, gf, final):
    Ra, D = xa.shape
    Rb = xb.shape[0]
    tr = POST_ROWS
    assert Ra % tr == 0 and Rb % tr == 0
    na, nb = Ra // tr, Rb // tr
    rows_a = pl.BlockSpec((tr, D), lambda i: (jnp.minimum(i, na - 1), 0))
    rows_b = pl.BlockSpec((tr, D), lambda i: (jnp.maximum(i - na, 0), 0))

    return pl.pallas_call(
        functools.partial(_post_kernel, final=final, steps_a=na),
        grid=(na + nb,),
        in_specs=[
            rows_a, rows_a, rows_b, rows_b,
            _const_spec(w_out.shape),
            _const_spec((1, D)),
            _const_spec(w_up.shape),
            _const_spec(w_down.shape),
            _const_spec((1, D)),
        ],
        out_specs=[rows_a, rows_b],
        out_shape=[jax.ShapeDtypeStruct((Ra, D), F32), jax.ShapeDtypeStruct((Rb, D), F32)],
        compiler_params=pltpu.CompilerParams(
            dimension_semantics=("arbitrary",), vmem_limit_bytes=VMEM_LIMIT),
        name="post",
    )(xa, mixa, xb, mixb, w_out, g2, w_up, w_down, gf)


def _in_proj_kernel(x_ref, g1_ref, w_ref, o_ref, h_s):
    @pl.when(pl.program_id(0) == 0)
    def _():
        h_s[...] = _rmsnorm(x_ref[...], g1_ref[...]).astype(BF16)

    o_ref[...] = jnp.dot(h_s[...], w_ref[...].astype(BF16), preferred_element_type=F32)


def _in_proj(x, g1, w_in, layer, ncol):
    R, D = x.shape
    C = w_in.shape[2]
    return pl.pallas_call(
        _in_proj_kernel,
        grid=(C // ncol,),
        in_specs=[
            _const_spec((R, D)),
            _const_spec((1, D)),
            pl.BlockSpec((None, D, ncol), lambda j: (layer, 0, j)),
        ],
        out_specs=pl.BlockSpec((R, ncol), lambda j: (0, j)),
        out_shape=jax.ShapeDtypeStruct((R, C), F32),
        scratch_shapes=[pltpu.VMEM((R, D), BF16)],
        compiler_params=pltpu.CompilerParams(
            dimension_semantics=("arbitrary",), vmem_limit_bytes=VMEM_LIMIT),
        name="sample_in_proj",
    )(x, g1, w_in)


def _sample_mix_kernel(proj_ref, cache_ref, st_ref, cw_ref, cvec_ref, gn_ref, cos_ref, sin_ref,
                       dec_ref, rdec_ref, kdec_ref, *rest, ts, nb, heads, hd, cdim, cwidth):
    mix_ref, ncache_ref, nst_ref = rest[-3:]
    rdim = heads * hd

    glu = [proj_ref[t, :, 0:cdim] * jax.nn.sigmoid(proj_ref[t, :, cdim:2 * cdim]) for t in range(ts)]

    def window(r):
        return cache_ref[r] if r < cwidth - 1 else glu[r - (cwidth - 1)]

    acc = [None] * ts
    for r in range(cwidth - 1 + ts):
        x = window(r)
        for t in range(max(0, r - cwidth + 1), min(ts, r + 1)):
            term = x * cw_ref[r - t:r - t + 1, :]
            acc[t] = term if acc[t] is None else acc[t] + term
    for t in range(ts):
        cn = _layernorm(acc[t] + cvec_ref[0:1, :]) * cvec_ref[1:2, :] + cvec_ref[2:3, :]
        mix_ref[t, :, 0:cdim] = _silu(cn).astype(BF16)
    for r in range(cwidth - 1):
        ncache_ref[r] = window(r + ts)

    c0 = 2 * cdim
    cos2 = cos_ref[...]
    sin2 = sin_ref[...]
    scale = hd ** -0.5
    for hh in range(heads):
        lo = hh * hd

        def per_seq(base):
            return jnp.stack([proj_ref[:, s, base + lo:base + lo + hd] for s in range(nb)], axis=0)

        q = _rotary(per_seq(c0), cos2, sin2)
        k = _rotary(per_seq(c0 + rdim), cos2, sin2) * scale
        v = per_seq(c0 + 2 * rdim)
        g = per_seq(c0 + 3 * rdim)
        st = st_ref[:, hh]
        scores = jnp.einsum('btd,bjd->btj', q, k, preferred_element_type=F32) * dec_ref[hh]
        inner = jnp.einsum('btj,bjv->btv', scores, v, preferred_element_type=F32)
        cross = jnp.einsum('btd,bdv->btv', q, st, preferred_element_type=F32) * rdec_ref[hh]
        upd = jnp.einsum('bjd,bjv->bdv', k * kdec_ref[hh], v, preferred_element_type=F32)
        nst_ref[:, hh] = st * math.exp(_log_gamma(hh) * ts) + upd
        on = _layernorm(inner + cross)
        out = (on * gn_ref[0:1, lo:lo + hd] * _silu(g)).astype(BF16)
        for s in range(nb):
            mix_ref[:, s, cdim + lo:cdim + lo + hd] = out[s]


def _sample_mix(proj, cache, state, layer, prev, conv_w, cvec, gn_g, cos2, sin2):
    ts, S, C = proj.shape
    depth, _, heads, hd, _ = state.shape
    cwidth, cdim = conv_w.shape
    rdim = heads * hd
    D = cdim + rdim
    nb = SAMPLE_SEQS
    assert S % nb == 0 and hd == LANES

    dec, rdec, kdec = (jnp.asarray(t, F32) for t in _decay_tables(ts, heads))

    conv_spec = pl.BlockSpec((None, cwidth - 1, nb, cdim), lambda i: (layer, 0, i, 0))
    st_spec = pl.BlockSpec((None, nb, heads, hd, hd), lambda i: (layer, i, 0, 0, 0))
    args = [proj, cache, state, conv_w, cvec, gn_g, cos2, sin2, dec, rdec, kdec]
    in_specs = [
        pl.BlockSpec((ts, nb, C), lambda i: (0, i, 0)),
        conv_spec,
        st_spec,
        _const_spec(conv_w.shape),
        _const_spec(cvec.shape),
        _const_spec((1, rdim)),
        _const_spec((ts, hd)),
        _const_spec((ts, hd)),
        _const_spec(dec.shape),
        _const_spec(rdec.shape),
        _const_spec(kdec.shape),
    ]
    aliases = {}
    if prev is not None:
        aliases = {len(args): 1, len(args) + 1: 2}
        args += list(prev)
        in_specs += [pl.BlockSpec(memory_space=pl.ANY)] * 2

    kern = functools.partial(_sample_mix_kernel, ts=ts, nb=nb, heads=heads, hd=hd, cdim=cdim, cwidth=cwidth)
    return pl.pallas_call(
        kern,
        grid=(S // nb,),
        in_specs=in_specs,
        out_specs=[pl.BlockSpec((ts, nb, D), lambda i: (0, i, 0)), conv_spec, st_spec],
        out_shape=[
            jax.ShapeDtypeStruct((ts, S, D), BF16),
            jax.ShapeDtypeStruct((depth, cwidth - 1, S, cdim), F32),
            jax.ShapeDtypeStruct((depth, S, heads, hd, hd), F32),
        ],
        input_output_aliases=aliases,
        compiler_params=pltpu.CompilerParams(
            dimension_semantics=("arbitrary",), vmem_limit_bytes=VMEM_LIMIT),
        name="sample_mix",
    )(*args)


def _rope_tables(pos0, T, hd):
    half = hd // 2
    pos = pos0 + jnp.arange(T, dtype=F32)
    inv = ROPE_BASE ** (-jnp.arange(half, dtype=F32) / half)
    ang = pos[:, None] * inv[None, :]
    cos, sin = jnp.cos(ang), jnp.sin(ang)
    return jnp.concatenate([cos, cos], axis=-1), jnp.concatenate([-sin, sin], axis=-1)


def kernel(x_prompt, x_sample, cache_conv, state_ret, norm1_g, w_in, conv_w, conv_b, conv_ln_g,
           conv_ln_b, ret_gn_g, w_out, norm2_g, w_up, w_down, final_norm_g):
    depth = w_in.shape[0]
    B, T, D = x_prompt.shape
    S, ts, _ = x_sample.shape
    heads, hd = state_ret.shape[2], state_ret.shape[3]

    cvec = jnp.stack([conv_b, conv_ln_g, conv_ln_b], axis=1)
    gf = final_norm_g[None, :]
    cos_p, sin_p = _rope_tables(0.0, T, hd)
    cos_s, sin_s = _rope_tables(float(PAST_LEN), ts, hd)

    xp = x_prompt.reshape(B * T, D)
    xs = x_sample.transpose(1, 0, 2).reshape(ts * S, D)
    cache_t = cache_conv.transpose(0, 2, 1, 3)
    conv_p, ret_p, sample_state = [], [], None
    for l in range(depth):
        final = l == depth - 1
        g1, g2, gn = norm1_g[l][None, :], norm2_g[l][None, :], ret_gn_g[l][None, :]

        mix_p, nconv, nret, *post_w = _prompt_mix(xp.reshape(B, T, D), g1, w_in, l, conv_w[l], cvec[l], gn,
                                                  cos_p, sin_p, heads, hd, (w_out, w_up, w_down))
        conv_p.append(nconv)
        ret_p.append(nret)

        proj = _in_proj(xs, g1, w_in, l, 512).reshape(ts, S, -1)
        mix_s, *sample_state = _sample_mix(proj, cache_t, state_ret, l, sample_state, conv_w[l], cvec[l],
                                           gn, cos_s, sin_s)

        w_out_b, w_up_b, w_dn_b = post_w
        xp, xs = _post(xp, mix_p.reshape(B * T, D), xs, mix_s.reshape(ts * S, D), w_out_b, g2,
                       w_up_b, w_dn_b, gf, final)

    new_cache_t, new_state = sample_state
    return (xp.reshape(B, T, D), xs.reshape(ts, S, D).transpose(1, 0, 2), jnp.stack(conv_p),
            jnp.stack(ret_p), new_cache_t.transpose(0, 2, 1, 3), new_state)
```

```python
import functools
import math

import numpy as np
import jax
import jax.numpy as jnp
from jax import lax
from jax.experimental import pallas as pl
from jax.experimental.pallas import tpu as pltpu

F32 = jnp.float32
BF16 = jnp.bfloat16

EPS = 1e-6
ROPE_BASE = 10000.0
PAST_LEN = 16384
LANES = 128
SUBLANES = 8
VMEM_LIMIT = 56 * 1024 * 1024

PROMPT_CHUNK = 512
RET_CHUNK = 512
CONV_ROWS = 128
GLU_LANE_BLOCKS = 2
POST_ROWS = 512
SAMPLE_SEQS = 16
CAST_COLS = 512


def _log_gamma(h):
    return math.log(1.0 - 2.0 ** (-5.0 - h))


def _rmsnorm(x, g):
    ms = jnp.mean(x * x, axis=-1, keepdims=True)
    return x * lax.rsqrt(ms + EPS) * g


def _layernorm(x):
    mu = jnp.mean(x, axis=-1, keepdims=True)
    xc = x - mu
    var = jnp.mean(xc * xc, axis=-1, keepdims=True)
    return xc * lax.rsqrt(var + EPS)


def _silu(x):
    return x * jax.nn.sigmoid(x)


def _rotary(t, cos2, sin2):
    half = t.shape[-1] // 2
    return t * cos2 + pltpu.roll(t, half, t.ndim - 1) * sin2


def _const_spec(shape):
    nd = len(shape)
    return pl.BlockSpec(shape, lambda *_: (0,) * nd, pipeline_mode=pl.Buffered(1))


def _decay_tables(n, heads):
    idx = np.arange(n, dtype=np.float64)
    diff = idx[:, None] - idx[None, :]
    lg = np.array([_log_gamma(h) for h in range(heads)])
    dec = np.where(diff[None] >= 0, np.exp(lg[:, None, None] * np.maximum(diff, 0.0)[None]), 0.0)
    rdec = np.exp(lg[:, None] * (idx + 1.0))[:, :, None]
    kdec = np.exp(lg[:, None] * (n - 1.0 - idx))[:, :, None]
    return dec, rdec, kdec


def _prompt_mix_kernel(x_ref, g1_ref, win_ref, cw_ref, cvec_ref, gn_ref, cos_ref, sin_ref,
                       dec_ref, rdec_ref, kdec_ref, wo_ref, wu_ref, wd_ref, *rest,
                       tq, sub, heads, hd, cdim, cwidth, hist):
    mix_ref, nconv_ref, nret_ref, wob_ref, wub_ref, wdb_ref = rest[-15:-9]
    w_s, glu_s, shift_s, conv_s, q_s, k_s, v_s, g_s, state_s = rest[-9:]
    b = pl.program_id(0)
    c = pl.program_id(1)
    last = pl.num_programs(1) - 1
    rdim = heads * hd

    @pl.when((b == 0) & (c == 0))
    def _():
        for lo in range(0, w_s.shape[1], CAST_COLS):
            w_s[:, lo:lo + CAST_COLS] = win_ref[:, lo:lo + CAST_COLS].astype(BF16)

    @pl.when(c == 0)
    def _():
        glu_s[0:hist, :] = jnp.zeros((hist, cdim), F32)
        shift_s[:, 0:hist, :] = jnp.zeros((SUBLANES - 1, hist, cdim), F32)
        shift_s[:, tq:tq + hist, :] = jnp.zeros((SUBLANES - 1, hist, cdim), F32)
        state_s[...] = jnp.zeros_like(state_s)

    @pl.when(c > 0)
    def _():
        glu_s[0:hist, :] = glu_s[tq:tq + hist, :]
        shift_s[:, 0:hist, :] = shift_s[:, tq:tq + hist, :]

    wob_ref[...] = wo_ref[...].astype(BF16)
    wub_ref[...] = wu_ref[...].astype(BF16)
    wdb_ref[...] = wd_ref[...].astype(BF16)

    h = _rmsnorm(x_ref[0], g1_ref[...]).astype(BF16)

    def proj(lo, n):
        return jnp.dot(h, w_s[:, lo:lo + n], preferred_element_type=F32)

    off = hist - (cwidth - 1)
    for lb in range(cdim // LANES):
        ls = slice(lb * LANES, (lb + 1) * LANES)
        if lb % GLU_LANE_BLOCKS == 0:
            n = GLU_LANE_BLOCKS * LANES
            cs = slice(lb * LANES, lb * LANES + n)
            glu = proj(lb * LANES, n) * jax.nn.sigmoid(proj(cdim + lb * LANES, n))
            glu_s[hist:hist + tq, cs] = glu
            for ph in range(1, SUBLANES):
                shift_s[ph - 1, hist - ph:hist - ph + tq, cs] = glu
        for r0 in range(0, tq, CONV_ROWS):
            acc = None
            for w in range(cwidth):
                ph = (off + w) % SUBLANES
                a = r0 + (off + w) // SUBLANES * SUBLANES
                xs = glu_s[a:a + CONV_ROWS, ls] if ph == 0 else shift_s[ph - 1, a:a + CONV_ROWS, ls]
                term = xs * cw_ref[w:w + 1, ls]
                acc = term if acc is None else acc + term
            conv_s[r0:r0 + CONV_ROWS, ls] = acc + cvec_ref[0:1, ls]
    cn = _layernorm(conv_s[...]) * cvec_ref[1:2, :] + cvec_ref[2:3, :]
    mix_ref[0, :, 0:cdim] = _silu(cn).astype(BF16)

    c0 = 2 * cdim
    qf = proj(c0, rdim)
    kf = proj(c0 + rdim, rdim)
    cos2 = cos_ref[...]
    sin2 = sin_ref[...]
    scale = hd ** -0.5
    for hh in range(heads):
        hs = slice(hh * hd, (hh + 1) * hd)
        q_s[:, hs] = _rotary(qf[:, hs], cos2, sin2).astype(BF16)
        k_s[:, hs] = _rotary(kf[:, hs], cos2, sin2) * scale
    v_s[...] = proj(c0 + 2 * rdim, rdim).astype(BF16)
    g_s[...] = _silu(proj(c0 + 3 * rdim, rdim))

    for hh in range(heads):
        hs = slice(hh * hd, (hh + 1) * hd)
        st = state_s[hh]
        for r0 in range(0, tq, sub):
            rs = slice(r0, r0 + sub)
            qh = q_s[rs, hs]
            kh = k_s[rs, hs]
            vh = v_s[rs, hs]
            scores = lax.dot_general(qh, kh.astype(BF16), (((1,), (1,)), ((), ())),
                                     preferred_element_type=F32) * dec_ref[hh]
            inner = jnp.dot(scores.astype(BF16), vh, preferred_element_type=F32)
            cross = jnp.dot(qh, st.astype(BF16), preferred_element_type=F32) * rdec_ref[hh]
            kd = (kh * kdec_ref[hh]).astype(BF16)
            upd = lax.dot_general(kd, vh, (((0,), (0,)), ((), ())), preferred_element_type=F32)
            st = st * math.exp(_log_gamma(hh) * sub) + upd
            on = _layernorm(inner + cross)
            mix_ref[0, rs, cdim + hh * hd:cdim + (hh + 1) * hd] = (
                on * gn_ref[0:1, hs] * g_s[rs, hs]).astype(BF16)
        state_s[hh] = st

    @pl.when(c == last)
    def _():
        nconv_ref[0] = glu_s[hist + tq - (cwidth - 1):hist + tq, :]
        nret_ref[0] = state_s[...]


def _prompt_mix(x, g1, w_in, layer, prev, conv_w, cvec, gn_g, cos2, sin2, heads, hd, post_weights):
    B, T, D = x.shape
    depth = w_in.shape[0]
    cwidth, cdim = conv_w.shape
    rdim = heads * hd
    tq = PROMPT_CHUNK
    sub = RET_CHUNK
    hist = -(-(cwidth - 1) // SUBLANES) * SUBLANES
    assert T % tq == 0 and tq >= hist and tq % CONV_ROWS == 0 and tq % sub == 0 and hd == LANES
    assert w_in.shape[2] % CAST_COLS == 0

    dec, rdec, kdec = _decay_tables(sub, heads)
    dec = jnp.asarray(dec, F32)
    rdec = jnp.asarray(np.broadcast_to(rdec, (heads, sub, hd)), F32)
    kdec = jnp.asarray(np.broadcast_to(kdec, (heads, sub, hd)), F32)

    chunks = T // tq
    steps = B * chunks
    for w in post_weights:
        assert w.shape[1] % (steps * 2 * SUBLANES) == 0
    slab_in = [pl.BlockSpec((None, w.shape[1] // steps, w.shape[2]), lambda b, c: (layer, b * chunks + c, 0))
               for w in post_weights]
    slab_out = [pl.BlockSpec((w.shape[1] // steps, w.shape[2]), lambda b, c: (b * chunks + c, 0))
                for w in post_weights]

    args = [x, g1, w_in, conv_w, cvec, gn_g, cos2, sin2, dec, rdec, kdec, *post_weights]
    in_specs = [
        pl.BlockSpec((1, tq, D), lambda b, c: (b, c, 0)),
        _const_spec((1, D)),
        pl.BlockSpec((None,) + w_in.shape[1:], lambda b, c: (layer, 0, 0), pipeline_mode=pl.Buffered(1)),
        _const_spec(conv_w.shape),
        _const_spec(cvec.shape),
        _const_spec((1, rdim)),
        pl.BlockSpec((tq, hd), lambda b, c: (c, 0)),
        pl.BlockSpec((tq, hd), lambda b, c: (c, 0)),
        _const_spec(dec.shape),
        _const_spec(rdec.shape),
        _const_spec(kdec.shape),
        *slab_in,
    ]
    aliases = {}
    if prev is not None:
        aliases = {len(args): 1, len(args) + 1: 2}
        args += list(prev)
        in_specs += [pl.BlockSpec(memory_space=pl.ANY)] * 2

    kern = functools.partial(_prompt_mix_kernel, tq=tq, sub=sub, heads=heads, hd=hd, cdim=cdim,
                             cwidth=cwidth, hist=hist)
    return pl.pallas_call(
        kern,
        grid=(B, chunks),
        in_specs=in_specs,
        out_specs=[
            pl.BlockSpec((1, tq, D), lambda b, c: (b, c, 0)),
            pl.BlockSpec((None, 1, cwidth - 1, cdim), lambda b, c: (layer, b, 0, 0)),
            pl.BlockSpec((None, 1, heads, hd, hd), lambda b, c: (layer, b, 0, 0, 0)),
            *slab_out,
        ],
        out_shape=[
            jax.ShapeDtypeStruct((B, T, D), BF16),
            jax.ShapeDtypeStruct((depth, B, cwidth - 1, cdim), F32),
            jax.ShapeDtypeStruct((depth, B, heads, hd, hd), F32),
            *[jax.ShapeDtypeStruct(w.shape[1:], BF16) for w in post_weights],
        ],
        input_output_aliases=aliases,
        scratch_shapes=[
            pltpu.VMEM(w_in.shape[1:], BF16),
            pltpu.VMEM((hist + tq, cdim), F32),
            pltpu.VMEM((SUBLANES - 1, hist + tq, cdim), F32),
            pltpu.VMEM((tq, cdim), F32),
            pltpu.VMEM((tq, rdim), BF16),
            pltpu.VMEM((tq, rdim), F32),
            pltpu.VMEM((tq, rdim), BF16),
            pltpu.VMEM((tq, rdim), F32),
            pltpu.VMEM((heads, hd, hd), F32),
        ],
        compiler_params=pltpu.CompilerParams(
            dimension_semantics=("arbitrary", "arbitrary"), vmem_limit_bytes=VMEM_LIMIT),
        name="prompt_mix",
    )(*args)


def _post_kernel(xa_ref, mixa_ref, xb_ref, mixb_ref, wout_ref, g2_ref, wup_ref, wdn_ref, gf_ref,
                 oa_ref, ob_ref, *, final, steps_a):
    def tile(x_ref, mix_ref, o_ref):
        y = x_ref[...] + jnp.dot(mix_ref[...], wout_ref[...], preferred_element_type=F32)
        h2 = _rmsnorm(y, g2_ref[...]).astype(BF16)
        hf = jnp.dot(h2, wup_ref[...], preferred_element_type=F32)
        act = jnp.square(jnp.maximum(hf, 0.0)).astype(BF16)
        out = y + jnp.dot(act, wdn_ref[...], preferred_element_type=F32)
        if final:
            out = _rmsnorm(out, gf_ref[...])
        o_ref[...] = out

    i = pl.program_id(0)

    @pl.when(i < steps_a)
    def _():
        tile(xa_ref, mixa_ref, oa_ref)

    @pl.when(i >= steps_a)
    def _():
        tile(xb_ref, mixb_ref, ob_ref)


def _post(xa, mixa, xb, mixb, w_out, g2, w_up, w_down, gf, final):
    Ra, D = xa.shape
    Rb = xb.shape[0]
    tr = POST_ROWS
    assert Ra % tr == 0 and Rb % tr == 0
    na, nb = Ra // tr, Rb // tr
    rows_a = pl.BlockSpec((tr, D), lambda i: (jnp.minimum(i, na - 1), 0))
    rows_b = pl.BlockSpec((tr, D), lambda i: (jnp.maximum(i - na, 0), 0))

    return pl.pallas_call(
        functools.partial(_post_kernel, final=final, steps_a=na),
        grid=(na + nb,),
        in_specs=[
            rows_a, rows_a, rows_b, rows_b,
            _const_spec(w_out.shape),
            _const_spec((1, D)),
            _const_spec(w_up.shape),
            _const_spec(w_down.shape),
            _const_spec((1, D)),
        ],
        out_specs=[rows_a, rows_b],
        out_shape=[jax.ShapeDtypeStruct((Ra, D), F32), jax.ShapeDtypeStruct((Rb, D), F32)],
        compiler_params=pltpu.CompilerParams(
            dimension_semantics=("arbitrary",), vmem_limit_bytes=VMEM_LIMIT),
        name="post",
    )(xa, mixa, xb, mixb, w_out, g2, w_up, w_down, gf)


def _in_proj_kernel(x_ref, g1_ref, w_ref, o_ref, h_s):
    @pl.when(pl.program_id(0) == 0)
    def _():
        h_s[...] = _rmsnorm(x_ref[...], g1_ref[...]).astype(BF16)

    o_ref[...] = jnp.dot(h_s[...], w_ref[...].astype(BF16), preferred_element_type=F32)


def _in_proj(x, g1, w_in, layer, ncol):
    R, D = x.shape
    C = w_in.shape[2]
    return pl.pallas_call(
        _in_proj_kernel,
        grid=(C // ncol,),
        in_specs=[
            _const_spec((R, D)),
            _const_spec((1, D)),
            pl.BlockSpec((None, D, ncol), lambda j: (layer, 0, j)),
        ],
        out_specs=pl.BlockSpec((R, ncol), lambda j: (0, j)),
        out_shape=jax.ShapeDtypeStruct((R, C), F32),
        scratch_shapes=[pltpu.VMEM((R, D), BF16)],
        compiler_params=pltpu.CompilerParams(
            dimension_semantics=("arbitrary",), vmem_limit_bytes=VMEM_LIMIT),
        name="sample_in_proj",
    )(x, g1, w_in)


def _sample_mix_kernel(proj_ref, cache_ref, st_ref, cw_ref, cvec_ref, gn_ref, cos_ref, sin_ref,
                       dec_ref, rdec_ref, kdec_ref, *rest, ts, nb, heads, hd, cdim, cwidth):
    mix_ref, ncache_ref, nst_ref = rest[-3:]
    rdim = heads * hd

    glu = [proj_ref[t, :, 0:cdim] * jax.nn.sigmoid(proj_ref[t, :, cdim:2 * cdim]) for t in range(ts)]

    def window(r):
        return cache_ref[r] if r < cwidth - 1 else glu[r - (cwidth - 1)]

    acc = [None] * ts
    for r in range(cwidth - 1 + ts):
        x = window(r)
        for t in range(max(0, r - cwidth + 1), min(ts, r + 1)):
            term = x * cw_ref[r - t:r - t + 1, :]
            acc[t] = term if acc[t] is None else acc[t] + term
    for t in range(ts):
        cn = _layernorm(acc[t] + cvec_ref[0:1, :]) * cvec_ref[1:2, :] + cvec_ref[2:3, :]
        mix_ref[t, :, 0:cdim] = _silu(cn).astype(BF16)
    for r in range(cwidth - 1):
        ncache_ref[r] = window(r + ts)

    c0 = 2 * cdim
    cos2 = cos_ref[...]
    sin2 = sin_ref[...]
    scale = hd ** -0.5
    for hh in range(heads):
        lo = hh * hd

        def per_seq(base):
            return jnp.stack([proj_ref[:, s, base + lo:base + lo + hd] for s in range(nb)], axis=0)

        q = _rotary(per_seq(c0), cos2, sin2)
        k = _rotary(per_seq(c0 + rdim), cos2, sin2) * scale
        v = per_seq(c0 + 2 * rdim)
        g = per_seq(c0 + 3 * rdim)
        st = st_ref[:, hh]
        scores = jnp.einsum('btd,bjd->btj', q, k, preferred_element_type=F32) * dec_ref[hh]
        inner = jnp.einsum('btj,bjv->btv', scores, v, preferred_element_type=F32)
        cross = jnp.einsum('btd,bdv->btv', q, st, preferred_element_type=F32) * rdec_ref[hh]
        upd = jnp.einsum('bjd,bjv->bdv', k * kdec_ref[hh], v, preferred_element_type=F32)
        nst_ref[:, hh] = st * math.exp(_log_gamma(hh) * ts) + upd
        on = _layernorm(inner + cross)
        out = (on * gn_ref[0:1, lo:lo + hd] * _silu(g)).astype(BF16)
        for s in range(nb):
            mix_ref[:, s, cdim + lo:cdim + lo + hd] = out[s]


def _sample_mix(proj, cache, state, layer, prev, conv_w, cvec, gn_g, cos2, sin2):
    ts, S, C = proj.shape
    depth, _, heads, hd, _ = state.shape
    cwidth, cdim = conv_w.shape
    rdim = heads * hd
    D = cdim + rdim
    nb = SAMPLE_SEQS
    assert S % nb == 0 and hd == LANES

    dec, rdec, kdec = (jnp.asarray(t, F32) for t in _decay_tables(ts, heads))

    conv_spec = pl.BlockSpec((None, cwidth - 1, nb, cdim), lambda i: (layer, 0, i, 0))
    st_spec = pl.BlockSpec((None, nb, heads, hd, hd), lambda i: (layer, i, 0, 0, 0))
    args = [proj, cache, state, conv_w, cvec, gn_g, cos2, sin2, dec, rdec, kdec]
    in_specs = [
        pl.BlockSpec((ts, nb, C), lambda i: (0, i, 0)),
        conv_spec,
        st_spec,
        _const_spec(conv_w.shape),
        _const_spec(cvec.shape),
        _const_spec((1, rdim)),
        _const_spec((ts, hd)),
        _const_spec((ts, hd)),
        _const_spec(dec.shape),
        _const_spec(rdec.shape),
        _const_spec(kdec.shape),
    ]
    aliases = {}
    if prev is not None:
        aliases = {len(args): 1, len(args) + 1: 2}
        args += list(prev)
        in_specs += [pl.BlockSpec(memory_space=pl.ANY)] * 2

    kern = functools.partial(_sample_mix_kernel, ts=ts, nb=nb, heads=heads, hd=hd, cdim=cdim, cwidth=cwidth)
    return pl.pallas_call(
        kern,
        grid=(S // nb,),
        in_specs=in_specs,
        out_specs=[pl.BlockSpec((ts, nb, D), lambda i: (0, i, 0)), conv_spec, st_spec],
        out_shape=[
            jax.ShapeDtypeStruct((ts, S, D), BF16),
            jax.ShapeDtypeStruct((depth, cwidth - 1, S, cdim), F32),
            jax.ShapeDtypeStruct((depth, S, heads, hd, hd), F32),
        ],
        input_output_aliases=aliases,
        compiler_params=pltpu.CompilerParams(
            dimension_semantics=("arbitrary",), vmem_limit_bytes=VMEM_LIMIT),
        name="sample_mix",
    )(*args)


def _rope_tables(pos0, T, hd):
    half = hd // 2
    pos = pos0 + np.arange(T, dtype=np.float64)
    inv = ROPE_BASE ** (-np.arange(half, dtype=np.float64) / half)
    ang = pos[:, None] * inv[None, :]
    cos, sin = np.cos(ang), np.sin(ang)
    return (jnp.asarray(np.concatenate([cos, cos], axis=-1), F32),
            jnp.asarray(np.concatenate([-sin, sin], axis=-1), F32))


def kernel(x_prompt, x_sample, cache_conv, state_ret, norm1_g, w_in, conv_w, conv_b, conv_ln_g,
           conv_ln_b, ret_gn_g, w_out, norm2_g, w_up, w_down, final_norm_g):
    depth = w_in.shape[0]
    B, T, D = x_prompt.shape
    S, ts, _ = x_sample.shape
    heads, hd = state_ret.shape[2], state_ret.shape[3]

    cvec = jnp.stack([conv_b, conv_ln_g, conv_ln_b], axis=1)
    gf = final_norm_g[None, :]
    cos_p, sin_p = _rope_tables(0.0, T, hd)
    cos_s, sin_s = _rope_tables(float(PAST_LEN), ts, hd)

    xp = x_prompt.reshape(B * T, D)
    xs = x_sample.transpose(1, 0, 2).reshape(ts * S, D)
    cache_t = cache_conv.transpose(0, 2, 1, 3)
    prompt_state, sample_state = None, None
    for l in range(depth):
        final = l == depth - 1
        g1, g2, gn = norm1_g[l][None, :], norm2_g[l][None, :], ret_gn_g[l][None, :]

        mix_p, *rest = _prompt_mix(xp.reshape(B, T, D), g1, w_in, l, prompt_state, conv_w[l], cvec[l], gn,
                                   cos_p, sin_p, heads, hd, (w_out, w_up, w_down))
        prompt_state, post_w = rest[:2], rest[2:]

        proj = _in_proj(xs, g1, w_in, l, 512).reshape(ts, S, -1)
        mix_s, *sample_state = _sample_mix(proj, cache_t, state_ret, l, sample_state, conv_w[l], cvec[l],
                                           gn, cos_s, sin_s)

        w_out_b, w_up_b, w_dn_b = post_w
        xp, xs = _post(xp, mix_p.reshape(B * T, D), xs, mix_s.reshape(ts * S, D), w_out_b, g2,
                       w_up_b, w_dn_b, gf, final)

    new_cache_t, new_state = sample_state
    return (xp.reshape(B, T, D), xs.reshape(ts, S, D).transpose(1, 0, 2), *prompt_state,
            new_cache_t.transpose(0, 2, 1, 3), new_state)
```

```python
import functools
import math

import numpy as np
import jax
import jax.numpy as jnp
from jax import lax
from jax.experimental import pallas as pl
from jax.experimental.pallas import tpu as pltpu

F32 = jnp.float32
BF16 = jnp.bfloat16

EPS = 1e-6
ROPE_BASE = 10000.0
PAST_LEN = 16384
LANES = 128
SUBLANES = 8
VMEM_LIMIT = 56 * 1024 * 1024

PROMPT_CHUNK = 512
RET_CHUNK = 512
CONV_ROWS = 128
GLU_LANE_BLOCKS = 2
POST_ROWS = 512
SAMPLE_SEQS = 32
IN_PROJ_COLS = 1024
CAST_COLS = 512


def _log_gamma(h):
    return math.log(1.0 - 2.0 ** (-5.0 - h))


def _rmsnorm(x, g):
    ms = jnp.mean(x * x, axis=-1, keepdims=True)
    return x * lax.rsqrt(ms + EPS) * g


def _layernorm(x):
    mu = jnp.mean(x, axis=-1, keepdims=True)
    xc = x - mu
    var = jnp.mean(xc * xc, axis=-1, keepdims=True)
    return xc * lax.rsqrt(var + EPS)


def _silu(x):
    return x * jax.nn.sigmoid(x)


def _rotary(t, cos2, sin2):
    half = t.shape[-1] // 2
    return t * cos2 + pltpu.roll(t, half, t.ndim - 1) * sin2


def _const_spec(shape):
    nd = len(shape)
    return pl.BlockSpec(shape, lambda *_: (0,) * nd, pipeline_mode=pl.Buffered(1))


def _decay_tables(n, heads):
    idx = np.arange(n, dtype=np.float64)
    diff = idx[:, None] - idx[None, :]
    lg = np.array([_log_gamma(h) for h in range(heads)])
    dec = np.where(diff[None] >= 0, np.exp(lg[:, None, None] * np.maximum(diff, 0.0)[None]), 0.0)
    rdec = np.exp(lg[:, None] * (idx + 1.0))[:, :, None]
    kdec = np.exp(lg[:, None] * (n - 1.0 - idx))[:, :, None]
    return dec, rdec, kdec


def _prompt_mix_kernel(x_ref, g1_ref, win_ref, cw_ref, cvec_ref, gn_ref, cos_ref, sin_ref,
                       dec_ref, rdec_ref, kdec_ref, wo_ref, wu_ref, wd_ref, *rest,
                       tq, sub, heads, hd, cdim, cwidth, hist):
    n_out, n_scratch = 6, 9
    mix_ref, nconv_ref, nret_ref, wob_ref, wub_ref, wdb_ref = rest[-(n_out + n_scratch):-n_scratch]
    w_s, glu_s, shift_s, conv_s, q_s, k_s, v_s, g_s, state_s = rest[-n_scratch:]
    b = pl.program_id(0)
    c = pl.program_id(1)
    last = pl.num_programs(1) - 1
    rdim = heads * hd

    @pl.when((b == 0) & (c == 0))
    def _():
        for lo in range(0, w_s.shape[1], CAST_COLS):
            w_s[:, lo:lo + CAST_COLS] = win_ref[:, lo:lo + CAST_COLS].astype(BF16)

    @pl.when(c == 0)
    def _():
        glu_s[0:hist, :] = jnp.zeros((hist, cdim), F32)
        shift_s[:, 0:hist, :] = jnp.zeros((SUBLANES - 1, hist, cdim), F32)
        shift_s[:, tq:tq + hist, :] = jnp.zeros((SUBLANES - 1, hist, cdim), F32)
        state_s[...] = jnp.zeros_like(state_s)

    @pl.when(c > 0)
    def _():
        glu_s[0:hist, :] = glu_s[tq:tq + hist, :]
        shift_s[:, 0:hist, :] = shift_s[:, tq:tq + hist, :]

    wob_ref[...] = wo_ref[...].astype(BF16)
    wub_ref[...] = wu_ref[...].astype(BF16)
    wdb_ref[...] = wd_ref[...].astype(BF16)

    h = _rmsnorm(x_ref[0], g1_ref[...]).astype(BF16)

    def proj(lo, n):
        return jnp.dot(h, w_s[:, lo:lo + n], preferred_element_type=F32)

    off = hist - (cwidth - 1)
    for lb in range(cdim // LANES):
        ls = slice(lb * LANES, (lb + 1) * LANES)
        if lb % GLU_LANE_BLOCKS == 0:
            n = GLU_LANE_BLOCKS * LANES
            cs = slice(lb * LANES, lb * LANES + n)
            glu = proj(lb * LANES, n) * jax.nn.sigmoid(proj(cdim + lb * LANES, n))
            glu_s[hist:hist + tq, cs] = glu
            for ph in range(1, SUBLANES):
                shift_s[ph - 1, hist - ph:hist - ph + tq, cs] = glu
        for r0 in range(0, tq, CONV_ROWS):
            acc = None
            for w in range(cwidth):
                ph = (off + w) % SUBLANES
                a = r0 + (off + w) // SUBLANES * SUBLANES
                xs = glu_s[a:a + CONV_ROWS, ls] if ph == 0 else shift_s[ph - 1, a:a + CONV_ROWS, ls]
                term = xs * cw_ref[w:w + 1, ls]
                acc = term if acc is None else acc + term
            conv_s[r0:r0 + CONV_ROWS, ls] = acc + cvec_ref[0:1, ls]
    cn = _layernorm(conv_s[...]) * cvec_ref[1:2, :] + cvec_ref[2:3, :]
    mix_ref[0, :, 0:cdim] = _silu(cn).astype(BF16)

    c0 = 2 * cdim
    qf = proj(c0, rdim)
    kf = proj(c0 + rdim, rdim)
    cos2 = cos_ref[...]
    sin2 = sin_ref[...]
    scale = hd ** -0.5
    for hh in range(heads):
        hs = slice(hh * hd, (hh + 1) * hd)
        q_s[:, hs] = _rotary(qf[:, hs], cos2, sin2).astype(BF16)
        k_s[:, hs] = _rotary(kf[:, hs], cos2, sin2) * scale
    v_s[...] = proj(c0 + 2 * rdim, rdim).astype(BF16)
    g_s[...] = _silu(proj(c0 + 3 * rdim, rdim))

    for hh in range(heads):
        hs = slice(hh * hd, (hh + 1) * hd)
        st = state_s[hh]
        for r0 in range(0, tq, sub):
            rs = slice(r0, r0 + sub)
            qh = q_s[rs, hs]
            kh = k_s[rs, hs]
            vh = v_s[rs, hs]
            scores = lax.dot_general(qh, kh.astype(BF16), (((1,), (1,)), ((), ())),
                                     preferred_element_type=F32) * dec_ref[hh]
            inner = jnp.dot(scores.astype(BF16), vh, preferred_element_type=F32)
            cross = jnp.dot(qh, st.astype(BF16), preferred_element_type=F32) * rdec_ref[hh]
            kd = (kh * kdec_ref[hh]).astype(BF16)
            upd = lax.dot_general(kd, vh, (((0,), (0,)), ((), ())), preferred_element_type=F32)
            st = st * math.exp(_log_gamma(hh) * sub) + upd
            on = _layernorm(inner + cross)
            mix_ref[0, rs, cdim + hh * hd:cdim + (hh + 1) * hd] = (
                on * gn_ref[0:1, hs] * g_s[rs, hs]).astype(BF16)
        state_s[hh] = st

    @pl.when(c == last)
    def _():
        nconv_ref[0] = glu_s[hist + tq - (cwidth - 1):hist + tq, :]
        nret_ref[0] = state_s[...]


def _prompt_mix(x, g1, w_in, layer, prev, conv_w, cvec, gn_g, cos2, sin2, heads, hd, post_weights):
    B, T, D = x.shape
    depth = w_in.shape[0]
    cwidth, cdim = conv_w.shape
    rdim = heads * hd
    tq = PROMPT_CHUNK
    sub = RET_CHUNK
    hist = -(-(cwidth - 1) // SUBLANES) * SUBLANES
    assert T % tq == 0 and tq >= hist and tq % CONV_ROWS == 0 and tq % sub == 0 and hd == LANES
    assert w_in.shape[2] % CAST_COLS == 0

    dec, rdec, kdec = _decay_tables(sub, heads)
    dec = jnp.asarray(dec, F32)
    rdec = jnp.asarray(np.broadcast_to(rdec, (heads, sub, hd)), F32)
    kdec = jnp.asarray(np.broadcast_to(kdec, (heads, sub, hd)), F32)

    chunks = T // tq
    steps = B * chunks
    for w in post_weights:
        assert w.shape[1] % (steps * 2 * SUBLANES) == 0
    slab_in = [pl.BlockSpec((None, w.shape[1] // steps, w.shape[2]), lambda b, c: (layer, b * chunks + c, 0))
               for w in post_weights]
    slab_out = [pl.BlockSpec((w.shape[1] // steps, w.shape[2]), lambda b, c: (b * chunks + c, 0))
                for w in post_weights]

    args = [x, g1, w_in, conv_w, cvec, gn_g, cos2, sin2, dec, rdec, kdec, *post_weights]
    in_specs = [
        pl.BlockSpec((1, tq, D), lambda b, c: (b, c, 0)),
        _const_spec((1, D)),
        pl.BlockSpec((None,) + w_in.shape[1:], lambda b, c: (layer, 0, 0), pipeline_mode=pl.Buffered(1)),
        _const_spec(conv_w.shape),
        _const_spec(cvec.shape),
        _const_spec((1, rdim)),
        pl.BlockSpec((tq, hd), lambda b, c: (c, 0)),
        pl.BlockSpec((tq, hd), lambda b, c: (c, 0)),
        _const_spec(dec.shape),
        _const_spec(rdec.shape),
        _const_spec(kdec.shape),
        *slab_in,
    ]
    aliases = {}
    if prev is not None:
        aliases = {len(args): 1, len(args) + 1: 2}
        args += list(prev)
        in_specs += [pl.BlockSpec(memory_space=pl.ANY)] * 2

    kern = functools.partial(_prompt_mix_kernel, tq=tq, sub=sub, heads=heads, hd=hd, cdim=cdim,
                             cwidth=cwidth, hist=hist)
    return pl.pallas_call(
        kern,
        grid=(B, chunks),
        in_specs=in_specs,
        out_specs=[
            pl.BlockSpec((1, tq, D), lambda b, c: (b, c, 0)),
            pl.BlockSpec((None, 1, cwidth - 1, cdim), lambda b, c: (layer, b, 0, 0)),
            pl.BlockSpec((None, 1, heads, hd, hd), lambda b, c: (layer, b, 0, 0, 0)),
            *slab_out,
        ],
        out_shape=[
            jax.ShapeDtypeStruct((B, T, D), BF16),
            jax.ShapeDtypeStruct((depth, B, cwidth - 1, cdim), F32),
            jax.ShapeDtypeStruct((depth, B, heads, hd, hd), F32),
            *[jax.ShapeDtypeStruct(w.shape[1:], BF16) for w in post_weights],
        ],
        input_output_aliases=aliases,
        scratch_shapes=[
            pltpu.VMEM(w_in.shape[1:], BF16),
            pltpu.VMEM((hist + tq, cdim), F32),
            pltpu.VMEM((SUBLANES - 1, hist + tq, cdim), F32),
            pltpu.VMEM((tq, cdim), F32),
            pltpu.VMEM((tq, rdim), BF16),
            pltpu.VMEM((tq, rdim), F32),
            pltpu.VMEM((tq, rdim), BF16),
            pltpu.VMEM((tq, rdim), F32),
            pltpu.VMEM((heads, hd, hd), F32),
        ],
        compiler_params=pltpu.CompilerParams(
            dimension_semantics=("arbitrary", "arbitrary"), vmem_limit_bytes=VMEM_LIMIT),
        name="prompt_mix",
    )(*args)


def _post_kernel(xa_ref, mixa_ref, xb_ref, mixb_ref, wout_ref, g2_ref, wup_ref, wdn_ref, gf_ref,
                 oa_ref, ob_ref, *, final, steps_a):
    def tile(x_ref, mix_ref, o_ref):
        y = x_ref[...] + jnp.dot(mix_ref[...], wout_ref[...], preferred_element_type=F32)
        h2 = _rmsnorm(y, g2_ref[...]).astype(BF16)
        hf = jnp.dot(h2, wup_ref[...], preferred_element_type=F32)
        act = jnp.square(jnp.maximum(hf, 0.0)).astype(BF16)
        out = y + jnp.dot(act, wdn_ref[...], preferred_element_type=F32)
        if final:
            out = _rmsnorm(out, gf_ref[...])
        o_ref[...] = out

    i = pl.program_id(0)

    @pl.when(i < steps_a)
    def _():
        tile(xa_ref, mixa_ref, oa_ref)

    @pl.when(i >= steps_a)
    def _():
        tile(xb_ref, mixb_ref, ob_ref)


def _post(xa, mixa, xb, mixb, w_out, g2, w_up, w_down, gf, final):
    Ra, D = xa.shape
    Rb = xb.shape[0]
    tr = POST_ROWS
    assert Ra % tr == 0 and Rb % tr == 0
    na, nb = Ra // tr, Rb // tr
    rows_a = pl.BlockSpec((tr, D), lambda i: (jnp.minimum(i, na - 1), 0))
    rows_b = pl.BlockSpec((tr, D), lambda i: (jnp.maximum(i - na, 0), 0))

    return pl.pallas_call(
        functools.partial(_post_kernel, final=final, steps_a=na),
        grid=(na + nb,),
        in_specs=[
            rows_a, rows_a, rows_b, rows_b,
            _const_spec(w_out.shape),
            _const_spec((1, D)),
            _const_spec(w_up.shape),
            _const_spec(w_down.shape),
            _const_spec((1, D)),
        ],
        out_specs=[rows_a, rows_b],
        out_shape=[jax.ShapeDtypeStruct((Ra, D), F32), jax.ShapeDtypeStruct((Rb, D), F32)],
        compiler_params=pltpu.CompilerParams(
            dimension_semantics=("arbitrary",), vmem_limit_bytes=VMEM_LIMIT),
        name="post",
    )(xa, mixa, xb, mixb, w_out, g2, w_up, w_down, gf)


def _in_proj_kernel(x_ref, g1_ref, w_ref, o_ref, h_s):
    @pl.when(pl.program_id(0) == 0)
    def _():
        h_s[...] = _rmsnorm(x_ref[...], g1_ref[...]).astype(BF16)

    o_ref[...] = jnp.dot(h_s[...], w_ref[...].astype(BF16), preferred_element_type=F32)


def _in_proj(x, g1, w_in, layer, ncol):
    R, D = x.shape
    C = w_in.shape[2]
    return pl.pallas_call(
        _in_proj_kernel,
        grid=(C // ncol,),
        in_specs=[
            _const_spec((R, D)),
            _const_spec((1, D)),
            pl.BlockSpec((None, D, ncol), lambda j: (layer, 0, j)),
        ],
        out_specs=pl.BlockSpec((R, ncol), lambda j: (0, j)),
        out_shape=jax.ShapeDtypeStruct((R, C), F32),
        scratch_shapes=[pltpu.VMEM((R, D), BF16)],
        compiler_params=pltpu.CompilerParams(
            dimension_semantics=("arbitrary",), vmem_limit_bytes=VMEM_LIMIT),
        name="sample_in_proj",
    )(x, g1, w_in)


def _sample_mix_kernel(proj_ref, cache_ref, st_ref, cw_ref, cvec_ref, gn_ref, cos_ref, sin_ref,
                       dec_ref, rdec_ref, kdec_ref, *rest, ts, nb, heads, hd, cdim, cwidth):
    mix_ref, ncache_ref, nst_ref = rest[-3:]
    rdim = heads * hd

    glu = [proj_ref[t, :, 0:cdim] * jax.nn.sigmoid(proj_ref[t, :, cdim:2 * cdim]) for t in range(ts)]

    def window(r):
        return cache_ref[r] if r < cwidth - 1 else glu[r - (cwidth - 1)]

    acc = [None] * ts
    for r in range(cwidth - 1 + ts):
        x = window(r)
        for t in range(max(0, r - cwidth + 1), min(ts, r + 1)):
            term = x * cw_ref[r - t:r - t + 1, :]
            acc[t] = term if acc[t] is None else acc[t] + term
    for t in range(ts):
        cn = _layernorm(acc[t] + cvec_ref[0:1, :]) * cvec_ref[1:2, :] + cvec_ref[2:3, :]
        mix_ref[t, :, 0:cdim] = _silu(cn).astype(BF16)
    for r in range(cwidth - 1):
        ncache_ref[r] = window(r + ts)

    c0 = 2 * cdim
    cos2 = cos_ref[...]
    sin2 = sin_ref[...]
    scale = hd ** -0.5
    for hh in range(heads):
        lo = hh * hd

        def per_seq(base):
            return jnp.stack([proj_ref[:, s, base + lo:base + lo + hd] for s in range(nb)], axis=0)

        q = _rotary(per_seq(c0), cos2, sin2)
        k = _rotary(per_seq(c0 + rdim), cos2, sin2) * scale
        v = per_seq(c0 + 2 * rdim)
        g = per_seq(c0 + 3 * rdim)
        st = st_ref[:, hh]
        scores = jnp.einsum('btd,bjd->btj', q, k, preferred_element_type=F32) * dec_ref[hh]
        inner = jnp.einsum('btj,bjv->btv', scores, v, preferred_element_type=F32)
        cross = jnp.einsum('btd,bdv->btv', q, st, preferred_element_type=F32) * rdec_ref[hh]
        upd = jnp.einsum('bjd,bjv->bdv', k * kdec_ref[hh], v, preferred_element_type=F32)
        nst_ref[:, hh] = st * math.exp(_log_gamma(hh) * ts) + upd
        on = _layernorm(inner + cross)
        out = (on * gn_ref[0:1, lo:lo + hd] * _silu(g)).astype(BF16)
        for s in range(nb):
            mix_ref[:, s, cdim + lo:cdim + lo + hd] = out[s]


def _sample_mix(proj, cache, state, layer, prev, conv_w, cvec, gn_g, cos2, sin2):
    ts, S, C = proj.shape
    depth, _, heads, hd, _ = state.shape
    cwidth, cdim = conv_w.shape
    rdim = heads * hd
    D = cdim + rdim
    nb = SAMPLE_SEQS
    assert S % nb == 0 and hd == LANES

    dec, rdec, kdec = (jnp.asarray(t, F32) for t in _decay_tables(ts, heads))

    conv_spec = pl.BlockSpec((None, cwidth - 1, nb, cdim), lambda i: (layer, 0, i, 0))
    st_spec = pl.BlockSpec((None, nb, heads, hd, hd), lambda i: (layer, i, 0, 0, 0))
    args = [proj, cache, state, conv_w, cvec, gn_g, cos2, sin2, dec, rdec, kdec]
    in_specs = [
        pl.BlockSpec((ts, nb, C), lambda i: (0, i, 0)),
        conv_spec,
        st_spec,
        _const_spec(conv_w.shape),
        _const_spec(cvec.shape),
        _const_spec((1, rdim)),
        _const_spec((ts, hd)),
        _const_spec((ts, hd)),
        _const_spec(dec.shape),
        _const_spec(rdec.shape),
        _const_spec(kdec.shape),
    ]
    aliases = {}
    if prev is not None:
        aliases = {len(args): 1, len(args) + 1: 2}
        args += list(prev)
        in_specs += [pl.BlockSpec(memory_space=pl.ANY)] * 2

    kern = functools.partial(_sample_mix_kernel, ts=ts, nb=nb, heads=heads, hd=hd, cdim=cdim, cwidth=cwidth)
    return pl.pallas_call(
        kern,
        grid=(S // nb,),
        in_specs=in_specs,
        out_specs=[pl.BlockSpec((ts, nb, D), lambda i: (0, i, 0)), conv_spec, st_spec],
        out_shape=[
            jax.ShapeDtypeStruct((ts, S, D), BF16),
            jax.ShapeDtypeStruct((depth, cwidth - 1, S, cdim), F32),
            jax.ShapeDtypeStruct((depth, S, heads, hd, hd), F32),
        ],
        input_output_aliases=aliases,
        compiler_params=pltpu.CompilerParams(
            dimension_semantics=("arbitrary",), vmem_limit_bytes=VMEM_LIMIT),
        name="sample_mix",
    )(*args)


def _rope_tables(pos0, T, hd):
    half = hd // 2
    pos = pos0 + np.arange(T, dtype=np.float64)
    inv = ROPE_BASE ** (-np.arange(half, dtype=np.float64) / half)
    ang = pos[:, None] * inv[None, :]
    cos, sin = np.cos(ang), np.sin(ang)
    return (jnp.asarray(np.concatenate([cos, cos], axis=-1), F32),
            jnp.asarray(np.concatenate([-sin, sin], axis=-1), F32))


def kernel(x_prompt, x_sample, cache_conv, state_ret, norm1_g, w_in, conv_w, conv_b, conv_ln_g,
           conv_ln_b, ret_gn_g, w_out, norm2_g, w_up, w_down, final_norm_g):
    depth = w_in.shape[0]
    B, T, D = x_prompt.shape
    S, ts, _ = x_sample.shape
    heads, hd = state_ret.shape[2], state_ret.shape[3]

    cvec = jnp.stack([conv_b, conv_ln_g, conv_ln_b], axis=1)
    gf = final_norm_g[None, :]
    cos_p, sin_p = _rope_tables(0.0, T, hd)
    cos_s, sin_s = _rope_tables(float(PAST_LEN), ts, hd)

    xp = x_prompt.reshape(B * T, D)
    xs = x_sample.transpose(1, 0, 2).reshape(ts * S, D)
    cache_t = cache_conv.transpose(0, 2, 1, 3)
    prompt_state, sample_state = None, None
    for l in range(depth):
        final = l == depth - 1
        g1, g2, gn = norm1_g[l][None, :], norm2_g[l][None, :], ret_gn_g[l][None, :]

        mix_p, *rest = _prompt_mix(xp.reshape(B, T, D), g1, w_in, l, prompt_state, conv_w[l], cvec[l], gn,
                                   cos_p, sin_p, heads, hd, (w_out, w_up, w_down))
        prompt_state, post_w = rest[:2], rest[2:]

        proj = _in_proj(xs, g1, w_in, l, IN_PROJ_COLS).reshape(ts, S, -1)
        mix_s, *sample_state = _sample_mix(proj, cache_t, state_ret, l, sample_state, conv_w[l], cvec[l],
                                           gn, cos_s, sin_s)

        w_out_b, w_up_b, w_dn_b = post_w
        xp, xs = _post(xp, mix_p.reshape(B * T, D), xs, mix_s.reshape(ts * S, D), w_out_b, g2,
                       w_up_b, w_dn_b, gf, final)

    new_cache_t, new_state = sample_state
    return (xp.reshape(B, T, D), xs.reshape(ts, S, D).transpose(1, 0, 2), *prompt_state,
            new_cache_t.transpose(0, 2, 1, 3), new_state)
```

```python
import functools
import math

import numpy as np
import jax
import jax.numpy as jnp
from jax import lax
from jax.experimental import pallas as pl
from jax.experimental.pallas import tpu as pltpu

F32 = jnp.float32
BF16 = jnp.bfloat16

EPS = 1e-6
ROPE_BASE = 10000.0
PAST_LEN = 16384
LANES = 128
SUBLANES = 8
VMEM_LIMIT = 56 * 1024 * 1024

PROMPT_CHUNK = 512
RET_CHUNK = 512
CAUSAL_ROWS = 256
CONV_ROWS = 128
GLU_LANE_BLOCKS = 2
POST_ROWS = 512
SAMPLE_SEQS = 32
IN_PROJ_COLS = 1024
CAST_COLS = 512


def _log_gamma(h):
    return math.log(1.0 - 2.0 ** (-5.0 - h))


def _rmsnorm(x, g):
    ms = jnp.mean(x * x, axis=-1, keepdims=True)
    return x * lax.rsqrt(ms + EPS) * g


def _layernorm(x):
    mu = jnp.mean(x, axis=-1, keepdims=True)
    xc = x - mu
    var = jnp.mean(xc * xc, axis=-1, keepdims=True)
    return xc * lax.rsqrt(var + EPS)


def _silu(x):
    return x * jax.nn.sigmoid(x)


def _rotary(t, cos2, sin2):
    half = t.shape[-1] // 2
    return t * cos2 + pltpu.roll(t, half, t.ndim - 1) * sin2


def _const_spec(shape):
    nd = len(shape)
    return pl.BlockSpec(shape, lambda *_: (0,) * nd, pipeline_mode=pl.Buffered(1))


def _decay_tables(n, heads):
    idx = np.arange(n, dtype=np.float64)
    diff = idx[:, None] - idx[None, :]
    lg = np.array([_log_gamma(h) for h in range(heads)])
    dec = np.where(diff[None] >= 0, np.exp(lg[:, None, None] * np.maximum(diff, 0.0)[None]), 0.0)
    rdec = np.exp(lg[:, None] * (idx + 1.0))[:, :, None]
    kdec = np.exp(lg[:, None] * (n - 1.0 - idx))[:, :, None]
    return dec, rdec, kdec


def _prompt_mix_kernel(x_ref, g1_ref, win_ref, cw_ref, cvec_ref, gn_ref, cos_ref, sin_ref,
                       dec_ref, rdec_ref, kdec_ref, wo_ref, wu_ref, wd_ref, *rest,
                       tq, sub, heads, hd, cdim, cwidth, hist):
    n_out, n_scratch = 6, 9
    mix_ref, nconv_ref, nret_ref, wob_ref, wub_ref, wdb_ref = rest[-(n_out + n_scratch):-n_scratch]
    w_s, glu_s, shift_s, conv_s, q_s, k_s, v_s, g_s, state_s = rest[-n_scratch:]
    b = pl.program_id(0)
    c = pl.program_id(1)
    last = pl.num_programs(1) - 1
    rdim = heads * hd

    @pl.when((b == 0) & (c == 0))
    def _():
        for lo in range(0, w_s.shape[1], CAST_COLS):
            w_s[:, lo:lo + CAST_COLS] = win_ref[:, lo:lo + CAST_COLS].astype(BF16)

    @pl.when(c == 0)
    def _():
        glu_s[0:hist, :] = jnp.zeros((hist, cdim), F32)
        shift_s[:, 0:hist, :] = jnp.zeros((SUBLANES - 1, hist, cdim), F32)
        shift_s[:, tq:tq + hist, :] = jnp.zeros((SUBLANES - 1, hist, cdim), F32)
        state_s[...] = jnp.zeros_like(state_s)

    @pl.when(c > 0)
    def _():
        glu_s[0:hist, :] = glu_s[tq:tq + hist, :]
        shift_s[:, 0:hist, :] = shift_s[:, tq:tq + hist, :]

    wob_ref[...] = wo_ref[...].astype(BF16)
    wub_ref[...] = wu_ref[...].astype(BF16)
    wdb_ref[...] = wd_ref[...].astype(BF16)

    h = _rmsnorm(x_ref[0], g1_ref[...]).astype(BF16)

    def proj(lo, n):
        return jnp.dot(h, w_s[:, lo:lo + n], preferred_element_type=F32)

    off = hist - (cwidth - 1)
    for lb in range(cdim // LANES):
        ls = slice(lb * LANES, (lb + 1) * LANES)
        if lb % GLU_LANE_BLOCKS == 0:
            n = GLU_LANE_BLOCKS * LANES
            cs = slice(lb * LANES, lb * LANES + n)
            glu = proj(lb * LANES, n) * jax.nn.sigmoid(proj(cdim + lb * LANES, n))
            glu_s[hist:hist + tq, cs] = glu
            for ph in range(1, SUBLANES):
                shift_s[ph - 1, hist - ph:hist - ph + tq, cs] = glu
        for r0 in range(0, tq, CONV_ROWS):
            acc = None
            for w in range(cwidth):
                ph = (off + w) % SUBLANES
                a = r0 + (off + w) // SUBLANES * SUBLANES
                xs = glu_s[a:a + CONV_ROWS, ls] if ph == 0 else shift_s[ph - 1, a:a + CONV_ROWS, ls]
                term = xs * cw_ref[w:w + 1, ls]
                acc = term if acc is None else acc + term
            conv_s[r0:r0 + CONV_ROWS, ls] = acc + cvec_ref[0:1, ls]
    for r0 in range(0, tq, CONV_ROWS):
        cn = _layernorm(conv_s[r0:r0 + CONV_ROWS, :]) * cvec_ref[1:2, :] + cvec_ref[2:3, :]
        mix_ref[0, r0:r0 + CONV_ROWS, 0:cdim] = _silu(cn).astype(BF16)

    c0 = 2 * cdim
    qf = proj(c0, rdim)
    kf = proj(c0 + rdim, rdim)
    cos2 = cos_ref[...]
    sin2 = sin_ref[...]
    scale = hd ** -0.5
    for hh in range(heads):
        hs = slice(hh * hd, (hh + 1) * hd)
        q_s[:, hs] = _rotary(qf[:, hs], cos2, sin2).astype(BF16)
        k_s[:, hs] = _rotary(kf[:, hs], cos2, sin2) * scale
    v_s[...] = proj(c0 + 2 * rdim, rdim).astype(BF16)
    g_s[...] = _silu(proj(c0 + 3 * rdim, rdim))

    for hh in range(heads):
        hs = slice(hh * hd, (hh + 1) * hd)
        st = state_s[hh]
        for r0 in range(0, tq, sub):
            rs = slice(r0, r0 + sub)
            qh = q_s[rs, hs]
            kh = k_s[rs, hs]
            vh = v_s[rs, hs]
            kb = kh.astype(BF16)
            parts = []
            for a in range(0, sub, CAUSAL_ROWS):
                e = a + CAUSAL_ROWS
                sc = lax.dot_general(qh[a:e], kb[0:e], (((1,), (1,)), ((), ())),
                                     preferred_element_type=F32) * dec_ref[hh, a:e, 0:e]
                parts.append(jnp.dot(sc.astype(BF16), vh[0:e], preferred_element_type=F32))
            inner = jnp.concatenate(parts, axis=0)
            cross = jnp.dot(qh, st.astype(BF16), preferred_element_type=F32) * rdec_ref[hh]
            kd = (kh * kdec_ref[hh]).astype(BF16)
            upd = lax.dot_general(kd, vh, (((0,), (0,)), ((), ())), preferred_element_type=F32)
            st = st * math.exp(_log_gamma(hh) * sub) + upd
            on = _layernorm(inner + cross)
            mix_ref[0, rs, cdim + hh * hd:cdim + (hh + 1) * hd] = (
                on * gn_ref[0:1, hs] * g_s[rs, hs]).astype(BF16)
        state_s[hh] = st

    @pl.when(c == last)
    def _():
        nconv_ref[0] = glu_s[hist + tq - (cwidth - 1):hist + tq, :]
        nret_ref[0] = state_s[...]


def _prompt_mix(x, g1, w_in, layer, prev, conv_w, cvec, gn_g, cos2, sin2, heads, hd, post_weights):
    B, T, D = x.shape
    depth = w_in.shape[0]
    cwidth, cdim = conv_w.shape
    rdim = heads * hd
    tq = PROMPT_CHUNK
    sub = RET_CHUNK
    hist = -(-(cwidth - 1) // SUBLANES) * SUBLANES
    assert T % tq == 0 and tq >= hist and tq % CONV_ROWS == 0 and tq % sub == 0 and hd == LANES
    assert w_in.shape[2] % CAST_COLS == 0

    dec, rdec, kdec = _decay_tables(sub, heads)
    dec = jnp.asarray(dec, F32)
    rdec = jnp.asarray(np.broadcast_to(rdec, (heads, sub, hd)), F32)
    kdec = jnp.asarray(np.broadcast_to(kdec, (heads, sub, hd)), F32)

    chunks = T // tq
    steps = B * chunks
    for w in post_weights:
        assert w.shape[1] % (steps * 2 * SUBLANES) == 0
    slab_in = [pl.BlockSpec((None, w.shape[1] // steps, w.shape[2]), lambda b, c: (layer, b * chunks + c, 0))
               for w in post_weights]
    slab_out = [pl.BlockSpec((w.shape[1] // steps, w.shape[2]), lambda b, c: (b * chunks + c, 0))
                for w in post_weights]

    args = [x, g1, w_in, conv_w, cvec, gn_g, cos2, sin2, dec, rdec, kdec, *post_weights]
    in_specs = [
        pl.BlockSpec((1, tq, D), lambda b, c: (b, c, 0)),
        _const_spec((1, D)),
        pl.BlockSpec((None,) + w_in.shape[1:], lambda b, c: (layer, 0, 0), pipeline_mode=pl.Buffered(1)),
        _const_spec(conv_w.shape),
        _const_spec(cvec.shape),
        _const_spec((1, rdim)),
        pl.BlockSpec((tq, hd), lambda b, c: (c, 0)),
        pl.BlockSpec((tq, hd), lambda b, c: (c, 0)),
        _const_spec(dec.shape),
        _const_spec(rdec.shape),
        _const_spec(kdec.shape),
        *slab_in,
    ]
    aliases = {}
    if prev is not None:
        aliases = {len(args): 1, len(args) + 1: 2}
        args += list(prev)
        in_specs += [pl.BlockSpec(memory_space=pl.ANY)] * 2

    kern = functools.partial(_prompt_mix_kernel, tq=tq, sub=sub, heads=heads, hd=hd, cdim=cdim,
                             cwidth=cwidth, hist=hist)
    return pl.pallas_call(
        kern,
        grid=(B, chunks),
        in_specs=in_specs,
        out_specs=[
            pl.BlockSpec((1, tq, D), lambda b, c: (b, c, 0)),
            pl.BlockSpec((None, 1, cwidth - 1, cdim), lambda b, c: (layer, b, 0, 0)),
            pl.BlockSpec((None, 1, heads, hd, hd), lambda b, c: (layer, b, 0, 0, 0)),
            *slab_out,
        ],
        out_shape=[
            jax.ShapeDtypeStruct((B, T, D), BF16),
            jax.ShapeDtypeStruct((depth, B, cwidth - 1, cdim), F32),
            jax.ShapeDtypeStruct((depth, B, heads, hd, hd), F32),
            *[jax.ShapeDtypeStruct(w.shape[1:], BF16) for w in post_weights],
        ],
        input_output_aliases=aliases,
        scratch_shapes=[
            pltpu.VMEM(w_in.shape[1:], BF16),
            pltpu.VMEM((hist + tq, cdim), F32),
            pltpu.VMEM((SUBLANES - 1, hist + tq, cdim), F32),
            pltpu.VMEM((tq, cdim), F32),
            pltpu.VMEM((tq, rdim), BF16),
            pltpu.VMEM((tq, rdim), F32),
            pltpu.VMEM((tq, rdim), BF16),
            pltpu.VMEM((tq, rdim), F32),
            pltpu.VMEM((heads, hd, hd), F32),
        ],
        compiler_params=pltpu.CompilerParams(
            dimension_semantics=("arbitrary", "arbitrary"), vmem_limit_bytes=VMEM_LIMIT),
        name="prompt_mix",
    )(*args)


def _post_kernel(xa_ref, mixa_ref, xb_ref, mixb_ref, wout_ref, g2_ref, wup_ref, wdn_ref, gf_ref,
                 oa_ref, ob_ref, *, final, steps_a):
    def tile(x_ref, mix_ref, o_ref):
        y = x_ref[...] + jnp.dot(mix_ref[...], wout_ref[...], preferred_element_type=F32)
        h2 = _rmsnorm(y, g2_ref[...]).astype(BF16)
        hf = jnp.dot(h2, wup_ref[...], preferred_element_type=F32)
        act = jnp.square(jnp.maximum(hf, 0.0)).astype(BF16)
        out = y + jnp.dot(act, wdn_ref[...], preferred_element_type=F32)
        if final:
            out = _rmsnorm(out, gf_ref[...])
        o_ref[...] = out

    i = pl.program_id(0)

    @pl.when(i < steps_a)
    def _():
        tile(xa_ref, mixa_ref, oa_ref)

    @pl.when(i >= steps_a)
    def _():
        tile(xb_ref, mixb_ref, ob_ref)


def _post(xa, mixa, xb, mixb, w_out, g2, w_up, w_down, gf, final):
    Ra, D = xa.shape
    Rb = xb.shape[0]
    tr = POST_ROWS
    assert Ra % tr == 0 and Rb % tr == 0
    na, nb = Ra // tr, Rb // tr
    rows_a = pl.BlockSpec((tr, D), lambda i: (jnp.minimum(i, na - 1), 0))
    rows_b = pl.BlockSpec((tr, D), lambda i: (jnp.maximum(i - na, 0), 0))

    return pl.pallas_call(
        functools.partial(_post_kernel, final=final, steps_a=na),
        grid=(na + nb,),
        in_specs=[
            rows_a, rows_a, rows_b, rows_b,
            _const_spec(w_out.shape),
            _const_spec((1, D)),
            _const_spec(w_up.shape),
            _const_spec(w_down.shape),
            _const_spec((1, D)),
        ],
        out_specs=[rows_a, rows_b],
        out_shape=[jax.ShapeDtypeStruct((Ra, D), F32), jax.ShapeDtypeStruct((Rb, D), F32)],
        compiler_params=pltpu.CompilerParams(
            dimension_semantics=("arbitrary",), vmem_limit_bytes=VMEM_LIMIT),
        name="post",
    )(xa, mixa, xb, mixb, w_out, g2, w_up, w_down, gf)


def _in_proj_kernel(x_ref, g1_ref, w_ref, o_ref, h_s):
    @pl.when(pl.program_id(0) == 0)
    def _():
        h_s[...] = _rmsnorm(x_ref[...], g1_ref[...]).astype(BF16)

    o_ref[...] = jnp.dot(h_s[...], w_ref[...].astype(BF16), preferred_element_type=F32)


def _in_proj(x, g1, w_in, layer, ncol):
    R, D = x.shape
    C = w_in.shape[2]
    return pl.pallas_call(
        _in_proj_kernel,
        grid=(C // ncol,),
        in_specs=[
            _const_spec((R, D)),
            _const_spec((1, D)),
            pl.BlockSpec((None, D, ncol), lambda j: (layer, 0, j)),
        ],
        out_specs=pl.BlockSpec((R, ncol), lambda j: (0, j)),
        out_shape=jax.ShapeDtypeStruct((R, C), F32),
        scratch_shapes=[pltpu.VMEM((R, D), BF16)],
        compiler_params=pltpu.CompilerParams(
            dimension_semantics=("arbitrary",), vmem_limit_bytes=VMEM_LIMIT),
        name="sample_in_proj",
    )(x, g1, w_in)


def _sample_mix_kernel(proj_ref, cache_ref, st_ref, cw_ref, cvec_ref, gn_ref, cos_ref, sin_ref,
                       dec_ref, rdec_ref, kdec_ref, *rest, ts, nb, heads, hd, cdim, cwidth):
    mix_ref, ncache_ref, nst_ref = rest[-3:]
    rdim = heads * hd

    glu = [proj_ref[t, :, 0:cdim] * jax.nn.sigmoid(proj_ref[t, :, cdim:2 * cdim]) for t in range(ts)]

    def window(r):
        return cache_ref[r] if r < cwidth - 1 else glu[r - (cwidth - 1)]

    acc = [None] * ts
    for r in range(cwidth - 1 + ts):
        x = window(r)
        for t in range(max(0, r - cwidth + 1), min(ts, r + 1)):
            term = x * cw_ref[r - t:r - t + 1, :]
            acc[t] = term if acc[t] is None else acc[t] + term
    for t in range(ts):
        cn = _layernorm(acc[t] + cvec_ref[0:1, :]) * cvec_ref[1:2, :] + cvec_ref[2:3, :]
        mix_ref[t, :, 0:cdim] = _silu(cn).astype(BF16)
    for r in range(cwidth - 1):
        ncache_ref[r] = window(r + ts)

    c0 = 2 * cdim
    cos2 = cos_ref[...]
    sin2 = sin_ref[...]
    scale = hd ** -0.5
    for hh in range(heads):
        lo = hh * hd

        def per_seq(base):
            return jnp.stack([proj_ref[:, s, base + lo:base + lo + hd] for s in range(nb)], axis=0)

        q = _rotary(per_seq(c0), cos2, sin2)
        k = _rotary(per_seq(c0 + rdim), cos2, sin2) * scale
        v = per_seq(c0 + 2 * rdim)
        g = per_seq(c0 + 3 * rdim)
        st = st_ref[:, hh]
        scores = jnp.einsum('btd,bjd->btj', q, k, preferred_element_type=F32) * dec_ref[hh]
        inner = jnp.einsum('btj,bjv->btv', scores, v, preferred_element_type=F32)
        cross = jnp.einsum('btd,bdv->btv', q, st, preferred_element_type=F32) * rdec_ref[hh]
        upd = jnp.einsum('bjd,bjv->bdv', k * kdec_ref[hh], v, preferred_element_type=F32)
        nst_ref[:, hh] = st * math.exp(_log_gamma(hh) * ts) + upd
        on = _layernorm(inner + cross)
        out = (on * gn_ref[0:1, lo:lo + hd] * _silu(g)).astype(BF16)
        for s in range(nb):
            mix_ref[:, s, cdim + lo:cdim + lo + hd] = out[s]


def _sample_mix(proj, cache, state, layer, prev, conv_w, cvec, gn_g, cos2, sin2):
    ts, S, C = proj.shape
    depth, _, heads, hd, _ = state.shape
    cwidth, cdim = conv_w.shape
    rdim = heads * hd
    D = cdim + rdim
    nb = SAMPLE_SEQS
    assert S % nb == 0 and hd == LANES

    dec, rdec, kdec = (jnp.asarray(t, F32) for t in _decay_tables(ts, heads))

    conv_spec = pl.BlockSpec((None, cwidth - 1, nb, cdim), lambda i: (layer, 0, i, 0))
    st_spec = pl.BlockSpec((None, nb, heads, hd, hd), lambda i: (layer, i, 0, 0, 0))
    args = [proj, cache, state, conv_w, cvec, gn_g, cos2, sin2, dec, rdec, kdec]
    in_specs = [
        pl.BlockSpec((ts, nb, C), lambda i: (0, i, 0)),
        conv_spec,
        st_spec,
        _const_spec(conv_w.shape),
        _const_spec(cvec.shape),
        _const_spec((1, rdim)),
        _const_spec((ts, hd)),
        _const_spec((ts, hd)),
        _const_spec(dec.shape),
        _const_spec(rdec.shape),
        _const_spec(kdec.shape),
    ]
    aliases = {}
    if prev is not None:
        aliases = {len(args): 1, len(args) + 1: 2}
        args += list(prev)
        in_specs += [pl.BlockSpec(memory_space=pl.ANY)] * 2

    kern = functools.partial(_sample_mix_kernel, ts=ts, nb=nb, heads=heads, hd=hd, cdim=cdim, cwidth=cwidth)
    return pl.pallas_call(
        kern,
        grid=(S // nb,),
        in_specs=in_specs,
        out_specs=[pl.BlockSpec((ts, nb, D), lambda i: (0, i, 0)), conv_spec, st_spec],
        out_shape=[
            jax.ShapeDtypeStruct((ts, S, D), BF16),
            jax.ShapeDtypeStruct((depth, cwidth - 1, S, cdim), F32),
            jax.ShapeDtypeStruct((depth, S, heads, hd, hd), F32),
        ],
        input_output_aliases=aliases,
        compiler_params=pltpu.CompilerParams(
            dimension_semantics=("arbitrary",), vmem_limit_bytes=VMEM_LIMIT),
        name="sample_mix",
    )(*args)


def _rope_tables(pos0, T, hd):
    half = hd // 2
    pos = pos0 + np.arange(T, dtype=np.float64)
    inv = ROPE_BASE ** (-np.arange(half, dtype=np.float64) / half)
    ang = pos[:, None] * inv[None, :]
    cos, sin = np.cos(ang), np.sin(ang)
    return (jnp.asarray(np.concatenate([cos, cos], axis=-1), F32),
            jnp.asarray(np.concatenate([-sin, sin], axis=-1), F32))


def kernel(x_prompt, x_sample, cache_conv, state_ret, norm1_g, w_in, conv_w, conv_b, conv_ln_g,
           conv_ln_b, ret_gn_g, w_out, norm2_g, w_up, w_down, final_norm_g):
    depth = w_in.shape[0]
    B, T, D = x_prompt.shape
    S, ts, _ = x_sample.shape
    heads, hd = state_ret.shape[2], state_ret.shape[3]

    cvec = jnp.stack([conv_b, conv_ln_g, conv_ln_b], axis=1)
    gf = final_norm_g[None, :]
    cos_p, sin_p = _rope_tables(0.0, T, hd)
    cos_s, sin_s = _rope_tables(float(PAST_LEN), ts, hd)

    xp = x_prompt.reshape(B * T, D)
    xs = x_sample.transpose(1, 0, 2).reshape(ts * S, D)
    cache_t = cache_conv.transpose(0, 2, 1, 3)
    prompt_state, sample_state = None, None
    for l in range(depth):
        final = l == depth - 1
        g1, g2, gn = norm1_g[l][None, :], norm2_g[l][None, :], ret_gn_g[l][None, :]

        mix_p, *rest = _prompt_mix(xp.reshape(B, T, D), g1, w_in, l, prompt_state, conv_w[l], cvec[l], gn,
                                   cos_p, sin_p, heads, hd, (w_out, w_up, w_down))
        prompt_state, post_w = rest[:2], rest[2:]

        proj = _in_proj(xs, g1, w_in, l, IN_PROJ_COLS).reshape(ts, S, -1)
        mix_s, *sample_state = _sample_mix(proj, cache_t, state_ret, l, sample_state, conv_w[l], cvec[l],
                                           gn, cos_s, sin_s)

        w_out_b, w_up_b, w_dn_b = post_w
        xp, xs = _post(xp, mix_p.reshape(B * T, D), xs, mix_s.reshape(ts * S, D), w_out_b, g2,
                       w_up_b, w_dn_b, gf, final)

    new_cache_t, new_state = sample_state
    return (xp.reshape(B, T, D), xs.reshape(ts, S, D).transpose(1, 0, 2), *prompt_state,
            new_cache_t.transpose(0, 2, 1, 3), new_state)
```

```python
import functools
import math

import numpy as np
import jax
import jax.numpy as jnp
from jax import lax
from jax.experimental import pallas as pl
from jax.experimental.pallas import tpu as pltpu

F32 = jnp.float32
BF16 = jnp.bfloat16

EPS = 1e-6
ROPE_BASE = 10000.0
PAST_LEN = 16384
LANES = 128
SUBLANES = 8
VMEM_LIMIT = 62 * 1024 * 1024

PROMPT_CHUNK = 512
RET_CHUNK = 512
CAUSAL_ROWS = 256
CONV_ROWS = 128
GLU_LANE_BLOCKS = 2
POST_ROWS = 1024
MLP_CHUNKS = 4
SAMPLE_SEQS = 32
IN_PROJ_COLS = 1024
CAST_COLS = 512


def _log_gamma(h):
    return math.log(1.0 - 2.0 ** (-5.0 - h))


def _rmsnorm(x, g):
    ms = jnp.mean(x * x, axis=-1, keepdims=True)
    return x * lax.rsqrt(ms + EPS) * g


def _layernorm(x):
    mu = jnp.mean(x, axis=-1, keepdims=True)
    xc = x - mu
    var = jnp.mean(xc * xc, axis=-1, keepdims=True)
    return xc * lax.rsqrt(var + EPS)


def _silu(x):
    return x * jax.nn.sigmoid(x)


def _rotary(t, cos2, sin2):
    half = t.shape[-1] // 2
    return t * cos2 + pltpu.roll(t, half, t.ndim - 1) * sin2


def _const_spec(shape):
    nd = len(shape)
    return pl.BlockSpec(shape, lambda *_: (0,) * nd, pipeline_mode=pl.Buffered(1))


def _decay_tables(n, heads):
    idx = np.arange(n, dtype=np.float64)
    diff = idx[:, None] - idx[None, :]
    lg = np.array([_log_gamma(h) for h in range(heads)])
    dec = np.where(diff[None] >= 0, np.exp(lg[:, None, None] * np.maximum(diff, 0.0)[None]), 0.0)
    rdec = np.exp(lg[:, None] * (idx + 1.0))[:, :, None]
    kdec = np.exp(lg[:, None] * (n - 1.0 - idx))[:, :, None]
    return dec, rdec, kdec


def _prompt_mix_kernel(x_ref, g1_ref, win_ref, cw_ref, cvec_ref, gn_ref, cos_ref, sin_ref,
                       dec_ref, rdec_ref, kdec_ref, wo_ref, wu_ref, wd_ref, *rest,
                       tq, sub, heads, hd, cdim, cwidth, hist):
    n_out, n_scratch = 6, 9
    mix_ref, nconv_ref, nret_ref, wob_ref, wub_ref, wdb_ref = rest[-(n_out + n_scratch):-n_scratch]
    w_s, glu_s, shift_s, conv_s, q_s, k_s, v_s, g_s, state_s = rest[-n_scratch:]
    b = pl.program_id(0)
    c = pl.program_id(1)
    last = pl.num_programs(1) - 1
    rdim = heads * hd

    @pl.when((b == 0) & (c == 0))
    def _():
        for lo in range(0, w_s.shape[1], CAST_COLS):
            w_s[:, lo:lo + CAST_COLS] = win_ref[:, lo:lo + CAST_COLS].astype(BF16)

    @pl.when(c == 0)
    def _():
        glu_s[0:hist, :] = jnp.zeros((hist, cdim), F32)
        shift_s[:, 0:hist, :] = jnp.zeros((SUBLANES - 1, hist, cdim), F32)
        shift_s[:, tq:tq + hist, :] = jnp.zeros((SUBLANES - 1, hist, cdim), F32)
        state_s[...] = jnp.zeros_like(state_s)

    @pl.when(c > 0)
    def _():
        glu_s[0:hist, :] = glu_s[tq:tq + hist, :]
        shift_s[:, 0:hist, :] = shift_s[:, tq:tq + hist, :]

    wob_ref[...] = wo_ref[...].astype(BF16)
    wub_ref[...] = wu_ref[...].astype(BF16)
    wdb_ref[...] = wd_ref[...].astype(BF16)

    h = _rmsnorm(x_ref[0], g1_ref[...]).astype(BF16)

    def proj(lo, n):
        return jnp.dot(h, w_s[:, lo:lo + n], preferred_element_type=F32)

    off = hist - (cwidth - 1)
    for lb in range(cdim // LANES):
        ls = slice(lb * LANES, (lb + 1) * LANES)
        if lb % GLU_LANE_BLOCKS == 0:
            n = GLU_LANE_BLOCKS * LANES
            cs = slice(lb * LANES, lb * LANES + n)
            glu = proj(lb * LANES, n) * jax.nn.sigmoid(proj(cdim + lb * LANES, n))
            glu_s[hist:hist + tq, cs] = glu
            for ph in range(1, SUBLANES):
                shift_s[ph - 1, hist - ph:hist - ph + tq, cs] = glu
        for r0 in range(0, tq, CONV_ROWS):
            acc = None
            for w in range(cwidth):
                ph = (off + w) % SUBLANES
                a = r0 + (off + w) // SUBLANES * SUBLANES
                xs = glu_s[a:a + CONV_ROWS, ls] if ph == 0 else shift_s[ph - 1, a:a + CONV_ROWS, ls]
                term = xs * cw_ref[w:w + 1, ls]
                acc = term if acc is None else acc + term
            conv_s[r0:r0 + CONV_ROWS, ls] = acc + cvec_ref[0:1, ls]
    for r0 in range(0, tq, CONV_ROWS):
        cn = _layernorm(conv_s[r0:r0 + CONV_ROWS, :]) * cvec_ref[1:2, :] + cvec_ref[2:3, :]
        mix_ref[0, r0:r0 + CONV_ROWS, 0:cdim] = _silu(cn).astype(BF16)

    c0 = 2 * cdim
    qf = proj(c0, rdim)
    kf = proj(c0 + rdim, rdim)
    cos2 = cos_ref[...]
    sin2 = sin_ref[...]
    scale = hd ** -0.5
    for hh in range(heads):
        hs = slice(hh * hd, (hh + 1) * hd)
        q_s[:, hs] = _rotary(qf[:, hs], cos2, sin2).astype(BF16)
        k_s[:, hs] = _rotary(kf[:, hs], cos2, sin2) * scale
    v_s[...] = proj(c0 + 2 * rdim, rdim).astype(BF16)
    g_s[...] = _silu(proj(c0 + 3 * rdim, rdim))

    for hh in range(heads):
        hs = slice(hh * hd, (hh + 1) * hd)
        st = state_s[hh]
        for r0 in range(0, tq, sub):
            rs = slice(r0, r0 + sub)
            qh = q_s[rs, hs]
            kh = k_s[rs, hs]
            vh = v_s[rs, hs]
            kb = kh.astype(BF16)
            parts = []
            for a in range(0, sub, CAUSAL_ROWS):
                e = a + CAUSAL_ROWS
                sc = lax.dot_general(qh[a:e], kb[0:e], (((1,), (1,)), ((), ())),
                                     preferred_element_type=F32) * dec_ref[hh, a:e, 0:e]
                parts.append(jnp.dot(sc.astype(BF16), vh[0:e], preferred_element_type=F32))
            inner = jnp.concatenate(parts, axis=0)
            cross = jnp.dot(qh, st.astype(BF16), preferred_element_type=F32) * rdec_ref[hh]
            kd = (kh * kdec_ref[hh]).astype(BF16)
            upd = lax.dot_general(kd, vh, (((0,), (0,)), ((), ())), preferred_element_type=F32)
            st = st * math.exp(_log_gamma(hh) * sub) + upd
            on = _layernorm(inner + cross)
            mix_ref[0, rs, cdim + hh * hd:cdim + (hh + 1) * hd] = (
                on * gn_ref[0:1, hs] * g_s[rs, hs]).astype(BF16)
        state_s[hh] = st

    @pl.when(c == last)
    def _():
        nconv_ref[0] = glu_s[hist + tq - (cwidth - 1):hist + tq, :]
        nret_ref[0] = state_s[...]


def _prompt_mix(x, g1, w_in, layer, prev, conv_w, cvec, gn_g, cos2, sin2, heads, hd, post_weights):
    B, T, D = x.shape
    depth = w_in.shape[0]
    cwidth, cdim = conv_w.shape
    rdim = heads * hd
    tq = PROMPT_CHUNK
    sub = RET_CHUNK
    hist = -(-(cwidth - 1) // SUBLANES) * SUBLANES
    assert T % tq == 0 and tq >= hist and tq % CONV_ROWS == 0 and tq % sub == 0 and hd == LANES
    assert w_in.shape[2] % CAST_COLS == 0

    dec, rdec, kdec = _decay_tables(sub, heads)
    dec = jnp.asarray(dec, F32)
    rdec = jnp.asarray(np.broadcast_to(rdec, (heads, sub, hd)), F32)
    kdec = jnp.asarray(np.broadcast_to(kdec, (heads, sub, hd)), F32)

    chunks = T // tq
    steps = B * chunks
    for w in post_weights:
        assert w.shape[1] % (steps * 2 * SUBLANES) == 0
    slab_in = [pl.BlockSpec((None, w.shape[1] // steps, w.shape[2]), lambda b, c: (layer, b * chunks + c, 0))
               for w in post_weights]
    slab_out = [pl.BlockSpec((w.shape[1] // steps, w.shape[2]), lambda b, c: (b * chunks + c, 0))
                for w in post_weights]

    args = [x, g1, w_in, conv_w, cvec, gn_g, cos2, sin2, dec, rdec, kdec, *post_weights]
    in_specs = [
        pl.BlockSpec((1, tq, D), lambda b, c: (b, c, 0)),
        _const_spec((1, D)),
        pl.BlockSpec((None,) + w_in.shape[1:], lambda b, c: (layer, 0, 0), pipeline_mode=pl.Buffered(1)),
        _const_spec(conv_w.shape),
        _const_spec(cvec.shape),
        _const_spec((1, rdim)),
        pl.BlockSpec((tq, hd), lambda b, c: (c, 0)),
        pl.BlockSpec((tq, hd), lambda b, c: (c, 0)),
        _const_spec(dec.shape),
        _const_spec(rdec.shape),
        _const_spec(kdec.shape),
        *slab_in,
    ]
    aliases = {}
    if prev is not None:
        aliases = {len(args): 1, len(args) + 1: 2}
        args += list(prev)
        in_specs += [pl.BlockSpec(memory_space=pl.ANY)] * 2

    kern = functools.partial(_prompt_mix_kernel, tq=tq, sub=sub, heads=heads, hd=hd, cdim=cdim,
                             cwidth=cwidth, hist=hist)
    return pl.pallas_call(
        kern,
        grid=(B, chunks),
        in_specs=in_specs,
        out_specs=[
            pl.BlockSpec((1, tq, D), lambda b, c: (b, c, 0)),
            pl.BlockSpec((None, 1, cwidth - 1, cdim), lambda b, c: (layer, b, 0, 0)),
            pl.BlockSpec((None, 1, heads, hd, hd), lambda b, c: (layer, b, 0, 0, 0)),
            *slab_out,
        ],
        out_shape=[
            jax.ShapeDtypeStruct((B, T, D), BF16),
            jax.ShapeDtypeStruct((depth, B, cwidth - 1, cdim), F32),
            jax.ShapeDtypeStruct((depth, B, heads, hd, hd), F32),
            *[jax.ShapeDtypeStruct(w.shape[1:], BF16) for w in post_weights],
        ],
        input_output_aliases=aliases,
        scratch_shapes=[
            pltpu.VMEM(w_in.shape[1:], BF16),
            pltpu.VMEM((hist + tq, cdim), F32),
            pltpu.VMEM((SUBLANES - 1, hist + tq, cdim), F32),
            pltpu.VMEM((tq, cdim), F32),
            pltpu.VMEM((tq, rdim), BF16),
            pltpu.VMEM((tq, rdim), F32),
            pltpu.VMEM((tq, rdim), BF16),
            pltpu.VMEM((tq, rdim), F32),
            pltpu.VMEM((heads, hd, hd), F32),
        ],
        compiler_params=pltpu.CompilerParams(
            dimension_semantics=("arbitrary", "arbitrary"), vmem_limit_bytes=VMEM_LIMIT),
        name="prompt_mix",
    )(*args)


def _post_kernel(xa_ref, mixa_ref, xb_ref, mixb_ref, wout_ref, g2_ref, wup_ref, wdn_ref, gf_ref,
                 oa_ref, ob_ref, *, final, steps_a):
    def tile(x_ref, mix_ref, o_ref):
        y = x_ref[...] + jnp.dot(mix_ref[...], wout_ref[...], preferred_element_type=F32)
        h2 = _rmsnorm(y, g2_ref[...]).astype(BF16)
        dff = wup_ref.shape[1]
        hc = dff // MLP_CHUNKS
        out = y
        for k in range(MLP_CHUNKS):
            hf = jnp.dot(h2, wup_ref[:, k * hc:(k + 1) * hc], preferred_element_type=F32)
            act = jnp.square(jnp.maximum(hf, 0.0)).astype(BF16)
            out = out + jnp.dot(act, wdn_ref[k * hc:(k + 1) * hc, :], preferred_element_type=F32)
        if final:
            out = _rmsnorm(out, gf_ref[...])
        o_ref[...] = out

    i = pl.program_id(0)

    @pl.when(i < steps_a)
    def _():
        tile(xa_ref, mixa_ref, oa_ref)

    @pl.when(i >= steps_a)
    def _():
        tile(xb_ref, mixb_ref, ob_ref)


def _post(xa, mixa, xb, mixb, w_out, g2, w_up, w_down, gf, final):
    Ra, D = xa.shape
    Rb = xb.shape[0]
    tra, trb = min(POST_ROWS, Ra), min(POST_ROWS, Rb)
    assert Ra % tra == 0 and Rb % trb == 0
    na, nb = Ra // tra, Rb // trb
    rows_a = pl.BlockSpec((tra, D), lambda i: (jnp.minimum(i, na - 1), 0))
    rows_b = pl.BlockSpec((trb, D), lambda i: (jnp.maximum(i - na, 0), 0))

    return pl.pallas_call(
        functools.partial(_post_kernel, final=final, steps_a=na),
        grid=(na + nb,),
        in_specs=[
            rows_a, rows_a, rows_b, rows_b,
            _const_spec(w_out.shape),
            _const_spec((1, D)),
            _const_spec(w_up.shape),
            _const_spec(w_down.shape),
            _const_spec((1, D)),
        ],
        out_specs=[rows_a, rows_b],
        out_shape=[jax.ShapeDtypeStruct((Ra, D), F32), jax.ShapeDtypeStruct((Rb, D), F32)],
        compiler_params=pltpu.CompilerParams(
            dimension_semantics=("arbitrary",), vmem_limit_bytes=VMEM_LIMIT),
        name="post",
    )(xa, mixa, xb, mixb, w_out, g2, w_up, w_down, gf)


def _in_proj_kernel(x_ref, g1_ref, w_ref, o_ref, h_s):
    @pl.when(pl.program_id(0) == 0)
    def _():
        h_s[...] = _rmsnorm(x_ref[...], g1_ref[...]).astype(BF16)

    o_ref[...] = jnp.dot(h_s[...], w_ref[...].astype(BF16), preferred_element_type=F32)


def _in_proj(x, g1, w_in, layer, ncol):
    R, D = x.shape
    C = w_in.shape[2]
    return pl.pallas_call(
        _in_proj_kernel,
        grid=(C // ncol,),
        in_specs=[
            _const_spec((R, D)),
            _const_spec((1, D)),
            pl.BlockSpec((None, D, ncol), lambda j: (layer, 0, j)),
        ],
        out_specs=pl.BlockSpec((R, ncol), lambda j: (0, j)),
        out_shape=jax.ShapeDtypeStruct((R, C), F32),
        scratch_shapes=[pltpu.VMEM((R, D), BF16)],
        compiler_params=pltpu.CompilerParams(
            dimension_semantics=("arbitrary",), vmem_limit_bytes=VMEM_LIMIT),
        name="sample_in_proj",
    )(x, g1, w_in)


def _sample_mix_kernel(proj_ref, cache_ref, st_ref, cw_ref, cvec_ref, gn_ref, cos_ref, sin_ref,
                       dec_ref, rdec_ref, kdec_ref, *rest, ts, nb, heads, hd, cdim, cwidth):
    mix_ref, ncache_ref, nst_ref = rest[-3:]
    rdim = heads * hd

    glu = [proj_ref[t, :, 0:cdim] * jax.nn.sigmoid(proj_ref[t, :, cdim:2 * cdim]) for t in range(ts)]

    def window(r):
        return cache_ref[r] if r < cwidth - 1 else glu[r - (cwidth - 1)]

    acc = [None] * ts
    for r in range(cwidth - 1 + ts):
        x = window(r)
        for t in range(max(0, r - cwidth + 1), min(ts, r + 1)):
            term = x * cw_ref[r - t:r - t + 1, :]
            acc[t] = term if acc[t] is None else acc[t] + term
    for t in range(ts):
        cn = _layernorm(acc[t] + cvec_ref[0:1, :]) * cvec_ref[1:2, :] + cvec_ref[2:3, :]
        mix_ref[t, :, 0:cdim] = _silu(cn).astype(BF16)
    for r in range(cwidth - 1):
        ncache_ref[r] = window(r + ts)

    c0 = 2 * cdim
    cos2 = cos_ref[...]
    sin2 = sin_ref[...]
    scale = hd ** -0.5
    for hh in range(heads):
        lo = hh * hd

        def per_seq(base):
            return jnp.stack([proj_ref[:, s, base + lo:base + lo + hd] for s in range(nb)], axis=0)

        q = _rotary(per_seq(c0), cos2, sin2)
        k = _rotary(per_seq(c0 + rdim), cos2, sin2) * scale
        v = per_seq(c0 + 2 * rdim)
        g = per_seq(c0 + 3 * rdim)
        st = st_ref[:, hh]
        scores = jnp.einsum('btd,bjd->btj', q, k, preferred_element_type=F32) * dec_ref[hh]
        inner = jnp.einsum('btj,bjv->btv', scores, v, preferred_element_type=F32)
        cross = jnp.einsum('btd,bdv->btv', q, st, preferred_element_type=F32) * rdec_ref[hh]
        upd = jnp.einsum('bjd,bjv->bdv', k * kdec_ref[hh], v, preferred_element_type=F32)
        nst_ref[:, hh] = st * math.exp(_log_gamma(hh) * ts) + upd
        on = _layernorm(inner + cross)
        out = (on * gn_ref[0:1, lo:lo + hd] * _silu(g)).astype(BF16)
        for s in range(nb):
            mix_ref[:, s, cdim + lo:cdim + lo + hd] = out[s]


def _sample_mix(proj, cache, state, layer, prev, conv_w, cvec, gn_g, cos2, sin2):
    ts, S, C = proj.shape
    depth, _, heads, hd, _ = state.shape
    cwidth, cdim = conv_w.shape
    rdim = heads * hd
    D = cdim + rdim
    nb = SAMPLE_SEQS
    assert S % nb == 0 and hd == LANES

    dec, rdec, kdec = (jnp.asarray(t, F32) for t in _decay_tables(ts, heads))

    conv_spec = pl.BlockSpec((None, cwidth - 1, nb, cdim), lambda i: (layer, 0, i, 0))
    st_spec = pl.BlockSpec((None, nb, heads, hd, hd), lambda i: (layer, i, 0, 0, 0))
    args = [proj, cache, state, conv_w, cvec, gn_g, cos2, sin2, dec, rdec, kdec]
    in_specs = [
        pl.BlockSpec((ts, nb, C), lambda i: (0, i, 0)),
        conv_spec,
        st_spec,
        _const_spec(conv_w.shape),
        _const_spec(cvec.shape),
        _const_spec((1, rdim)),
        _const_spec((ts, hd)),
        _const_spec((ts, hd)),
        _const_spec(dec.shape),
        _const_spec(rdec.shape),
        _const_spec(kdec.shape),
    ]
    aliases = {}
    if prev is not None:
        aliases = {len(args): 1, len(args) + 1: 2}
        args += list(prev)
        in_specs += [pl.BlockSpec(memory_space=pl.ANY)] * 2

    kern = functools.partial(_sample_mix_kernel, ts=ts, nb=nb, heads=heads, hd=hd, cdim=cdim, cwidth=cwidth)
    return pl.pallas_call(
        kern,
        grid=(S // nb,),
        in_specs=in_specs,
        out_specs=[pl.BlockSpec((ts, nb, D), lambda i: (0, i, 0)), conv_spec, st_spec],
        out_shape=[
            jax.ShapeDtypeStruct((ts, S, D), BF16),
            jax.ShapeDtypeStruct((depth, cwidth - 1, S, cdim), F32),
            jax.ShapeDtypeStruct((depth, S, heads, hd, hd), F32),
        ],
        input_output_aliases=aliases,
        compiler_params=pltpu.CompilerParams(
            dimension_semantics=("arbitrary",), vmem_limit_bytes=VMEM_LIMIT),
        name="sample_mix",
    )(*args)


def _rope_tables(pos0, T, hd):
    half = hd // 2
    pos = pos0 + np.arange(T, dtype=np.float64)
    inv = ROPE_BASE ** (-np.arange(half, dtype=np.float64) / half)
    ang = pos[:, None] * inv[None, :]
    cos, sin = np.cos(ang), np.sin(ang)
    return (jnp.asarray(np.concatenate([cos, cos], axis=-1), F32),
            jnp.asarray(np.concatenate([-sin, sin], axis=-1), F32))


def kernel(x_prompt, x_sample, cache_conv, state_ret, norm1_g, w_in, conv_w, conv_b, conv_ln_g,
           conv_ln_b, ret_gn_g, w_out, norm2_g, w_up, w_down, final_norm_g):
    depth = w_in.shape[0]
    B, T, D = x_prompt.shape
    S, ts, _ = x_sample.shape
    heads, hd = state_ret.shape[2], state_ret.shape[3]

    cvec = jnp.stack([conv_b, conv_ln_g, conv_ln_b], axis=1)
    gf = final_norm_g[None, :]
    cos_p, sin_p = _rope_tables(0.0, T, hd)
    cos_s, sin_s = _rope_tables(float(PAST_LEN), ts, hd)

    xp = x_prompt.reshape(B * T, D)
    xs = x_sample.transpose(1, 0, 2).reshape(ts * S, D)
    cache_t = cache_conv.transpose(0, 2, 1, 3)
    prompt_state, sample_state = None, None
    for l in range(depth):
        final = l == depth - 1
        g1, g2, gn = norm1_g[l][None, :], norm2_g[l][None, :], ret_gn_g[l][None, :]

        mix_p, *rest = _prompt_mix(xp.reshape(B, T, D), g1, w_in, l, prompt_state, conv_w[l], cvec[l], gn,
                                   cos_p, sin_p, heads, hd, (w_out, w_up, w_down))
        prompt_state, post_w = rest[:2], rest[2:]

        proj = _in_proj(xs, g1, w_in, l, IN_PROJ_COLS).reshape(ts, S, -1)
        mix_s, *sample_state = _sample_mix(proj, cache_t, state_ret, l, sample_state, conv_w[l], cvec[l],
                                           gn, cos_s, sin_s)

        w_out_b, w_up_b, w_dn_b = post_w
        xp, xs = _post(xp, mix_p.reshape(B * T, D), xs, mix_s.reshape(ts * S, D), w_out_b, g2,
                       w_up_b, w_dn_b, gf, final)

    new_cache_t, new_state = sample_state
    return (xp.reshape(B, T, D), xs.reshape(ts, S, D).transpose(1, 0, 2), *prompt_state,
            new_cache_t.transpose(0, 2, 1, 3), new_state)
```

```python
import functools
import math

import numpy as np
import jax
import jax.numpy as jnp
from jax import lax
from jax.experimental import pallas as pl
from jax.experimental.pallas import tpu as pltpu

F32 = jnp.float32
BF16 = jnp.bfloat16

EPS = 1e-6
ROPE_BASE = 10000.0
PAST_LEN = 16384
LANES = 128
SUBLANES = 8
VMEM_LIMIT = 62 * 1024 * 1024

PROMPT_CHUNK = 512
RET_CHUNK = 512
CAUSAL_ROWS = 256
CONV_ROWS = 128
GLU_LANE_BLOCKS = 2
POST_ROWS = 1024
MLP_CHUNKS = 4
SAMPLE_SEQS = 32
IN_PROJ_COLS = 1536
CAST_COLS = 512


def _log_gamma(h):
    return math.log(1.0 - 2.0 ** (-5.0 - h))


def _rmsnorm(x, g):
    ms = jnp.mean(x * x, axis=-1, keepdims=True)
    return x * lax.rsqrt(ms + EPS) * g


def _layernorm(x):
    mu = jnp.mean(x, axis=-1, keepdims=True)
    xc = x - mu
    var = jnp.mean(xc * xc, axis=-1, keepdims=True)
    return xc * lax.rsqrt(var + EPS)


def _silu(x):
    return x * jax.nn.sigmoid(x)


def _rotary(t, cos2, sin2):
    half = t.shape[-1] // 2
    return t * cos2 + pltpu.roll(t, half, t.ndim - 1) * sin2


def _const_spec(shape):
    nd = len(shape)
    return pl.BlockSpec(shape, lambda *_: (0,) * nd, pipeline_mode=pl.Buffered(1))


def _decay_tables(n, heads):
    idx = np.arange(n, dtype=np.float64)
    diff = idx[:, None] - idx[None, :]
    lg = np.array([_log_gamma(h) for h in range(heads)])
    dec = np.where(diff[None] >= 0, np.exp(lg[:, None, None] * np.maximum(diff, 0.0)[None]), 0.0)
    rdec = np.exp(lg[:, None] * (idx + 1.0))[:, :, None]
    kdec = np.exp(lg[:, None] * (n - 1.0 - idx))[:, :, None]
    return dec, rdec, kdec


def _prompt_mix_kernel(x_ref, g1_ref, win_ref, cw_ref, cvec_ref, gn_ref, cos_ref, sin_ref,
                       dec_ref, rdec_ref, kdec_ref, wo_ref, wu_ref, wd_ref, *rest,
                       tq, sub, heads, hd, cdim, cwidth, hist):
    n_out, n_scratch = 6, 9
    mix_ref, nconv_ref, nret_ref, wob_ref, wub_ref, wdb_ref = rest[-(n_out + n_scratch):-n_scratch]
    w_s, glu_s, shift_s, conv_s, q_s, k_s, v_s, g_s, state_s = rest[-n_scratch:]
    b = pl.program_id(0)
    c = pl.program_id(1)
    last = pl.num_programs(1) - 1
    rdim = heads * hd

    @pl.when((b == 0) & (c == 0))
    def _():
        for lo in range(0, w_s.shape[1], CAST_COLS):
            w_s[:, lo:lo + CAST_COLS] = win_ref[:, lo:lo + CAST_COLS].astype(BF16)

    @pl.when(c == 0)
    def _():
        glu_s[0:hist, :] = jnp.zeros((hist, cdim), F32)
        shift_s[:, 0:hist, :] = jnp.zeros((SUBLANES - 1, hist, cdim), F32)
        shift_s[:, tq:tq + hist, :] = jnp.zeros((SUBLANES - 1, hist, cdim), F32)
        state_s[...] = jnp.zeros_like(state_s)

    @pl.when(c > 0)
    def _():
        glu_s[0:hist, :] = glu_s[tq:tq + hist, :]
        shift_s[:, 0:hist, :] = shift_s[:, tq:tq + hist, :]

    wob_ref[...] = wo_ref[...].astype(BF16)
    wub_ref[...] = wu_ref[...].astype(BF16)
    wdb_ref[...] = wd_ref[...].astype(BF16)

    h = _rmsnorm(x_ref[0], g1_ref[...]).astype(BF16)

    def proj(lo, n):
        return jnp.dot(h, w_s[:, lo:lo + n], preferred_element_type=F32)

    off = hist - (cwidth - 1)
    for lb in range(cdim // LANES):
        ls = slice(lb * LANES, (lb + 1) * LANES)
        if lb % GLU_LANE_BLOCKS == 0:
            n = GLU_LANE_BLOCKS * LANES
            cs = slice(lb * LANES, lb * LANES + n)
            glu = proj(lb * LANES, n) * jax.nn.sigmoid(proj(cdim + lb * LANES, n))
            glu_s[hist:hist + tq, cs] = glu
            for ph in range(1, SUBLANES):
                shift_s[ph - 1, hist - ph:hist - ph + tq, cs] = glu
        for r0 in range(0, tq, CONV_ROWS):
            acc = None
            for w in range(cwidth):
                ph = (off + w) % SUBLANES
                a = r0 + (off + w) // SUBLANES * SUBLANES
                xs = glu_s[a:a + CONV_ROWS, ls] if ph == 0 else shift_s[ph - 1, a:a + CONV_ROWS, ls]
                term = xs * cw_ref[w:w + 1, ls]
                acc = term if acc is None else acc + term
            conv_s[r0:r0 + CONV_ROWS, ls] = acc + cvec_ref[0:1, ls]
    for r0 in range(0, tq, CONV_ROWS):
        cn = _layernorm(conv_s[r0:r0 + CONV_ROWS, :]) * cvec_ref[1:2, :] + cvec_ref[2:3, :]
        mix_ref[0, r0:r0 + CONV_ROWS, 0:cdim] = _silu(cn).astype(BF16)

    c0 = 2 * cdim
    qf = proj(c0, rdim)
    kf = proj(c0 + rdim, rdim)
    cos2 = cos_ref[...]
    sin2 = sin_ref[...]
    scale = hd ** -0.5
    for hh in range(heads):
        hs = slice(hh * hd, (hh + 1) * hd)
        q_s[:, hs] = _rotary(qf[:, hs], cos2, sin2).astype(BF16)
        k_s[:, hs] = _rotary(kf[:, hs], cos2, sin2) * scale
    v_s[...] = proj(c0 + 2 * rdim, rdim).astype(BF16)
    g_s[...] = _silu(proj(c0 + 3 * rdim, rdim))

    for hh in range(heads):
        hs = slice(hh * hd, (hh + 1) * hd)
        st = state_s[hh]
        for r0 in range(0, tq, sub):
            rs = slice(r0, r0 + sub)
            qh = q_s[rs, hs]
            kh = k_s[rs, hs]
            vh = v_s[rs, hs]
            kb = kh.astype(BF16)
            parts = []
            for a in range(0, sub, CAUSAL_ROWS):
                e = a + CAUSAL_ROWS
                sc = lax.dot_general(qh[a:e], kb[0:e], (((1,), (1,)), ((), ())),
                                     preferred_element_type=F32) * dec_ref[hh, a:e, 0:e]
                parts.append(jnp.dot(sc.astype(BF16), vh[0:e], preferred_element_type=F32))
            inner = jnp.concatenate(parts, axis=0)
            cross = jnp.dot(qh, st.astype(BF16), preferred_element_type=F32) * rdec_ref[hh]
            kd = (kh * kdec_ref[hh]).astype(BF16)
            upd = lax.dot_general(kd, vh, (((0,), (0,)), ((), ())), preferred_element_type=F32)
            st = st * math.exp(_log_gamma(hh) * sub) + upd
            on = _layernorm(inner + cross)
            mix_ref[0, rs, cdim + hh * hd:cdim + (hh + 1) * hd] = (
                on * gn_ref[0:1, hs] * g_s[rs, hs]).astype(BF16)
        state_s[hh] = st

    @pl.when(c == last)
    def _():
        nconv_ref[0] = glu_s[hist + tq - (cwidth - 1):hist + tq, :]
        nret_ref[0] = state_s[...]


def _prompt_mix(x, g1, w_in, layer, prev, conv_w, cvec, gn_g, cos2, sin2, heads, hd, post_weights):
    B, T, D = x.shape
    depth = w_in.shape[0]
    cwidth, cdim = conv_w.shape
    rdim = heads * hd
    tq = PROMPT_CHUNK
    sub = RET_CHUNK
    hist = -(-(cwidth - 1) // SUBLANES) * SUBLANES
    assert T % tq == 0 and tq >= hist and tq % CONV_ROWS == 0 and tq % sub == 0 and hd == LANES
    assert sub % CAUSAL_ROWS == 0
    assert w_in.shape[2] % CAST_COLS == 0

    dec, rdec, kdec = _decay_tables(sub, heads)
    dec = jnp.asarray(dec, F32)
    rdec = jnp.asarray(np.broadcast_to(rdec, (heads, sub, hd)), F32)
    kdec = jnp.asarray(np.broadcast_to(kdec, (heads, sub, hd)), F32)

    chunks = T // tq
    steps = B * chunks
    for w in post_weights:
        assert w.shape[1] % (steps * 2 * SUBLANES) == 0
    slab_in = [pl.BlockSpec((None, w.shape[1] // steps, w.shape[2]), lambda b, c: (layer, b * chunks + c, 0))
               for w in post_weights]
    slab_out = [pl.BlockSpec((w.shape[1] // steps, w.shape[2]), lambda b, c: (b * chunks + c, 0))
                for w in post_weights]

    args = [x, g1, w_in, conv_w, cvec, gn_g, cos2, sin2, dec, rdec, kdec, *post_weights]
    in_specs = [
        pl.BlockSpec((1, tq, D), lambda b, c: (b, c, 0)),
        _const_spec((1, D)),
        pl.BlockSpec((None,) + w_in.shape[1:], lambda b, c: (layer, 0, 0), pipeline_mode=pl.Buffered(1)),
        _const_spec(conv_w.shape),
        _const_spec(cvec.shape),
        _const_spec((1, rdim)),
        pl.BlockSpec((tq, hd), lambda b, c: (c, 0)),
        pl.BlockSpec((tq, hd), lambda b, c: (c, 0)),
        _const_spec(dec.shape),
        _const_spec(rdec.shape),
        _const_spec(kdec.shape),
        *slab_in,
    ]
    aliases = {}
    if prev is not None:
        aliases = {len(args): 1, len(args) + 1: 2}
        args += list(prev)
        in_specs += [pl.BlockSpec(memory_space=pl.ANY)] * 2

    kern = functools.partial(_prompt_mix_kernel, tq=tq, sub=sub, heads=heads, hd=hd, cdim=cdim,
                             cwidth=cwidth, hist=hist)
    return pl.pallas_call(
        kern,
        grid=(B, chunks),
        in_specs=in_specs,
        out_specs=[
            pl.BlockSpec((1, tq, D), lambda b, c: (b, c, 0)),
            pl.BlockSpec((None, 1, cwidth - 1, cdim), lambda b, c: (layer, b, 0, 0)),
            pl.BlockSpec((None, 1, heads, hd, hd), lambda b, c: (layer, b, 0, 0, 0)),
            *slab_out,
        ],
        out_shape=[
            jax.ShapeDtypeStruct((B, T, D), BF16),
            jax.ShapeDtypeStruct((depth, B, cwidth - 1, cdim), F32),
            jax.ShapeDtypeStruct((depth, B, heads, hd, hd), F32),
            *[jax.ShapeDtypeStruct(w.shape[1:], BF16) for w in post_weights],
        ],
        input_output_aliases=aliases,
        scratch_shapes=[
            pltpu.VMEM(w_in.shape[1:], BF16),
            pltpu.VMEM((hist + tq, cdim), F32),
            pltpu.VMEM((SUBLANES - 1, hist + tq, cdim), F32),
            pltpu.VMEM((tq, cdim), F32),
            pltpu.VMEM((tq, rdim), BF16),
            pltpu.VMEM((tq, rdim), F32),
            pltpu.VMEM((tq, rdim), BF16),
            pltpu.VMEM((tq, rdim), F32),
            pltpu.VMEM((heads, hd, hd), F32),
        ],
        compiler_params=pltpu.CompilerParams(
            dimension_semantics=("arbitrary", "arbitrary"), vmem_limit_bytes=VMEM_LIMIT),
        name="prompt_mix",
    )(*args)


def _post_kernel(xa_ref, mixa_ref, xb_ref, mixb_ref, wout_ref, g2_ref, wup_ref, wdn_ref, gf_ref,
                 oa_ref, ob_ref, *, final, steps_a):
    def tile(x_ref, mix_ref, o_ref):
        y = x_ref[...] + jnp.dot(mix_ref[...], wout_ref[...], preferred_element_type=F32)
        h2 = _rmsnorm(y, g2_ref[...]).astype(BF16)
        dff = wup_ref.shape[1]
        hc = dff // MLP_CHUNKS
        out = y
        for k in range(MLP_CHUNKS):
            hf = jnp.dot(h2, wup_ref[:, k * hc:(k + 1) * hc], preferred_element_type=F32)
            act = jnp.square(jnp.maximum(hf, 0.0)).astype(BF16)
            out = out + jnp.dot(act, wdn_ref[k * hc:(k + 1) * hc, :], preferred_element_type=F32)
        if final:
            out = _rmsnorm(out, gf_ref[...])
        o_ref[...] = out

    i = pl.program_id(0)

    @pl.when(i < steps_a)
    def _():
        tile(xa_ref, mixa_ref, oa_ref)

    @pl.when(i >= steps_a)
    def _():
        tile(xb_ref, mixb_ref, ob_ref)


def _post(xa, mixa, xb, mixb, w_out, g2, w_up, w_down, gf, final):
    Ra, D = xa.shape
    Rb = xb.shape[0]
    tra, trb = min(POST_ROWS, Ra), min(POST_ROWS, Rb)
    assert Ra % tra == 0 and Rb % trb == 0
    na, nb = Ra // tra, Rb // trb
    rows_a = pl.BlockSpec((tra, D), lambda i: (jnp.minimum(i, na - 1), 0))
    rows_b = pl.BlockSpec((trb, D), lambda i: (jnp.maximum(i - na, 0), 0))

    return pl.pallas_call(
        functools.partial(_post_kernel, final=final, steps_a=na),
        grid=(na + nb,),
        in_specs=[
            rows_a, rows_a, rows_b, rows_b,
            _const_spec(w_out.shape),
            _const_spec((1, D)),
            _const_spec(w_up.shape),
            _const_spec(w_down.shape),
            _const_spec((1, D)),
        ],
        out_specs=[rows_a, rows_b],
        out_shape=[jax.ShapeDtypeStruct((Ra, D), F32), jax.ShapeDtypeStruct((Rb, D), F32)],
        compiler_params=pltpu.CompilerParams(
            dimension_semantics=("arbitrary",), vmem_limit_bytes=VMEM_LIMIT),
        name="post",
    )(xa, mixa, xb, mixb, w_out, g2, w_up, w_down, gf)


def _in_proj_kernel(x_ref, g1_ref, w_ref, o_ref, h_s):
    @pl.when(pl.program_id(0) == 0)
    def _():
        h_s[...] = _rmsnorm(x_ref[...], g1_ref[...]).astype(BF16)

    o_ref[...] = jnp.dot(h_s[...], w_ref[...].astype(BF16), preferred_element_type=F32)


def _in_proj(x, g1, w_in, layer, ncol):
    R, D = x.shape
    C = w_in.shape[2]
    return pl.pallas_call(
        _in_proj_kernel,
        grid=(C // ncol,),
        in_specs=[
            _const_spec((R, D)),
            _const_spec((1, D)),
            pl.BlockSpec((None, D, ncol), lambda j: (layer, 0, j)),
        ],
        out_specs=pl.BlockSpec((R, ncol), lambda j: (0, j)),
        out_shape=jax.ShapeDtypeStruct((R, C), F32),
        scratch_shapes=[pltpu.VMEM((R, D), BF16)],
        compiler_params=pltpu.CompilerParams(
            dimension_semantics=("arbitrary",), vmem_limit_bytes=VMEM_LIMIT),
        name="sample_in_proj",
    )(x, g1, w_in)


def _sample_mix_kernel(proj_ref, cache_ref, st_ref, cw_ref, cvec_ref, gn_ref, cos_ref, sin_ref,
                       dec_ref, rdec_ref, kdec_ref, *rest, ts, nb, heads, hd, cdim, cwidth):
    mix_ref, ncache_ref, nst_ref = rest[-3:]
    rdim = heads * hd

    glu = [proj_ref[t, :, 0:cdim] * jax.nn.sigmoid(proj_ref[t, :, cdim:2 * cdim]) for t in range(ts)]

    def window(r):
        return cache_ref[r] if r < cwidth - 1 else glu[r - (cwidth - 1)]

    acc = [None] * ts
    for r in range(cwidth - 1 + ts):
        x = window(r)
        for t in range(max(0, r - cwidth + 1), min(ts, r + 1)):
            term = x * cw_ref[r - t:r - t + 1, :]
            acc[t] = term if acc[t] is None else acc[t] + term
    for t in range(ts):
        cn = _layernorm(acc[t] + cvec_ref[0:1, :]) * cvec_ref[1:2, :] + cvec_ref[2:3, :]
        mix_ref[t, :, 0:cdim] = _silu(cn).astype(BF16)
    for r in range(cwidth - 1):
        ncache_ref[r] = window(r + ts)

    c0 = 2 * cdim
    cos2 = cos_ref[...]
    sin2 = sin_ref[...]
    scale = hd ** -0.5
    for hh in range(heads):
        lo = hh * hd

        def per_seq(base):
            return jnp.stack([proj_ref[:, s, base + lo:base + lo + hd] for s in range(nb)], axis=0)

        q = _rotary(per_seq(c0), cos2, sin2)
        k = _rotary(per_seq(c0 + rdim), cos2, sin2) * scale
        v = per_seq(c0 + 2 * rdim)
        g = per_seq(c0 + 3 * rdim)
        st = st_ref[:, hh]
        scores = jnp.einsum('btd,bjd->btj', q, k, preferred_element_type=F32) * dec_ref[hh]
        inner = jnp.einsum('btj,bjv->btv', scores, v, preferred_element_type=F32)
        cross = jnp.einsum('btd,bdv->btv', q, st, preferred_element_type=F32) * rdec_ref[hh]
        upd = jnp.einsum('bjd,bjv->bdv', k * kdec_ref[hh], v, preferred_element_type=F32)
        nst_ref[:, hh] = st * math.exp(_log_gamma(hh) * ts) + upd
        on = _layernorm(inner + cross)
        out = (on * gn_ref[0:1, lo:lo + hd] * _silu(g)).astype(BF16)
        for s in range(nb):
            mix_ref[:, s, cdim + lo:cdim + lo + hd] = out[s]


def _sample_mix(proj, cache, state, layer, prev, conv_w, cvec, gn_g, cos2, sin2):
    ts, S, C = proj.shape
    depth, _, heads, hd, _ = state.shape
    cwidth, cdim = conv_w.shape
    rdim = heads * hd
    D = cdim + rdim
    nb = SAMPLE_SEQS
    assert S % nb == 0 and hd == LANES

    dec, rdec, kdec = (jnp.asarray(t, F32) for t in _decay_tables(ts, heads))

    conv_spec = pl.BlockSpec((None, cwidth - 1, nb, cdim), lambda i: (layer, 0, i, 0))
    st_spec = pl.BlockSpec((None, nb, heads, hd, hd), lambda i: (layer, i, 0, 0, 0))
    args = [proj, cache, state, conv_w, cvec, gn_g, cos2, sin2, dec, rdec, kdec]
    in_specs = [
        pl.BlockSpec((ts, nb, C), lambda i: (0, i, 0)),
        conv_spec,
        st_spec,
        _const_spec(conv_w.shape),
        _const_spec(cvec.shape),
        _const_spec((1, rdim)),
        _const_spec((ts, hd)),
        _const_spec((ts, hd)),
        _const_spec(dec.shape),
        _const_spec(rdec.shape),
        _const_spec(kdec.shape),
    ]
    aliases = {}
    if prev is not None:
        aliases = {len(args): 1, len(args) + 1: 2}
        args += list(prev)
        in_specs += [pl.BlockSpec(memory_space=pl.ANY)] * 2

    kern = functools.partial(_sample_mix_kernel, ts=ts, nb=nb, heads=heads, hd=hd, cdim=cdim, cwidth=cwidth)
    return pl.pallas_call(
        kern,
        grid=(S // nb,),
        in_specs=in_specs,
        out_specs=[pl.BlockSpec((ts, nb, D), lambda i: (0, i, 0)), conv_spec, st_spec],
        out_shape=[
            jax.ShapeDtypeStruct((ts, S, D), BF16),
            jax.ShapeDtypeStruct((depth, cwidth - 1, S, cdim), F32),
            jax.ShapeDtypeStruct((depth, S, heads, hd, hd), F32),
        ],
        input_output_aliases=aliases,
        compiler_params=pltpu.CompilerParams(
            dimension_semantics=("arbitrary",), vmem_limit_bytes=VMEM_LIMIT),
        name="sample_mix",
    )(*args)


def _rope_tables(pos0, T, hd):
    half = hd // 2
    pos = pos0 + np.arange(T, dtype=np.float64)
    inv = ROPE_BASE ** (-np.arange(half, dtype=np.float64) / half)
    ang = pos[:, None] * inv[None, :]
    cos, sin = np.cos(ang), np.sin(ang)
    return (jnp.asarray(np.concatenate([cos, cos], axis=-1), F32),
            jnp.asarray(np.concatenate([-sin, sin], axis=-1), F32))


def kernel(x_prompt, x_sample, cache_conv, state_ret, norm1_g, w_in, conv_w, conv_b, conv_ln_g,
           conv_ln_b, ret_gn_g, w_out, norm2_g, w_up, w_down, final_norm_g):
    depth = w_in.shape[0]
    B, T, D = x_prompt.shape
    S, ts, _ = x_sample.shape
    heads, hd = state_ret.shape[2], state_ret.shape[3]

    cvec = jnp.stack([conv_b, conv_ln_g, conv_ln_b], axis=1)
    gf = final_norm_g[None, :]
    cos_p, sin_p = _rope_tables(0.0, T, hd)
    cos_s, sin_s = _rope_tables(float(PAST_LEN), ts, hd)

    xp = x_prompt.reshape(B * T, D)
    xs = x_sample.transpose(1, 0, 2).reshape(ts * S, D)
    cache_t = cache_conv.transpose(0, 2, 1, 3)
    prompt_state, sample_state = None, None
    for l in range(depth):
        final = l == depth - 1
        g1, g2, gn = norm1_g[l][None, :], norm2_g[l][None, :], ret_gn_g[l][None, :]

        mix_p, *rest = _prompt_mix(xp.reshape(B, T, D), g1, w_in, l, prompt_state, conv_w[l], cvec[l], gn,
                                   cos_p, sin_p, heads, hd, (w_out, w_up, w_down))
        prompt_state, post_w = rest[:2], rest[2:]

        proj = _in_proj(xs, g1, w_in, l, IN_PROJ_COLS).reshape(ts, S, -1)
        mix_s, *sample_state = _sample_mix(proj, cache_t, state_ret, l, sample_state, conv_w[l], cvec[l],
                                           gn, cos_s, sin_s)

        w_out_b, w_up_b, w_dn_b = post_w
        xp, xs = _post(xp, mix_p.reshape(B * T, D), xs, mix_s.reshape(ts * S, D), w_out_b, g2,
                       w_up_b, w_dn_b, gf, final)

    new_cache_t, new_state = sample_state
    return (xp.reshape(B, T, D), xs.reshape(ts, S, D).transpose(1, 0, 2), *prompt_state,
            new_cache_t.transpose(0, 2, 1, 3), new_state)
```

```python
import functools
import math

import numpy as np
import jax
import jax.numpy as jnp
from jax import lax
from jax.experimental import pallas as pl
from jax.experimental.pallas import tpu as pltpu

F32 = jnp.float32
BF16 = jnp.bfloat16

EPS = 1e-6
ROPE_BASE = 10000.0
PAST_LEN = 16384
LANES = 128
SUBLANES = 8
VMEM_LIMIT = 62 * 1024 * 1024

PROMPT_CHUNK = 512
RET_CHUNK = 512
CAUSAL_ROWS = 256
CONV_ROWS = 128
GLU_LANE_BLOCKS = 2
POST_ROWS = 1024
MLP_CHUNKS = 4
SAMPLE_SEQS = 32
IN_PROJ_COLS = 1536


def _log_gamma(h):
    return math.log(1.0 - 2.0 ** (-5.0 - h))


def _rmsnorm(x, g):
    ms = jnp.mean(x * x, axis=-1, keepdims=True)
    return x * lax.rsqrt(ms + EPS) * g


def _layernorm(x):
    mu = jnp.mean(x, axis=-1, keepdims=True)
    xc = x - mu
    var = jnp.mean(xc * xc, axis=-1, keepdims=True)
    return xc * lax.rsqrt(var + EPS)


def _silu(x):
    return x * jax.nn.sigmoid(x)


def _rotary(t, cos2, sin2):
    half = t.shape[-1] // 2
    return t * cos2 + pltpu.roll(t, half, t.ndim - 1) * sin2


def _const_spec(shape):
    nd = len(shape)
    return pl.BlockSpec(shape, lambda *_: (0,) * nd, pipeline_mode=pl.Buffered(1))


def _decay_tables(n, heads):
    idx = np.arange(n, dtype=np.float64)
    diff = idx[:, None] - idx[None, :]
    lg = np.array([_log_gamma(h) for h in range(heads)])
    dec = np.where(diff[None] >= 0, np.exp(lg[:, None, None] * np.maximum(diff, 0.0)[None]), 0.0)
    rdec = np.exp(lg[:, None] * (idx + 1.0))[:, :, None]
    kdec = np.exp(lg[:, None] * (n - 1.0 - idx))[:, :, None]
    return dec, rdec, kdec


def _prompt_mix_kernel(x_ref, g1_ref, win_ref, cw_ref, cvec_ref, gn_ref, cos_ref, sin_ref,
                       dec_ref, rdec_ref, kdec_ref, wo_ref, wu_ref, wd_ref, *rest,
                       tq, sub, heads, hd, cdim, cwidth, hist):
    n_out, n_scratch = 6, 8
    mix_ref, nconv_ref, nret_ref, wob_ref, wub_ref, wdb_ref = rest[-(n_out + n_scratch):-n_scratch]
    glu_s, shift_s, conv_s, q_s, k_s, v_s, g_s, state_s = rest[-n_scratch:]
    c = pl.program_id(1)
    last = pl.num_programs(1) - 1
    rdim = heads * hd

    @pl.when(c == 0)
    def _():
        glu_s[0:hist, :] = jnp.zeros((hist, cdim), F32)
        shift_s[:, 0:hist, :] = jnp.zeros((SUBLANES - 1, hist, cdim), F32)
        shift_s[:, tq:tq + hist, :] = jnp.zeros((SUBLANES - 1, hist, cdim), F32)
        state_s[...] = jnp.zeros_like(state_s)

    @pl.when(c > 0)
    def _():
        glu_s[0:hist, :] = glu_s[tq:tq + hist, :]
        shift_s[:, 0:hist, :] = shift_s[:, tq:tq + hist, :]

    wob_ref[...] = wo_ref[...].astype(BF16)
    wub_ref[...] = wu_ref[...].astype(BF16)
    wdb_ref[...] = wd_ref[...].astype(BF16)

    h = _rmsnorm(x_ref[0], g1_ref[...]).astype(BF16)

    def proj(lo, n):
        return jnp.dot(h, win_ref[:, lo:lo + n], preferred_element_type=F32)

    off = hist - (cwidth - 1)
    for lb in range(cdim // LANES):
        ls = slice(lb * LANES, (lb + 1) * LANES)
        if lb % GLU_LANE_BLOCKS == 0:
            n = GLU_LANE_BLOCKS * LANES
            cs = slice(lb * LANES, lb * LANES + n)
            glu = proj(lb * LANES, n) * jax.nn.sigmoid(proj(cdim + lb * LANES, n))
            glu_s[hist:hist + tq, cs] = glu
            for ph in range(1, SUBLANES):
                shift_s[ph - 1, hist - ph:hist - ph + tq, cs] = glu
        for r0 in range(0, tq, CONV_ROWS):
            acc = None
            for w in range(cwidth):
                ph = (off + w) % SUBLANES
                a = r0 + (off + w) // SUBLANES * SUBLANES
                xs = glu_s[a:a + CONV_ROWS, ls] if ph == 0 else shift_s[ph - 1, a:a + CONV_ROWS, ls]
                term = xs * cw_ref[w:w + 1, ls]
                acc = term if acc is None else acc + term
            conv_s[r0:r0 + CONV_ROWS, ls] = acc + cvec_ref[0:1, ls]
    for r0 in range(0, tq, CONV_ROWS):
        cn = _layernorm(conv_s[r0:r0 + CONV_ROWS, :]) * cvec_ref[1:2, :] + cvec_ref[2:3, :]
        mix_ref[0, r0:r0 + CONV_ROWS, 0:cdim] = _silu(cn).astype(BF16)

    c0 = 2 * cdim
    qf = proj(c0, rdim)
    kf = proj(c0 + rdim, rdim)
    cos2 = cos_ref[...]
    sin2 = sin_ref[...]
    scale = hd ** -0.5
    for hh in range(heads):
        hs = slice(hh * hd, (hh + 1) * hd)
        q_s[:, hs] = _rotary(qf[:, hs], cos2, sin2).astype(BF16)
        k_s[:, hs] = _rotary(kf[:, hs], cos2, sin2) * scale
    v_s[...] = proj(c0 + 2 * rdim, rdim).astype(BF16)
    g_s[...] = _silu(proj(c0 + 3 * rdim, rdim))

    for hh in range(heads):
        hs = slice(hh * hd, (hh + 1) * hd)
        st = state_s[hh]
        for r0 in range(0, tq, sub):
            rs = slice(r0, r0 + sub)
            qh = q_s[rs, hs]
            kh = k_s[rs, hs]
            vh = v_s[rs, hs]
            kb = kh.astype(BF16)
            parts = []
            for a in range(0, sub, CAUSAL_ROWS):
                e = a + CAUSAL_ROWS
                sc = lax.dot_general(qh[a:e], kb[0:e], (((1,), (1,)), ((), ())),
                                     preferred_element_type=F32) * dec_ref[hh, a:e, 0:e]
                parts.append(jnp.dot(sc.astype(BF16), vh[0:e], preferred_element_type=F32))
            inner = jnp.concatenate(parts, axis=0)
            cross = jnp.dot(qh, st.astype(BF16), preferred_element_type=F32) * rdec_ref[hh]
            kd = (kh * kdec_ref[hh]).astype(BF16)
            upd = lax.dot_general(kd, vh, (((0,), (0,)), ((), ())), preferred_element_type=F32)
            st = st * math.exp(_log_gamma(hh) * sub) + upd
            on = _layernorm(inner + cross)
            mix_ref[0, rs, cdim + hh * hd:cdim + (hh + 1) * hd] = (
                on * gn_ref[0:1, hs] * g_s[rs, hs]).astype(BF16)
        state_s[hh] = st

    @pl.when(c == last)
    def _():
        nconv_ref[0] = glu_s[hist + tq - (cwidth - 1):hist + tq, :]
        nret_ref[0] = state_s[...]


def _prompt_mix(x, g1, w_in, layer, prev, conv_w, cvec, gn_g, cos2, sin2, heads, hd, post_weights):
    B, T, D = x.shape
    depth = post_weights[0].shape[0]
    cwidth, cdim = conv_w.shape
    rdim = heads * hd
    tq = PROMPT_CHUNK
    sub = RET_CHUNK
    hist = -(-(cwidth - 1) // SUBLANES) * SUBLANES
    assert T % tq == 0 and tq >= hist and tq % CONV_ROWS == 0 and tq % sub == 0 and hd == LANES
    assert sub % CAUSAL_ROWS == 0

    dec, rdec, kdec = _decay_tables(sub, heads)
    dec = jnp.asarray(dec, F32)
    rdec = jnp.asarray(np.broadcast_to(rdec, (heads, sub, hd)), F32)
    kdec = jnp.asarray(np.broadcast_to(kdec, (heads, sub, hd)), F32)

    chunks = T // tq
    steps = B * chunks
    for w in post_weights:
        assert w.shape[1] % (steps * 2 * SUBLANES) == 0
    slab_in = [pl.BlockSpec((None, w.shape[1] // steps, w.shape[2]), lambda b, c: (layer, b * chunks + c, 0))
               for w in post_weights]
    slab_out = [pl.BlockSpec((w.shape[1] // steps, w.shape[2]), lambda b, c: (b * chunks + c, 0))
                for w in post_weights]

    args = [x, g1, w_in, conv_w, cvec, gn_g, cos2, sin2, dec, rdec, kdec, *post_weights]
    in_specs = [
        pl.BlockSpec((1, tq, D), lambda b, c: (b, c, 0)),
        _const_spec((1, D)),
        _const_spec(w_in.shape),
        _const_spec(conv_w.shape),
        _const_spec(cvec.shape),
        _const_spec((1, rdim)),
        pl.BlockSpec((tq, hd), lambda b, c: (c, 0)),
        pl.BlockSpec((tq, hd), lambda b, c: (c, 0)),
        _const_spec(dec.shape),
        _const_spec(rdec.shape),
        _const_spec(kdec.shape),
        *slab_in,
    ]
    aliases = {}
    if prev is not None:
        aliases = {len(args): 1, len(args) + 1: 2}
        args += list(prev)
        in_specs += [pl.BlockSpec(memory_space=pl.ANY)] * 2

    kern = functools.partial(_prompt_mix_kernel, tq=tq, sub=sub, heads=heads, hd=hd, cdim=cdim,
                             cwidth=cwidth, hist=hist)
    return pl.pallas_call(
        kern,
        grid=(B, chunks),
        in_specs=in_specs,
        out_specs=[
            pl.BlockSpec((1, tq, D), lambda b, c: (b, c, 0)),
            pl.BlockSpec((None, 1, cwidth - 1, cdim), lambda b, c: (layer, b, 0, 0)),
            pl.BlockSpec((None, 1, heads, hd, hd), lambda b, c: (layer, b, 0, 0, 0)),
            *slab_out,
        ],
        out_shape=[
            jax.ShapeDtypeStruct((B, T, D), BF16),
            jax.ShapeDtypeStruct((depth, B, cwidth - 1, cdim), F32),
            jax.ShapeDtypeStruct((depth, B, heads, hd, hd), F32),
            *[jax.ShapeDtypeStruct(w.shape[1:], BF16) for w in post_weights],
        ],
        input_output_aliases=aliases,
        scratch_shapes=[
            pltpu.VMEM((hist + tq, cdim), F32),
            pltpu.VMEM((SUBLANES - 1, hist + tq, cdim), F32),
            pltpu.VMEM((tq, cdim), F32),
            pltpu.VMEM((tq, rdim), BF16),
            pltpu.VMEM((tq, rdim), F32),
            pltpu.VMEM((tq, rdim), BF16),
            pltpu.VMEM((tq, rdim), F32),
            pltpu.VMEM((heads, hd, hd), F32),
        ],
        compiler_params=pltpu.CompilerParams(
            dimension_semantics=("arbitrary", "arbitrary"), vmem_limit_bytes=VMEM_LIMIT),
        name="prompt_mix",
    )(*args)


def _post_kernel(xa_ref, mixa_ref, xb_ref, mixb_ref, wout_ref, g2_ref, wup_ref, wdn_ref, gf_ref,
                 oa_ref, ob_ref, *, final, steps_a):
    def tile(x_ref, mix_ref, o_ref):
        y = x_ref[...] + jnp.dot(mix_ref[...], wout_ref[...], preferred_element_type=F32)
        h2 = _rmsnorm(y, g2_ref[...]).astype(BF16)
        dff = wup_ref.shape[1]
        hc = dff // MLP_CHUNKS
        out = y
        for k in range(MLP_CHUNKS):
            hf = jnp.dot(h2, wup_ref[:, k * hc:(k + 1) * hc], preferred_element_type=F32)
            act = jnp.square(jnp.maximum(hf, 0.0)).astype(BF16)
            out = out + jnp.dot(act, wdn_ref[k * hc:(k + 1) * hc, :], preferred_element_type=F32)
        if final:
            out = _rmsnorm(out, gf_ref[...])
        o_ref[...] = out

    i = pl.program_id(0)

    @pl.when(i < steps_a)
    def _():
        tile(xa_ref, mixa_ref, oa_ref)

    @pl.when(i >= steps_a)
    def _():
        tile(xb_ref, mixb_ref, ob_ref)


def _post(xa, mixa, xb, mixb, w_out, g2, w_up, w_down, gf, final):
    Ra, D = xa.shape
    Rb = xb.shape[0]
    tra, trb = min(POST_ROWS, Ra), min(POST_ROWS, Rb)
    assert Ra % tra == 0 and Rb % trb == 0
    na, nb = Ra // tra, Rb // trb
    rows_a = pl.BlockSpec((tra, D), lambda i: (jnp.minimum(i, na - 1), 0))
    rows_b = pl.BlockSpec((trb, D), lambda i: (jnp.maximum(i - na, 0), 0))

    return pl.pallas_call(
        functools.partial(_post_kernel, final=final, steps_a=na),
        grid=(na + nb,),
        in_specs=[
            rows_a, rows_a, rows_b, rows_b,
            _const_spec(w_out.shape),
            _const_spec((1, D)),
            _const_spec(w_up.shape),
            _const_spec(w_down.shape),
            _const_spec((1, D)),
        ],
        out_specs=[rows_a, rows_b],
        out_shape=[jax.ShapeDtypeStruct((Ra, D), F32), jax.ShapeDtypeStruct((Rb, D), F32)],
        compiler_params=pltpu.CompilerParams(
            dimension_semantics=("arbitrary",), vmem_limit_bytes=VMEM_LIMIT),
        name="post",
    )(xa, mixa, xb, mixb, w_out, g2, w_up, w_down, gf)


def _in_proj_kernel(x_ref, g1_ref, w_ref, o_ref, wb_ref, h_s):
    @pl.when(pl.program_id(0) == 0)
    def _():
        h_s[...] = _rmsnorm(x_ref[...], g1_ref[...]).astype(BF16)

    wb_ref[...] = w_ref[...].astype(BF16)
    o_ref[...] = jnp.dot(h_s[...], wb_ref[...], preferred_element_type=F32)


def _in_proj(x, g1, w_in, layer, ncol):
    R, D = x.shape
    C = w_in.shape[2]
    return pl.pallas_call(
        _in_proj_kernel,
        grid=(C // ncol,),
        in_specs=[
            _const_spec((R, D)),
            _const_spec((1, D)),
            pl.BlockSpec((None, D, ncol), lambda j: (layer, 0, j)),
        ],
        out_specs=[pl.BlockSpec((R, ncol), lambda j: (0, j)), pl.BlockSpec((D, ncol), lambda j: (0, j))],
        out_shape=[jax.ShapeDtypeStruct((R, C), F32), jax.ShapeDtypeStruct((D, C), BF16)],
        scratch_shapes=[pltpu.VMEM((R, D), BF16)],
        compiler_params=pltpu.CompilerParams(
            dimension_semantics=("arbitrary",), vmem_limit_bytes=VMEM_LIMIT),
        name="sample_in_proj",
    )(x, g1, w_in)


def _sample_mix_kernel(proj_ref, cache_ref, st_ref, cw_ref, cvec_ref, gn_ref, cos_ref, sin_ref,
                       dec_ref, rdec_ref, kdec_ref, *rest, ts, nb, heads, hd, cdim, cwidth):
    mix_ref, ncache_ref, nst_ref = rest[-3:]
    rdim = heads * hd

    glu = [proj_ref[t, :, 0:cdim] * jax.nn.sigmoid(proj_ref[t, :, cdim:2 * cdim]) for t in range(ts)]

    def window(r):
        return cache_ref[r] if r < cwidth - 1 else glu[r - (cwidth - 1)]

    acc = [None] * ts
    for r in range(cwidth - 1 + ts):
        x = window(r)
        for t in range(max(0, r - cwidth + 1), min(ts, r + 1)):
            term = x * cw_ref[r - t:r - t + 1, :]
            acc[t] = term if acc[t] is None else acc[t] + term
    for t in range(ts):
        cn = _layernorm(acc[t] + cvec_ref[0:1, :]) * cvec_ref[1:2, :] + cvec_ref[2:3, :]
        mix_ref[t, :, 0:cdim] = _silu(cn).astype(BF16)
    for r in range(cwidth - 1):
        ncache_ref[r] = window(r + ts)

    c0 = 2 * cdim
    cos2 = cos_ref[...]
    sin2 = sin_ref[...]
    scale = hd ** -0.5
    for hh in range(heads):
        lo = hh * hd

        def per_seq(base):
            return jnp.stack([proj_ref[:, s, base + lo:base + lo + hd] for s in range(nb)], axis=0)

        q = _rotary(per_seq(c0), cos2, sin2)
        k = _rotary(per_seq(c0 + rdim), cos2, sin2) * scale
        v = per_seq(c0 + 2 * rdim)
        g = per_seq(c0 + 3 * rdim)
        st = st_ref[:, hh]
        scores = jnp.einsum('btd,bjd->btj', q, k, preferred_element_type=F32) * dec_ref[hh]
        inner = jnp.einsum('btj,bjv->btv', scores, v, preferred_element_type=F32)
        cross = jnp.einsum('btd,bdv->btv', q, st, preferred_element_type=F32) * rdec_ref[hh]
        upd = jnp.einsum('bjd,bjv->bdv', k * kdec_ref[hh], v, preferred_element_type=F32)
        nst_ref[:, hh] = st * math.exp(_log_gamma(hh) * ts) + upd
        on = _layernorm(inner + cross)
        out = (on * gn_ref[0:1, lo:lo + hd] * _silu(g)).astype(BF16)
        for s in range(nb):
            mix_ref[:, s, cdim + lo:cdim + lo + hd] = out[s]


def _sample_mix(proj, cache, state, layer, prev, conv_w, cvec, gn_g, cos2, sin2):
    ts, S, C = proj.shape
    depth, _, heads, hd, _ = state.shape
    cwidth, cdim = conv_w.shape
    rdim = heads * hd
    D = cdim + rdim
    nb = SAMPLE_SEQS
    assert S % nb == 0 and hd == LANES

    dec, rdec, kdec = (jnp.asarray(t, F32) for t in _decay_tables(ts, heads))

    conv_spec = pl.BlockSpec((None, cwidth - 1, nb, cdim), lambda i: (layer, 0, i, 0))
    st_spec = pl.BlockSpec((None, nb, heads, hd, hd), lambda i: (layer, i, 0, 0, 0))
    args = [proj, cache, state, conv_w, cvec, gn_g, cos2, sin2, dec, rdec, kdec]
    in_specs = [
        pl.BlockSpec((ts, nb, C), lambda i: (0, i, 0)),
        conv_spec,
        st_spec,
        _const_spec(conv_w.shape),
        _const_spec(cvec.shape),
        _const_spec((1, rdim)),
        _const_spec((ts, hd)),
        _const_spec((ts, hd)),
        _const_spec(dec.shape),
        _const_spec(rdec.shape),
        _const_spec(kdec.shape),
    ]
    aliases = {}
    if prev is not None:
        aliases = {len(args): 1, len(args) + 1: 2}
        args += list(prev)
        in_specs += [pl.BlockSpec(memory_space=pl.ANY)] * 2

    kern = functools.partial(_sample_mix_kernel, ts=ts, nb=nb, heads=heads, hd=hd, cdim=cdim, cwidth=cwidth)
    return pl.pallas_call(
        kern,
        grid=(S // nb,),
        in_specs=in_specs,
        out_specs=[pl.BlockSpec((ts, nb, D), lambda i: (0, i, 0)), conv_spec, st_spec],
        out_shape=[
            jax.ShapeDtypeStruct((ts, S, D), BF16),
            jax.ShapeDtypeStruct((depth, cwidth - 1, S, cdim), F32),
            jax.ShapeDtypeStruct((depth, S, heads, hd, hd), F32),
        ],
        input_output_aliases=aliases,
        compiler_params=pltpu.CompilerParams(
            dimension_semantics=("arbitrary",), vmem_limit_bytes=VMEM_LIMIT),
        name="sample_mix",
    )(*args)


def _rope_tables(pos0, T, hd):
    half = hd // 2
    pos = pos0 + np.arange(T, dtype=np.float64)
    inv = ROPE_BASE ** (-np.arange(half, dtype=np.float64) / half)
    ang = pos[:, None] * inv[None, :]
    cos, sin = np.cos(ang), np.sin(ang)
    return (jnp.asarray(np.concatenate([cos, cos], axis=-1), F32),
            jnp.asarray(np.concatenate([-sin, sin], axis=-1), F32))


def kernel(x_prompt, x_sample, cache_conv, state_ret, norm1_g, w_in, conv_w, conv_b, conv_ln_g,
           conv_ln_b, ret_gn_g, w_out, norm2_g, w_up, w_down, final_norm_g):
    depth = w_in.shape[0]
    B, T, D = x_prompt.shape
    S, ts, _ = x_sample.shape
    heads, hd = state_ret.shape[2], state_ret.shape[3]

    cvec = jnp.stack([conv_b, conv_ln_g, conv_ln_b], axis=1)
    gf = final_norm_g[None, :]
    cos_p, sin_p = _rope_tables(0.0, T, hd)
    cos_s, sin_s = _rope_tables(float(PAST_LEN), ts, hd)

    xp = x_prompt.reshape(B * T, D)
    xs = x_sample.transpose(1, 0, 2).reshape(ts * S, D)
    cache_t = cache_conv.transpose(0, 2, 1, 3)
    prompt_state, sample_state = None, None
    for l in range(depth):
        final = l == depth - 1
        g1, g2, gn = norm1_g[l][None, :], norm2_g[l][None, :], ret_gn_g[l][None, :]

        proj, w_in_b = _in_proj(xs, g1, w_in, l, IN_PROJ_COLS)
        mix_s, *sample_state = _sample_mix(proj.reshape(ts, S, -1), cache_t, state_ret, l, sample_state,
                                           conv_w[l], cvec[l], gn, cos_s, sin_s)

        mix_p, *rest = _prompt_mix(xp.reshape(B, T, D), g1, w_in_b, l, prompt_state, conv_w[l], cvec[l], gn,
                                   cos_p, sin_p, heads, hd, (w_out, w_up, w_down))
        prompt_state, post_w = rest[:2], rest[2:]

        w_out_b, w_up_b, w_dn_b = post_w
        xp, xs = _post(xp, mix_p.reshape(B * T, D), xs, mix_s.reshape(ts * S, D), w_out_b, g2,
                       w_up_b, w_dn_b, gf, final)

    new_cache_t, new_state = sample_state
    return (xp.reshape(B, T, D), xs.reshape(ts, S, D).transpose(1, 0, 2), *prompt_state,
            new_cache_t.transpose(0, 2, 1, 3), new_state)
```

```python
import functools
import math

import numpy as np
import jax
import jax.numpy as jnp
from jax import lax
from jax.experimental import pallas as pl
from jax.experimental.pallas import tpu as pltpu

F32 = jnp.float32
BF16 = jnp.bfloat16

EPS = 1e-6
ROPE_BASE = 10000.0
PAST_LEN = 16384
LANES = 128
SUBLANES = 8
VMEM_LIMIT = 62 * 1024 * 1024

PROMPT_CHUNK = 1024
RET_CHUNK = 512
CAUSAL_ROWS = 256
CONV_ROWS = 128
GLU_LANE_BLOCKS = 2
POST_ROWS = 1024
MLP_CHUNKS = 4
SAMPLE_SEQS = 32
IN_PROJ_COLS = 1536


def _log_gamma(h):
    return math.log(1.0 - 2.0 ** (-5.0 - h))


def _rmsnorm(x, g):
    ms = jnp.mean(x * x, axis=-1, keepdims=True)
    return x * lax.rsqrt(ms + EPS) * g


def _layernorm(x):
    mu = jnp.mean(x, axis=-1, keepdims=True)
    xc = x - mu
    var = jnp.mean(xc * xc, axis=-1, keepdims=True)
    return xc * lax.rsqrt(var + EPS)


def _silu(x):
    return x * jax.nn.sigmoid(x)


def _rotary(t, cos2, sin2):
    half = t.shape[-1] // 2
    return t * cos2 + pltpu.roll(t, half, t.ndim - 1) * sin2


def _const_spec(shape):
    nd = len(shape)
    return pl.BlockSpec(shape, lambda *_: (0,) * nd, pipeline_mode=pl.Buffered(1))


def _decay_tables(n, heads):
    idx = np.arange(n, dtype=np.float64)
    diff = idx[:, None] - idx[None, :]
    lg = np.array([_log_gamma(h) for h in range(heads)])
    dec = np.where(diff[None] >= 0, np.exp(lg[:, None, None] * np.maximum(diff, 0.0)[None]), 0.0)
    rdec = np.exp(lg[:, None] * (idx + 1.0))[:, :, None]
    kdec = np.exp(lg[:, None] * (n - 1.0 - idx))[:, :, None]
    return dec, rdec, kdec


def _prompt_mix_kernel(x_ref, g1_ref, win_ref, cw_ref, cvec_ref, gn_ref, cos_ref, sin_ref,
                       dec_ref, rdec_ref, kdec_ref, wo_ref, wu_ref, wd_ref, *rest,
                       tq, sub, heads, hd, cdim, cwidth, hist):
    n_out, n_scratch = 6, 8
    mix_ref, nconv_ref, nret_ref, wob_ref, wub_ref, wdb_ref = rest[-(n_out + n_scratch):-n_scratch]
    glu_s, shift_s, conv_s, q_s, k_s, v_s, g_s, state_s = rest[-n_scratch:]
    c = pl.program_id(1)
    last = pl.num_programs(1) - 1
    rdim = heads * hd

    @pl.when(c == 0)
    def _():
        glu_s[0:hist, :] = jnp.zeros((hist, cdim), F32)
        shift_s[:, 0:hist, :] = jnp.zeros((SUBLANES - 1, hist, cdim), F32)
        shift_s[:, tq:tq + hist, :] = jnp.zeros((SUBLANES - 1, hist, cdim), F32)
        state_s[...] = jnp.zeros_like(state_s)

    @pl.when(c > 0)
    def _():
        glu_s[0:hist, :] = glu_s[tq:tq + hist, :]
        shift_s[:, 0:hist, :] = shift_s[:, tq:tq + hist, :]

    wob_ref[...] = wo_ref[...].astype(BF16)
    wub_ref[...] = wu_ref[...].astype(BF16)
    wdb_ref[...] = wd_ref[...].astype(BF16)

    h = _rmsnorm(x_ref[0], g1_ref[...]).astype(BF16)

    def proj(lo, n):
        return jnp.dot(h, win_ref[:, lo:lo + n], preferred_element_type=F32)

    off = hist - (cwidth - 1)
    for lb in range(cdim // LANES):
        ls = slice(lb * LANES, (lb + 1) * LANES)
        if lb % GLU_LANE_BLOCKS == 0:
            n = GLU_LANE_BLOCKS * LANES
            cs = slice(lb * LANES, lb * LANES + n)
            glu = proj(lb * LANES, n) * jax.nn.sigmoid(proj(cdim + lb * LANES, n))
            glu_s[hist:hist + tq, cs] = glu
            for ph in range(1, SUBLANES):
                shift_s[ph - 1, hist - ph:hist - ph + tq, cs] = glu
        for r0 in range(0, tq, CONV_ROWS):
            acc = None
            for w in range(cwidth):
                ph = (off + w) % SUBLANES
                a = r0 + (off + w) // SUBLANES * SUBLANES
                xs = glu_s[a:a + CONV_ROWS, ls] if ph == 0 else shift_s[ph - 1, a:a + CONV_ROWS, ls]
                term = xs * cw_ref[w:w + 1, ls]
                acc = term if acc is None else acc + term
            conv_s[r0:r0 + CONV_ROWS, ls] = acc + cvec_ref[0:1, ls]
    for r0 in range(0, tq, CONV_ROWS):
        cn = _layernorm(conv_s[r0:r0 + CONV_ROWS, :]) * cvec_ref[1:2, :] + cvec_ref[2:3, :]
        mix_ref[0, r0:r0 + CONV_ROWS, 0:cdim] = _silu(cn).astype(BF16)

    c0 = 2 * cdim
    qf = proj(c0, rdim)
    kf = proj(c0 + rdim, rdim)
    cos2 = cos_ref[...]
    sin2 = sin_ref[...]
    scale = hd ** -0.5
    for hh in range(heads):
        hs = slice(hh * hd, (hh + 1) * hd)
        q_s[:, hs] = _rotary(qf[:, hs], cos2, sin2).astype(BF16)
        k_s[:, hs] = _rotary(kf[:, hs], cos2, sin2) * scale
    v_s[...] = proj(c0 + 2 * rdim, rdim).astype(BF16)
    g_s[...] = _silu(proj(c0 + 3 * rdim, rdim))

    for hh in range(heads):
        hs = slice(hh * hd, (hh + 1) * hd)
        st = state_s[hh]
        for r0 in range(0, tq, sub):
            rs = slice(r0, r0 + sub)
            qh = q_s[rs, hs]
            kh = k_s[rs, hs]
            vh = v_s[rs, hs]
            kb = kh.astype(BF16)
            parts = []
            for a in range(0, sub, CAUSAL_ROWS):
                e = a + CAUSAL_ROWS
                sc = lax.dot_general(qh[a:e], kb[0:e], (((1,), (1,)), ((), ())),
                                     preferred_element_type=F32) * dec_ref[hh, a:e, 0:e]
                parts.append(jnp.dot(sc.astype(BF16), vh[0:e], preferred_element_type=F32))
            inner = jnp.concatenate(parts, axis=0)
            cross = jnp.dot(qh, st.astype(BF16), preferred_element_type=F32) * rdec_ref[hh]
            kd = (kh * kdec_ref[hh]).astype(BF16)
            upd = lax.dot_general(kd, vh, (((0,), (0,)), ((), ())), preferred_element_type=F32)
            st = st * math.exp(_log_gamma(hh) * sub) + upd
            on = _layernorm(inner + cross)
            mix_ref[0, rs, cdim + hh * hd:cdim + (hh + 1) * hd] = (
                on * gn_ref[0:1, hs] * g_s[rs, hs]).astype(BF16)
        state_s[hh] = st

    @pl.when(c == last)
    def _():
        nconv_ref[0] = glu_s[hist + tq - (cwidth - 1):hist + tq, :]
        nret_ref[0] = state_s[...]


def _prompt_mix(x, g1, w_in, layer, prev, conv_w, cvec, gn_g, cos2, sin2, heads, hd, post_weights):
    B, T, D = x.shape
    depth = post_weights[0].shape[0]
    cwidth, cdim = conv_w.shape
    rdim = heads * hd
    tq = PROMPT_CHUNK
    sub = RET_CHUNK
    hist = -(-(cwidth - 1) // SUBLANES) * SUBLANES
    assert T % tq == 0 and tq >= hist and tq % CONV_ROWS == 0 and tq % sub == 0 and hd == LANES
    assert sub % CAUSAL_ROWS == 0

    dec, rdec, kdec = _decay_tables(sub, heads)
    dec = jnp.asarray(dec, F32)
    rdec = jnp.asarray(np.broadcast_to(rdec, (heads, sub, hd)), F32)
    kdec = jnp.asarray(np.broadcast_to(kdec, (heads, sub, hd)), F32)

    chunks = T // tq
    steps = B * chunks
    for w in post_weights:
        assert w.shape[1] % (steps * 2 * SUBLANES) == 0
    slab_in = [pl.BlockSpec((None, w.shape[1] // steps, w.shape[2]), lambda b, c: (layer, b * chunks + c, 0))
               for w in post_weights]
    slab_out = [pl.BlockSpec((w.shape[1] // steps, w.shape[2]), lambda b, c: (b * chunks + c, 0))
                for w in post_weights]

    args = [x, g1, w_in, conv_w, cvec, gn_g, cos2, sin2, dec, rdec, kdec, *post_weights]
    in_specs = [
        pl.BlockSpec((1, tq, D), lambda b, c: (b, c, 0)),
        _const_spec((1, D)),
        _const_spec(w_in.shape),
        _const_spec(conv_w.shape),
        _const_spec(cvec.shape),
        _const_spec((1, rdim)),
        pl.BlockSpec((tq, hd), lambda b, c: (c, 0)),
        pl.BlockSpec((tq, hd), lambda b, c: (c, 0)),
        _const_spec(dec.shape),
        _const_spec(rdec.shape),
        _const_spec(kdec.shape),
        *slab_in,
    ]
    aliases = {}
    if prev is not None:
        aliases = {len(args): 1, len(args) + 1: 2}
        args += list(prev)
        in_specs += [pl.BlockSpec(memory_space=pl.ANY)] * 2

    kern = functools.partial(_prompt_mix_kernel, tq=tq, sub=sub, heads=heads, hd=hd, cdim=cdim,
                             cwidth=cwidth, hist=hist)
    return pl.pallas_call(
        kern,
        grid=(B, chunks),
        in_specs=in_specs,
        out_specs=[
            pl.BlockSpec((1, tq, D), lambda b, c: (b, c, 0)),
            pl.BlockSpec((None, 1, cwidth - 1, cdim), lambda b, c: (layer, b, 0, 0)),
            pl.BlockSpec((None, 1, heads, hd, hd), lambda b, c: (layer, b, 0, 0, 0)),
            *slab_out,
        ],
        out_shape=[
            jax.ShapeDtypeStruct((B, T, D), BF16),
            jax.ShapeDtypeStruct((depth, B, cwidth - 1, cdim), F32),
            jax.ShapeDtypeStruct((depth, B, heads, hd, hd), F32),
            *[jax.ShapeDtypeStruct(w.shape[1:], BF16) for w in post_weights],
        ],
        input_output_aliases=aliases,
        scratch_shapes=[
            pltpu.VMEM((hist + tq, cdim), F32),
            pltpu.VMEM((SUBLANES - 1, hist + tq, cdim), F32),
            pltpu.VMEM((tq, cdim), F32),
            pltpu.VMEM((tq, rdim), BF16),
            pltpu.VMEM((tq, rdim), F32),
            pltpu.VMEM((tq, rdim), BF16),
            pltpu.VMEM((tq, rdim), F32),
            pltpu.VMEM((heads, hd, hd), F32),
        ],
        compiler_params=pltpu.CompilerParams(
            dimension_semantics=("arbitrary", "arbitrary"), vmem_limit_bytes=VMEM_LIMIT),
        name="prompt_mix",
    )(*args)


def _post_kernel(xa_ref, mixa_ref, xb_ref, mixb_ref, wout_ref, g2_ref, wup_ref, wdn_ref, gf_ref,
                 oa_ref, ob_ref, *, final, steps_a):
    def tile(x_ref, mix_ref, o_ref):
        y = x_ref[...] + jnp.dot(mix_ref[...], wout_ref[...], preferred_element_type=F32)
        h2 = _rmsnorm(y, g2_ref[...]).astype(BF16)
        dff = wup_ref.shape[1]
        hc = dff // MLP_CHUNKS
        out = y
        for k in range(MLP_CHUNKS):
            hf = jnp.dot(h2, wup_ref[:, k * hc:(k + 1) * hc], preferred_element_type=F32)
            act = jnp.square(jnp.maximum(hf, 0.0)).astype(BF16)
            out = out + jnp.dot(act, wdn_ref[k * hc:(k + 1) * hc, :], preferred_element_type=F32)
        if final:
            out = _rmsnorm(out, gf_ref[...])
        o_ref[...] = out

    i = pl.program_id(0)

    @pl.when(i < steps_a)
    def _():
        tile(xa_ref, mixa_ref, oa_ref)

    @pl.when(i >= steps_a)
    def _():
        tile(xb_ref, mixb_ref, ob_ref)


def _post(xa, mixa, xb, mixb, w_out, g2, w_up, w_down, gf, final):
    Ra, D = xa.shape
    Rb = xb.shape[0]
    tra, trb = min(POST_ROWS, Ra), min(POST_ROWS, Rb)
    assert Ra % tra == 0 and Rb % trb == 0
    na, nb = Ra // tra, Rb // trb
    rows_a = pl.BlockSpec((tra, D), lambda i: (jnp.minimum(i, na - 1), 0))
    rows_b = pl.BlockSpec((trb, D), lambda i: (jnp.maximum(i - na, 0), 0))

    return pl.pallas_call(
        functools.partial(_post_kernel, final=final, steps_a=na),
        grid=(na + nb,),
        in_specs=[
            rows_a, rows_a, rows_b, rows_b,
            _const_spec(w_out.shape),
            _const_spec((1, D)),
            _const_spec(w_up.shape),
            _const_spec(w_down.shape),
            _const_spec((1, D)),
        ],
        out_specs=[rows_a, rows_b],
        out_shape=[jax.ShapeDtypeStruct((Ra, D), F32), jax.ShapeDtypeStruct((Rb, D), F32)],
        compiler_params=pltpu.CompilerParams(
            dimension_semantics=("arbitrary",), vmem_limit_bytes=VMEM_LIMIT),
        name="post",
    )(xa, mixa, xb, mixb, w_out, g2, w_up, w_down, gf)


def _in_proj_kernel(x_ref, g1_ref, w_ref, o_ref, wb_ref, h_s):
    @pl.when(pl.program_id(0) == 0)
    def _():
        h_s[...] = _rmsnorm(x_ref[...], g1_ref[...]).astype(BF16)

    wb_ref[...] = w_ref[...].astype(BF16)
    o_ref[...] = jnp.dot(h_s[...], wb_ref[...], preferred_element_type=F32)


def _in_proj(x, g1, w_in, layer, ncol):
    R, D = x.shape
    C = w_in.shape[2]
    return pl.pallas_call(
        _in_proj_kernel,
        grid=(C // ncol,),
        in_specs=[
            _const_spec((R, D)),
            _const_spec((1, D)),
            pl.BlockSpec((None, D, ncol), lambda j: (layer, 0, j)),
        ],
        out_specs=[pl.BlockSpec((R, ncol), lambda j: (0, j)), pl.BlockSpec((D, ncol), lambda j: (0, j))],
        out_shape=[jax.ShapeDtypeStruct((R, C), F32), jax.ShapeDtypeStruct((D, C), BF16)],
        scratch_shapes=[pltpu.VMEM((R, D), BF16)],
        compiler_params=pltpu.CompilerParams(
            dimension_semantics=("arbitrary",), vmem_limit_bytes=VMEM_LIMIT),
        name="sample_in_proj",
    )(x, g1, w_in)


def _sample_mix_kernel(proj_ref, cache_ref, st_ref, cw_ref, cvec_ref, gn_ref, cos_ref, sin_ref,
                       dec_ref, rdec_ref, kdec_ref, *rest, ts, nb, heads, hd, cdim, cwidth):
    mix_ref, ncache_ref, nst_ref = rest[-3:]
    rdim = heads * hd

    glu = [proj_ref[t, :, 0:cdim] * jax.nn.sigmoid(proj_ref[t, :, cdim:2 * cdim]) for t in range(ts)]

    def window(r):
        return cache_ref[r] if r < cwidth - 1 else glu[r - (cwidth - 1)]

    acc = [None] * ts
    for r in range(cwidth - 1 + ts):
        x = window(r)
        for t in range(max(0, r - cwidth + 1), min(ts, r + 1)):
            term = x * cw_ref[r - t:r - t + 1, :]
            acc[t] = term if acc[t] is None else acc[t] + term
    for t in range(ts):
        cn = _layernorm(acc[t] + cvec_ref[0:1, :]) * cvec_ref[1:2, :] + cvec_ref[2:3, :]
        mix_ref[t, :, 0:cdim] = _silu(cn).astype(BF16)
    for r in range(cwidth - 1):
        ncache_ref[r] = window(r + ts)

    c0 = 2 * cdim
    cos2 = cos_ref[...]
    sin2 = sin_ref[...]
    scale = hd ** -0.5
    for hh in range(heads):
        lo = hh * hd

        def per_seq(base):
            return jnp.stack([proj_ref[:, s, base + lo:base + lo + hd] for s in range(nb)], axis=0)

        q = _rotary(per_seq(c0), cos2, sin2)
        k = _rotary(per_seq(c0 + rdim), cos2, sin2) * scale
        v = per_seq(c0 + 2 * rdim)
        g = per_seq(c0 + 3 * rdim)
        st = st_ref[:, hh]
        scores = jnp.einsum('btd,bjd->btj', q, k, preferred_element_type=F32) * dec_ref[hh]
        inner = jnp.einsum('btj,bjv->btv', scores, v, preferred_element_type=F32)
        cross = jnp.einsum('btd,bdv->btv', q, st, preferred_element_type=F32) * rdec_ref[hh]
        upd = jnp.einsum('bjd,bjv->bdv', k * kdec_ref[hh], v, preferred_element_type=F32)
        nst_ref[:, hh] = st * math.exp(_log_gamma(hh) * ts) + upd
        on = _layernorm(inner + cross)
        out = (on * gn_ref[0:1, lo:lo + hd] * _silu(g)).astype(BF16)
        for s in range(nb):
            mix_ref[:, s, cdim + lo:cdim + lo + hd] = out[s]


def _sample_mix(proj, cache, state, layer, prev, conv_w, cvec, gn_g, cos2, sin2):
    ts, S, C = proj.shape
    depth, _, heads, hd, _ = state.shape
    cwidth, cdim = conv_w.shape
    rdim = heads * hd
    D = cdim + rdim
    nb = SAMPLE_SEQS
    assert S % nb == 0 and hd == LANES

    dec, rdec, kdec = (jnp.asarray(t, F32) for t in _decay_tables(ts, heads))

    conv_spec = pl.BlockSpec((None, cwidth - 1, nb, cdim), lambda i: (layer, 0, i, 0))
    st_spec = pl.BlockSpec((None, nb, heads, hd, hd), lambda i: (layer, i, 0, 0, 0))
    args = [proj, cache, state, conv_w, cvec, gn_g, cos2, sin2, dec, rdec, kdec]
    in_specs = [
        pl.BlockSpec((ts, nb, C), lambda i: (0, i, 0)),
        conv_spec,
        st_spec,
        _const_spec(conv_w.shape),
        _const_spec(cvec.shape),
        _const_spec((1, rdim)),
        _const_spec((ts, hd)),
        _const_spec((ts, hd)),
        _const_spec(dec.shape),
        _const_spec(rdec.shape),
        _const_spec(kdec.shape),
    ]
    aliases = {}
    if prev is not None:
        aliases = {len(args): 1, len(args) + 1: 2}
        args += list(prev)
        in_specs += [pl.BlockSpec(memory_space=pl.ANY)] * 2

    kern = functools.partial(_sample_mix_kernel, ts=ts, nb=nb, heads=heads, hd=hd, cdim=cdim, cwidth=cwidth)
    return pl.pallas_call(
        kern,
        grid=(S // nb,),
        in_specs=in_specs,
        out_specs=[pl.BlockSpec((ts, nb, D), lambda i: (0, i, 0)), conv_spec, st_spec],
        out_shape=[
            jax.ShapeDtypeStruct((ts, S, D), BF16),
            jax.ShapeDtypeStruct((depth, cwidth - 1, S, cdim), F32),
            jax.ShapeDtypeStruct((depth, S, heads, hd, hd), F32),
        ],
        input_output_aliases=aliases,
        compiler_params=pltpu.CompilerParams(
            dimension_semantics=("arbitrary",), vmem_limit_bytes=VMEM_LIMIT),
        name="sample_mix",
    )(*args)


def _rope_tables(pos0, T, hd):
    half = hd // 2
    pos = pos0 + np.arange(T, dtype=np.float64)
    inv = ROPE_BASE ** (-np.arange(half, dtype=np.float64) / half)
    ang = pos[:, None] * inv[None, :]
    cos, sin = np.cos(ang), np.sin(ang)
    return (jnp.asarray(np.concatenate([cos, cos], axis=-1), F32),
            jnp.asarray(np.concatenate([-sin, sin], axis=-1), F32))


def kernel(x_prompt, x_sample, cache_conv, state_ret, norm1_g, w_in, conv_w, conv_b, conv_ln_g,
           conv_ln_b, ret_gn_g, w_out, norm2_g, w_up, w_down, final_norm_g):
    depth = w_in.shape[0]
    B, T, D = x_prompt.shape
    S, ts, _ = x_sample.shape
    heads, hd = state_ret.shape[2], state_ret.shape[3]

    cvec = jnp.stack([conv_b, conv_ln_g, conv_ln_b], axis=1)
    gf = final_norm_g[None, :]
    cos_p, sin_p = _rope_tables(0.0, T, hd)
    cos_s, sin_s = _rope_tables(float(PAST_LEN), ts, hd)

    xp = x_prompt.reshape(B * T, D)
    xs = x_sample.transpose(1, 0, 2).reshape(ts * S, D)
    cache_t = cache_conv.transpose(0, 2, 1, 3)
    prompt_state, sample_state = None, None
    for l in range(depth):
        final = l == depth - 1
        g1, g2, gn = norm1_g[l][None, :], norm2_g[l][None, :], ret_gn_g[l][None, :]

        proj, w_in_b = _in_proj(xs, g1, w_in, l, IN_PROJ_COLS)
        mix_s, *sample_state = _sample_mix(proj.reshape(ts, S, -1), cache_t, state_ret, l, sample_state,
                                           conv_w[l], cvec[l], gn, cos_s, sin_s)

        mix_p, *rest = _prompt_mix(xp.reshape(B, T, D), g1, w_in_b, l, prompt_state, conv_w[l], cvec[l], gn,
                                   cos_p, sin_p, heads, hd, (w_out, w_up, w_down))
        prompt_state, post_w = rest[:2], rest[2:]

        w_out_b, w_up_b, w_dn_b = post_w
        xp, xs = _post(xp, mix_p.reshape(B * T, D), xs, mix_s.reshape(ts * S, D), w_out_b, g2,
                       w_up_b, w_dn_b, gf, final)

    new_cache_t, new_state = sample_state
    return (xp.reshape(B, T, D), xs.reshape(ts, S, D).transpose(1, 0, 2), *prompt_state,
            new_cache_t.transpose(0, 2, 1, 3), new_state)
```

```python
import functools
import math

import numpy as np
import jax
import jax.numpy as jnp
from jax import lax
from jax.experimental import pallas as pl
from jax.experimental.pallas import tpu as pltpu

F32 = jnp.float32
BF16 = jnp.bfloat16

EPS = 1e-6
ROPE_BASE = 10000.0
PAST_LEN = 16384
LANES = 128
SUBLANES = 8
VMEM_LIMIT = 62 * 1024 * 1024

PROMPT_CHUNK = 1024
RET_CHUNK = 512
CAUSAL_ROWS = 256
CONV_ROWS = 128
GLU_LANE_BLOCKS = 2
POST_ROWS = 1024
MLP_CHUNKS = 4
SAMPLE_SEQS = 16
STATE_BUFS = 3
IN_PROJ_COLS = 1536


def _log_gamma(h):
    return math.log(1.0 - 2.0 ** (-5.0 - h))


def _rmsnorm(x, g):
    ms = jnp.mean(x * x, axis=-1, keepdims=True)
    return x * lax.rsqrt(ms + EPS) * g


def _layernorm(x):
    mu = jnp.mean(x, axis=-1, keepdims=True)
    xc = x - mu
    var = jnp.mean(xc * xc, axis=-1, keepdims=True)
    return xc * lax.rsqrt(var + EPS)


def _silu(x):
    return x * jax.nn.sigmoid(x)


def _rotary(t, cos2, sin2):
    half = t.shape[-1] // 2
    return t * cos2 + pltpu.roll(t, half, t.ndim - 1) * sin2


def _const_spec(shape):
    nd = len(shape)
    return pl.BlockSpec(shape, lambda *_: (0,) * nd, pipeline_mode=pl.Buffered(1))


def _decay_tables(n, heads):
    idx = np.arange(n, dtype=np.float64)
    diff = idx[:, None] - idx[None, :]
    lg = np.array([_log_gamma(h) for h in range(heads)])
    dec = np.where(diff[None] >= 0, np.exp(lg[:, None, None] * np.maximum(diff, 0.0)[None]), 0.0)
    rdec = np.exp(lg[:, None] * (idx + 1.0))[:, :, None]
    kdec = np.exp(lg[:, None] * (n - 1.0 - idx))[:, :, None]
    return dec, rdec, kdec


def _prompt_mix_kernel(x_ref, g1_ref, win_ref, cw_ref, cvec_ref, gn_ref, cos_ref, sin_ref,
                       dec_ref, rdec_ref, kdec_ref, wo_ref, wu_ref, wd_ref, *rest,
                       tq, sub, heads, hd, cdim, cwidth, hist):
    n_out, n_scratch = 6, 8
    mix_ref, nconv_ref, nret_ref, wob_ref, wub_ref, wdb_ref = rest[-(n_out + n_scratch):-n_scratch]
    glu_s, shift_s, conv_s, q_s, k_s, v_s, g_s, state_s = rest[-n_scratch:]
    c = pl.program_id(1)
    last = pl.num_programs(1) - 1
    rdim = heads * hd

    @pl.when(c == 0)
    def _():
        glu_s[0:hist, :] = jnp.zeros((hist, cdim), F32)
        shift_s[:, 0:hist, :] = jnp.zeros((SUBLANES - 1, hist, cdim), F32)
        shift_s[:, tq:tq + hist, :] = jnp.zeros((SUBLANES - 1, hist, cdim), F32)
        state_s[...] = jnp.zeros_like(state_s)

    @pl.when(c > 0)
    def _():
        glu_s[0:hist, :] = glu_s[tq:tq + hist, :]
        shift_s[:, 0:hist, :] = shift_s[:, tq:tq + hist, :]

    wob_ref[...] = wo_ref[...].astype(BF16)
    wub_ref[...] = wu_ref[...].astype(BF16)
    wdb_ref[...] = wd_ref[...].astype(BF16)

    h = _rmsnorm(x_ref[0], g1_ref[...]).astype(BF16)

    def proj(lo, n):
        return jnp.dot(h, win_ref[:, lo:lo + n], preferred_element_type=F32)

    off = hist - (cwidth - 1)
    for lb in range(cdim // LANES):
        ls = slice(lb * LANES, (lb + 1) * LANES)
        if lb % GLU_LANE_BLOCKS == 0:
            n = GLU_LANE_BLOCKS * LANES
            cs = slice(lb * LANES, lb * LANES + n)
            glu = proj(lb * LANES, n) * jax.nn.sigmoid(proj(cdim + lb * LANES, n))
            glu_s[hist:hist + tq, cs] = glu
            for ph in range(1, SUBLANES):
                shift_s[ph - 1, hist - ph:hist - ph + tq, cs] = glu
        for r0 in range(0, tq, CONV_ROWS):
            acc = None
            for w in range(cwidth):
                ph = (off + w) % SUBLANES
                a = r0 + (off + w) // SUBLANES * SUBLANES
                xs = glu_s[a:a + CONV_ROWS, ls] if ph == 0 else shift_s[ph - 1, a:a + CONV_ROWS, ls]
                term = xs * cw_ref[w:w + 1, ls]
                acc = term if acc is None else acc + term
            conv_s[r0:r0 + CONV_ROWS, ls] = acc + cvec_ref[0:1, ls]
    for r0 in range(0, tq, CONV_ROWS):
        cn = _layernorm(conv_s[r0:r0 + CONV_ROWS, :]) * cvec_ref[1:2, :] + cvec_ref[2:3, :]
        mix_ref[0, r0:r0 + CONV_ROWS, 0:cdim] = _silu(cn).astype(BF16)

    c0 = 2 * cdim
    qf = proj(c0, rdim)
    kf = proj(c0 + rdim, rdim)
    cos2 = cos_ref[...]
    sin2 = sin_ref[...]
    scale = hd ** -0.5
    for hh in range(heads):
        hs = slice(hh * hd, (hh + 1) * hd)
        q_s[:, hs] = _rotary(qf[:, hs], cos2, sin2).astype(BF16)
        k_s[:, hs] = _rotary(kf[:, hs], cos2, sin2) * scale
    v_s[...] = proj(c0 + 2 * rdim, rdim).astype(BF16)
    g_s[...] = _silu(proj(c0 + 3 * rdim, rdim))

    for hh in range(heads):
        hs = slice(hh * hd, (hh + 1) * hd)
        st = state_s[hh]
        for r0 in range(0, tq, sub):
            rs = slice(r0, r0 + sub)
            qh = q_s[rs, hs]
            kh = k_s[rs, hs]
            vh = v_s[rs, hs]
            kb = kh.astype(BF16)
            parts = []
            for a in range(0, sub, CAUSAL_ROWS):
                e = a + CAUSAL_ROWS
                sc = lax.dot_general(qh[a:e], kb[0:e], (((1,), (1,)), ((), ())),
                                     preferred_element_type=F32) * dec_ref[hh, a:e, 0:e]
                parts.append(jnp.dot(sc.astype(BF16), vh[0:e], preferred_element_type=F32))
            inner = jnp.concatenate(parts, axis=0)
            cross = jnp.dot(qh, st.astype(BF16), preferred_element_type=F32) * rdec_ref[hh]
            kd = (kh * kdec_ref[hh]).astype(BF16)
            upd = lax.dot_general(kd, vh, (((0,), (0,)), ((), ())), preferred_element_type=F32)
            st = st * math.exp(_log_gamma(hh) * sub) + upd
            on = _layernorm(inner + cross)
            mix_ref[0, rs, cdim + hh * hd:cdim + (hh + 1) * hd] = (
                on * gn_ref[0:1, hs] * g_s[rs, hs]).astype(BF16)
        state_s[hh] = st

    @pl.when(c == last)
    def _():
        nconv_ref[0] = glu_s[hist + tq - (cwidth - 1):hist + tq, :]
        nret_ref[0] = state_s[...]


def _prompt_mix(x, g1, w_in, layer, prev, conv_w, cvec, gn_g, cos2, sin2, heads, hd, post_weights):
    B, T, D = x.shape
    depth = post_weights[0].shape[0]
    cwidth, cdim = conv_w.shape
    rdim = heads * hd
    tq = PROMPT_CHUNK
    sub = RET_CHUNK
    hist = -(-(cwidth - 1) // SUBLANES) * SUBLANES
    assert T % tq == 0 and tq >= hist and tq % CONV_ROWS == 0 and tq % sub == 0 and hd == LANES
    assert sub % CAUSAL_ROWS == 0

    dec, rdec, kdec = _decay_tables(sub, heads)
    dec = jnp.asarray(dec, F32)
    rdec = jnp.asarray(np.broadcast_to(rdec, (heads, sub, hd)), F32)
    kdec = jnp.asarray(np.broadcast_to(kdec, (heads, sub, hd)), F32)

    chunks = T // tq
    steps = B * chunks
    for w in post_weights:
        assert w.shape[1] % (steps * 2 * SUBLANES) == 0
    slab_in = [pl.BlockSpec((None, w.shape[1] // steps, w.shape[2]), lambda b, c: (layer, b * chunks + c, 0))
               for w in post_weights]
    slab_out = [pl.BlockSpec((w.shape[1] // steps, w.shape[2]), lambda b, c: (b * chunks + c, 0))
                for w in post_weights]

    args = [x, g1, w_in, conv_w, cvec, gn_g, cos2, sin2, dec, rdec, kdec, *post_weights]
    in_specs = [
        pl.BlockSpec((1, tq, D), lambda b, c: (b, c, 0)),
        _const_spec((1, D)),
        _const_spec(w_in.shape),
        _const_spec(conv_w.shape),
        _const_spec(cvec.shape),
        _const_spec((1, rdim)),
        pl.BlockSpec((tq, hd), lambda b, c: (c, 0)),
        pl.BlockSpec((tq, hd), lambda b, c: (c, 0)),
        _const_spec(dec.shape),
        _const_spec(rdec.shape),
        _const_spec(kdec.shape),
        *slab_in,
    ]
    aliases = {}
    if prev is not None:
        aliases = {len(args): 1, len(args) + 1: 2}
        args += list(prev)
        in_specs += [pl.BlockSpec(memory_space=pl.ANY)] * 2

    kern = functools.partial(_prompt_mix_kernel, tq=tq, sub=sub, heads=heads, hd=hd, cdim=cdim,
                             cwidth=cwidth, hist=hist)
    return pl.pallas_call(
        kern,
        grid=(B, chunks),
        in_specs=in_specs,
        out_specs=[
            pl.BlockSpec((1, tq, D), lambda b, c: (b, c, 0)),
            pl.BlockSpec((None, 1, cwidth - 1, cdim), lambda b, c: (layer, b, 0, 0)),
            pl.BlockSpec((None, 1, heads, hd, hd), lambda b, c: (layer, b, 0, 0, 0)),
            *slab_out,
        ],
        out_shape=[
            jax.ShapeDtypeStruct((B, T, D), BF16),
            jax.ShapeDtypeStruct((depth, B, cwidth - 1, cdim), F32),
            jax.ShapeDtypeStruct((depth, B, heads, hd, hd), F32),
            *[jax.ShapeDtypeStruct(w.shape[1:], BF16) for w in post_weights],
        ],
        input_output_aliases=aliases,
        scratch_shapes=[
            pltpu.VMEM((hist + tq, cdim), F32),
            pltpu.VMEM((SUBLANES - 1, hist + tq, cdim), F32),
            pltpu.VMEM((tq, cdim), F32),
            pltpu.VMEM((tq, rdim), BF16),
            pltpu.VMEM((tq, rdim), F32),
            pltpu.VMEM((tq, rdim), BF16),
            pltpu.VMEM((tq, rdim), F32),
            pltpu.VMEM((heads, hd, hd), F32),
        ],
        compiler_params=pltpu.CompilerParams(
            dimension_semantics=("arbitrary", "arbitrary"), vmem_limit_bytes=VMEM_LIMIT),
        name="prompt_mix",
    )(*args)


def _post_kernel(xa_ref, mixa_ref, xb_ref, mixb_ref, wout_ref, g2_ref, wup_ref, wdn_ref, gf_ref,
                 oa_ref, ob_ref, *, final, steps_a):
    def tile(x_ref, mix_ref, o_ref):
        y = x_ref[...] + jnp.dot(mix_ref[...], wout_ref[...], preferred_element_type=F32)
        h2 = _rmsnorm(y, g2_ref[...]).astype(BF16)
        dff = wup_ref.shape[1]
        hc = dff // MLP_CHUNKS
        out = y
        for k in range(MLP_CHUNKS):
            hf = jnp.dot(h2, wup_ref[:, k * hc:(k + 1) * hc], preferred_element_type=F32)
            act = jnp.square(jnp.maximum(hf, 0.0)).astype(BF16)
            out = out + jnp.dot(act, wdn_ref[k * hc:(k + 1) * hc, :], preferred_element_type=F32)
        if final:
            out = _rmsnorm(out, gf_ref[...])
        o_ref[...] = out

    i = pl.program_id(0)

    @pl.when(i < steps_a)
    def _():
        tile(xa_ref, mixa_ref, oa_ref)

    @pl.when(i >= steps_a)
    def _():
        tile(xb_ref, mixb_ref, ob_ref)


def _post(xa, mixa, xb, mixb, w_out, g2, w_up, w_down, gf, final):
    Ra, D = xa.shape
    Rb = xb.shape[0]
    tra, trb = min(POST_ROWS, Ra), min(POST_ROWS, Rb)
    assert Ra % tra == 0 and Rb % trb == 0
    na, nb = Ra // tra, Rb // trb
    rows_a = pl.BlockSpec((tra, D), lambda i: (jnp.minimum(i, na - 1), 0))
    rows_b = pl.BlockSpec((trb, D), lambda i: (jnp.maximum(i - na, 0), 0))

    return pl.pallas_call(
        functools.partial(_post_kernel, final=final, steps_a=na),
        grid=(na + nb,),
        in_specs=[
            rows_a, rows_a, rows_b, rows_b,
            _const_spec(w_out.shape),
            _const_spec((1, D)),
            _const_spec(w_up.shape),
            _const_spec(w_down.shape),
            _const_spec((1, D)),
        ],
        out_specs=[rows_a, rows_b],
        out_shape=[jax.ShapeDtypeStruct((Ra, D), F32), jax.ShapeDtypeStruct((Rb, D), F32)],
        compiler_params=pltpu.CompilerParams(
            dimension_semantics=("arbitrary",), vmem_limit_bytes=VMEM_LIMIT),
        name="post",
    )(xa, mixa, xb, mixb, w_out, g2, w_up, w_down, gf)


def _in_proj_kernel(x_ref, g1_ref, w_ref, o_ref, wb_ref, h_s):
    @pl.when(pl.program_id(0) == 0)
    def _():
        h_s[...] = _rmsnorm(x_ref[...], g1_ref[...]).astype(BF16)

    wb_ref[...] = w_ref[...].astype(BF16)
    o_ref[...] = jnp.dot(h_s[...], wb_ref[...], preferred_element_type=F32)


def _in_proj(x, g1, w_in, layer, ncol):
    R, D = x.shape
    C = w_in.shape[2]
    return pl.pallas_call(
        _in_proj_kernel,
        grid=(C // ncol,),
        in_specs=[
            _const_spec((R, D)),
            _const_spec((1, D)),
            pl.BlockSpec((None, D, ncol), lambda j: (layer, 0, j)),
        ],
        out_specs=[pl.BlockSpec((R, ncol), lambda j: (0, j)), pl.BlockSpec((D, ncol), lambda j: (0, j))],
        out_shape=[jax.ShapeDtypeStruct((R, C), F32), jax.ShapeDtypeStruct((D, C), BF16)],
        scratch_shapes=[pltpu.VMEM((R, D), BF16)],
        compiler_params=pltpu.CompilerParams(
            dimension_semantics=("arbitrary",), vmem_limit_bytes=VMEM_LIMIT),
        name="sample_in_proj",
    )(x, g1, w_in)


def _sample_mix_kernel(proj_ref, cache_ref, st_ref, cw_ref, cvec_ref, gn_ref, cos_ref, sin_ref,
                       dec_ref, rdec_ref, kdec_ref, *rest, layer, ts, nb, heads, hd, cdim, cwidth):
    mix_ref, ncache_ref, nst_ref, st_buf, st_sem = rest[-5:]
    rdim = heads * hd
    i = pl.program_id(0)
    steps = pl.num_programs(0)

    def fetch(step):
        slot = step % STATE_BUFS
        return pltpu.make_async_copy(st_ref.at[layer, pl.ds(step * nb, nb)], st_buf.at[slot], st_sem.at[slot])

    @pl.when(i == 0)
    def _():
        for s in range(STATE_BUFS - 1):
            fetch(s).start()

    @pl.when(i + (STATE_BUFS - 1) < steps)
    def _():
        fetch(i + (STATE_BUFS - 1)).start()

    fetch(i).wait()
    st_cur = st_buf.at[i % STATE_BUFS]

    glu = [proj_ref[t, :, 0:cdim] * jax.nn.sigmoid(proj_ref[t, :, cdim:2 * cdim]) for t in range(ts)]

    def window(r):
        return cache_ref[r] if r < cwidth - 1 else glu[r - (cwidth - 1)]

    acc = [None] * ts
    for r in range(cwidth - 1 + ts):
        x = window(r)
        for t in range(max(0, r - cwidth + 1), min(ts, r + 1)):
            term = x * cw_ref[r - t:r - t + 1, :]
            acc[t] = term if acc[t] is None else acc[t] + term
    for t in range(ts):
        cn = _layernorm(acc[t] + cvec_ref[0:1, :]) * cvec_ref[1:2, :] + cvec_ref[2:3, :]
        mix_ref[t, :, 0:cdim] = _silu(cn).astype(BF16)
    for r in range(cwidth - 1):
        ncache_ref[r] = window(r + ts)

    c0 = 2 * cdim
    cos2 = cos_ref[...]
    sin2 = sin_ref[...]
    scale = hd ** -0.5
    for hh in range(heads):
        lo = hh * hd

        def per_seq(base):
            return jnp.stack([proj_ref[:, s, base + lo:base + lo + hd] for s in range(nb)], axis=0)

        q = _rotary(per_seq(c0), cos2, sin2)
        k = _rotary(per_seq(c0 + rdim), cos2, sin2) * scale
        v = per_seq(c0 + 2 * rdim)
        g = per_seq(c0 + 3 * rdim)
        st = st_cur[:, hh]
        scores = jnp.einsum('btd,bjd->btj', q, k, preferred_element_type=F32) * dec_ref[hh]
        inner = jnp.einsum('btj,bjv->btv', scores, v, preferred_element_type=F32)
        cross = jnp.einsum('btd,bdv->btv', q, st, preferred_element_type=F32) * rdec_ref[hh]
        upd = jnp.einsum('bjd,bjv->bdv', k * kdec_ref[hh], v, preferred_element_type=F32)
        nst_ref[:, hh] = st * math.exp(_log_gamma(hh) * ts) + upd
        on = _layernorm(inner + cross)
        out = (on * gn_ref[0:1, lo:lo + hd] * _silu(g)).astype(BF16)
        for s in range(nb):
            mix_ref[:, s, cdim + lo:cdim + lo + hd] = out[s]


def _sample_mix(proj, cache, state, layer, prev, conv_w, cvec, gn_g, cos2, sin2):
    ts, S, C = proj.shape
    depth, _, heads, hd, _ = state.shape
    cwidth, cdim = conv_w.shape
    rdim = heads * hd
    D = cdim + rdim
    nb = SAMPLE_SEQS
    assert S % nb == 0 and S // nb >= STATE_BUFS - 1 and hd == LANES

    dec, rdec, kdec = (jnp.asarray(t, F32) for t in _decay_tables(ts, heads))

    conv_spec = pl.BlockSpec((None, cwidth - 1, nb, cdim), lambda i: (layer, 0, i, 0))
    st_spec = pl.BlockSpec((None, nb, heads, hd, hd), lambda i: (layer, i, 0, 0, 0))
    args = [proj, cache, state, conv_w, cvec, gn_g, cos2, sin2, dec, rdec, kdec]
    in_specs = [
        pl.BlockSpec((ts, nb, C), lambda i: (0, i, 0)),
        conv_spec,
        pl.BlockSpec(memory_space=pl.ANY),
        _const_spec(conv_w.shape),
        _const_spec(cvec.shape),
        _const_spec((1, rdim)),
        _const_spec((ts, hd)),
        _const_spec((ts, hd)),
        _const_spec(dec.shape),
        _const_spec(rdec.shape),
        _const_spec(kdec.shape),
    ]
    aliases = {}
    if prev is not None:
        aliases = {len(args): 1, len(args) + 1: 2}
        args += list(prev)
        in_specs += [pl.BlockSpec(memory_space=pl.ANY)] * 2

    kern = functools.partial(_sample_mix_kernel, layer=layer, ts=ts, nb=nb, heads=heads, hd=hd, cdim=cdim,
                             cwidth=cwidth)
    return pl.pallas_call(
        kern,
        grid=(S // nb,),
        in_specs=in_specs,
        out_specs=[pl.BlockSpec((ts, nb, D), lambda i: (0, i, 0)), conv_spec, st_spec],
        out_shape=[
            jax.ShapeDtypeStruct((ts, S, D), BF16),
            jax.ShapeDtypeStruct((depth, cwidth - 1, S, cdim), F32),
            jax.ShapeDtypeStruct((depth, S, heads, hd, hd), F32),
        ],
        scratch_shapes=[
            pltpu.VMEM((STATE_BUFS, nb, heads, hd, hd), F32),
            pltpu.SemaphoreType.DMA((STATE_BUFS,)),
        ],
        input_output_aliases=aliases,
        compiler_params=pltpu.CompilerParams(
            dimension_semantics=("arbitrary",), vmem_limit_bytes=VMEM_LIMIT),
        name="sample_mix",
    )(*args)


def _rope_tables(pos0, T, hd):
    half = hd // 2
    pos = pos0 + np.arange(T, dtype=np.float64)
    inv = ROPE_BASE ** (-np.arange(half, dtype=np.float64) / half)
    ang = pos[:, None] * inv[None, :]
    cos, sin = np.cos(ang), np.sin(ang)
    return (jnp.asarray(np.concatenate([cos, cos], axis=-1), F32),
            jnp.asarray(np.concatenate([-sin, sin], axis=-1), F32))


def kernel(x_prompt, x_sample, cache_conv, state_ret, norm1_g, w_in, conv_w, conv_b, conv_ln_g,
           conv_ln_b, ret_gn_g, w_out, norm2_g, w_up, w_down, final_norm_g):
    depth = w_in.shape[0]
    B, T, D = x_prompt.shape
    S, ts, _ = x_sample.shape
    heads, hd = state_ret.shape[2], state_ret.shape[3]

    cvec = jnp.stack([conv_b, conv_ln_g, conv_ln_b], axis=1)
    gf = final_norm_g[None, :]
    cos_p, sin_p = _rope_tables(0.0, T, hd)
    cos_s, sin_s = _rope_tables(float(PAST_LEN), ts, hd)

    xp = x_prompt.reshape(B * T, D)
    xs = x_sample.transpose(1, 0, 2).reshape(ts * S, D)
    cache_t = cache_conv.transpose(0, 2, 1, 3)
    prompt_state, sample_state = None, None
    for l in range(depth):
        final = l == depth - 1
        g1, g2, gn = norm1_g[l][None, :], norm2_g[l][None, :], ret_gn_g[l][None, :]

        proj, w_in_b = _in_proj(xs, g1, w_in, l, IN_PROJ_COLS)
        mix_s, *sample_state = _sample_mix(proj.reshape(ts, S, -1), cache_t, state_ret, l, sample_state,
                                           conv_w[l], cvec[l], gn, cos_s, sin_s)

        mix_p, *rest = _prompt_mix(xp.reshape(B, T, D), g1, w_in_b, l, prompt_state, conv_w[l], cvec[l], gn,
                                   cos_p, sin_p, heads, hd, (w_out, w_up, w_down))
        prompt_state, post_w = rest[:2], rest[2:]

        w_out_b, w_up_b, w_dn_b = post_w
        xp, xs = _post(xp, mix_p.reshape(B * T, D), xs, mix_s.reshape(ts * S, D), w_out_b, g2,
                       w_up_b, w_dn_b, gf, final)

    new_cache_t, new_state = sample_state
    return (xp.reshape(B, T, D), xs.reshape(ts, S, D).transpose(1, 0, 2), *prompt_state,
            new_cache_t.transpose(0, 2, 1, 3), new_state)
```

```python
import functools
import math

import numpy as np
import jax
import jax.numpy as jnp
from jax import lax
from jax.experimental import pallas as pl
from jax.experimental.pallas import tpu as pltpu

F32 = jnp.float32
BF16 = jnp.bfloat16

EPS = 1e-6
ROPE_BASE = 10000.0
PAST_LEN = 16384
LANES = 128
SUBLANES = 8
VMEM_LIMIT = 62 * 1024 * 1024

PROMPT_CHUNK = 1024
RET_CHUNK = 512
CAUSAL_ROWS = 256
CONV_ROWS = 128
GLU_LANE_BLOCKS = 2
POST_ROWS = 1024
MLP_CHUNKS = 4
SAMPLE_SEQS = 16
STATE_BUFS = 3
IN_PROJ_COLS = 1536


def _log_gamma(h):
    return math.log(1.0 - 2.0 ** (-5.0 - h))


def _rmsnorm(x, g):
    ms = jnp.mean(x * x, axis=-1, keepdims=True)
    return x * lax.rsqrt(ms + EPS) * g


def _layernorm(x):
    mu = jnp.mean(x, axis=-1, keepdims=True)
    xc = x - mu
    var = jnp.mean(xc * xc, axis=-1, keepdims=True)
    return xc * lax.rsqrt(var + EPS)


def _silu(x):
    return x * jax.nn.sigmoid(x)


def _rotary(t, cos2, sin2):
    half = t.shape[-1] // 2
    return t * cos2 + pltpu.roll(t, half, t.ndim - 1) * sin2


def _const_spec(shape):
    nd = len(shape)
    return pl.BlockSpec(shape, lambda *_: (0,) * nd, pipeline_mode=pl.Buffered(1))


def _decay_tables(n, heads):
    idx = np.arange(n, dtype=np.float64)
    diff = idx[:, None] - idx[None, :]
    lg = np.array([_log_gamma(h) for h in range(heads)])
    dec = np.where(diff[None] >= 0, np.exp(lg[:, None, None] * np.maximum(diff, 0.0)[None]), 0.0)
    rdec = np.exp(lg[:, None] * (idx + 1.0))[:, :, None]
    kdec = np.exp(lg[:, None] * (n - 1.0 - idx))[:, :, None]
    return dec, rdec, kdec


def _prompt_mix_kernel(x_ref, g1_ref, win_ref, cw_ref, cvec_ref, gn_ref, cos_ref, sin_ref,
                       dec_ref, rdec_ref, kdec_ref, wo_ref, wu_ref, wd_ref, *rest,
                       tq, sub, heads, hd, cdim, cwidth, hist):
    n_out, n_scratch = 6, 8
    mix_ref, nconv_ref, nret_ref, wob_ref, wub_ref, wdb_ref = rest[-(n_out + n_scratch):-n_scratch]
    glu_s, shift_s, conv_s, q_s, k_s, v_s, g_s, state_s = rest[-n_scratch:]
    c = pl.program_id(1)
    last = pl.num_programs(1) - 1
    rdim = heads * hd

    @pl.when(c == 0)
    def _():
        glu_s[0:hist, :] = jnp.zeros((hist, cdim), F32)
        shift_s[:, 0:hist, :] = jnp.zeros((SUBLANES - 1, hist, cdim), F32)
        shift_s[:, tq:tq + hist, :] = jnp.zeros((SUBLANES - 1, hist, cdim), F32)
        state_s[...] = jnp.zeros_like(state_s)

    @pl.when(c > 0)
    def _():
        glu_s[0:hist, :] = glu_s[tq:tq + hist, :]
        shift_s[:, 0:hist, :] = shift_s[:, tq:tq + hist, :]

    wob_ref[...] = wo_ref[...].astype(BF16)
    wub_ref[...] = wu_ref[...].astype(BF16)
    wdb_ref[...] = wd_ref[...].astype(BF16)

    h = _rmsnorm(x_ref[0], g1_ref[...]).astype(BF16)

    def proj(lo, n):
        return jnp.dot(h, win_ref[:, lo:lo + n], preferred_element_type=F32)

    off = hist - (cwidth - 1)
    for lb in range(cdim // LANES):
        ls = slice(lb * LANES, (lb + 1) * LANES)
        if lb % GLU_LANE_BLOCKS == 0:
            n = GLU_LANE_BLOCKS * LANES
            cs = slice(lb * LANES, lb * LANES + n)
            glu = proj(lb * LANES, n) * jax.nn.sigmoid(proj(cdim + lb * LANES, n))
            glu_s[hist:hist + tq, cs] = glu
            for ph in range(1, SUBLANES):
                shift_s[ph - 1, hist - ph:hist - ph + tq, cs] = glu
        for r0 in range(0, tq, CONV_ROWS):
            acc = None
            for w in range(cwidth):
                ph = (off + w) % SUBLANES
                a = r0 + (off + w) // SUBLANES * SUBLANES
                xs = glu_s[a:a + CONV_ROWS, ls] if ph == 0 else shift_s[ph - 1, a:a + CONV_ROWS, ls]
                term = xs * cw_ref[w:w + 1, ls]
                acc = term if acc is None else acc + term
            conv_s[r0:r0 + CONV_ROWS, ls] = acc + cvec_ref[0:1, ls]
    for r0 in range(0, tq, CONV_ROWS):
        cn = _layernorm(conv_s[r0:r0 + CONV_ROWS, :]) * cvec_ref[1:2, :] + cvec_ref[2:3, :]
        mix_ref[0, r0:r0 + CONV_ROWS, 0:cdim] = _silu(cn).astype(BF16)

    c0 = 2 * cdim
    qf = proj(c0, rdim)
    kf = proj(c0 + rdim, rdim)
    cos2 = cos_ref[...]
    sin2 = sin_ref[...]
    scale = hd ** -0.5
    for hh in range(heads):
        hs = slice(hh * hd, (hh + 1) * hd)
        q_s[:, hs] = _rotary(qf[:, hs], cos2, sin2).astype(BF16)
        k_s[:, hs] = _rotary(kf[:, hs], cos2, sin2) * scale
    v_s[...] = proj(c0 + 2 * rdim, rdim).astype(BF16)
    g_s[...] = _silu(proj(c0 + 3 * rdim, rdim))

    for hh in range(heads):
        hs = slice(hh * hd, (hh + 1) * hd)
        st = state_s[hh]
        for r0 in range(0, tq, sub):
            rs = slice(r0, r0 + sub)
            qh = q_s[rs, hs]
            kh = k_s[rs, hs]
            vh = v_s[rs, hs]
            kb = kh.astype(BF16)
            parts = []
            for a in range(0, sub, CAUSAL_ROWS):
                e = a + CAUSAL_ROWS
                sc = lax.dot_general(qh[a:e], kb[0:e], (((1,), (1,)), ((), ())),
                                     preferred_element_type=F32) * dec_ref[hh, a:e, 0:e]
                parts.append(jnp.dot(sc.astype(BF16), vh[0:e], preferred_element_type=F32))
            inner = jnp.concatenate(parts, axis=0)
            cross = jnp.dot(qh, st.astype(BF16), preferred_element_type=F32) * rdec_ref[hh]
            kd = (kh * kdec_ref[hh]).astype(BF16)
            upd = lax.dot_general(kd, vh, (((0,), (0,)), ((), ())), preferred_element_type=F32)
            st = st * math.exp(_log_gamma(hh) * sub) + upd
            on = _layernorm(inner + cross)
            mix_ref[0, rs, cdim + hh * hd:cdim + (hh + 1) * hd] = (
                on * gn_ref[0:1, hs] * g_s[rs, hs]).astype(BF16)
        state_s[hh] = st

    @pl.when(c == last)
    def _():
        nconv_ref[0] = glu_s[hist + tq - (cwidth - 1):hist + tq, :]
        nret_ref[0] = state_s[...]


def _prompt_mix(x, g1, w_in, layer, prev, conv_w, cvec, gn_g, cos2, sin2, heads, hd, post_weights):
    B, T, D = x.shape
    depth = post_weights[0].shape[0]
    cwidth, cdim = conv_w.shape
    rdim = heads * hd
    tq = PROMPT_CHUNK
    sub = RET_CHUNK
    hist = -(-(cwidth - 1) // SUBLANES) * SUBLANES
    assert T % tq == 0 and tq >= hist and tq % CONV_ROWS == 0 and tq % sub == 0 and hd == LANES
    assert sub % CAUSAL_ROWS == 0

    dec, rdec, kdec = _decay_tables(sub, heads)
    dec = jnp.asarray(dec, F32)
    rdec = jnp.asarray(np.broadcast_to(rdec, (heads, sub, hd)), F32)
    kdec = jnp.asarray(np.broadcast_to(kdec, (heads, sub, hd)), F32)

    chunks = T // tq
    steps = B * chunks
    for w in post_weights:
        assert w.shape[1] % (steps * 2 * SUBLANES) == 0
    slab_in = [pl.BlockSpec((None, w.shape[1] // steps, w.shape[2]), lambda b, c: (layer, b * chunks + c, 0))
               for w in post_weights]
    slab_out = [pl.BlockSpec((w.shape[1] // steps, w.shape[2]), lambda b, c: (b * chunks + c, 0))
                for w in post_weights]

    args = [x, g1, w_in, conv_w, cvec, gn_g, cos2, sin2, dec, rdec, kdec, *post_weights]
    in_specs = [
        pl.BlockSpec((1, tq, D), lambda b, c: (b, c, 0)),
        _const_spec((1, D)),
        _const_spec(w_in.shape),
        _const_spec(conv_w.shape),
        _const_spec(cvec.shape),
        _const_spec((1, rdim)),
        pl.BlockSpec((tq, hd), lambda b, c: (c, 0)),
        pl.BlockSpec((tq, hd), lambda b, c: (c, 0)),
        _const_spec(dec.shape),
        _const_spec(rdec.shape),
        _const_spec(kdec.shape),
        *slab_in,
    ]
    aliases = {}
    if prev is not None:
        aliases = {len(args): 1, len(args) + 1: 2}
        args += list(prev)
        in_specs += [pl.BlockSpec(memory_space=pl.ANY)] * 2

    kern = functools.partial(_prompt_mix_kernel, tq=tq, sub=sub, heads=heads, hd=hd, cdim=cdim,
                             cwidth=cwidth, hist=hist)
    return pl.pallas_call(
        kern,
        grid=(B, chunks),
        in_specs=in_specs,
        out_specs=[
            pl.BlockSpec((1, tq, D), lambda b, c: (b, c, 0)),
            pl.BlockSpec((None, 1, cwidth - 1, cdim), lambda b, c: (layer, b, 0, 0)),
            pl.BlockSpec((None, 1, heads, hd, hd), lambda b, c: (layer, b, 0, 0, 0)),
            *slab_out,
        ],
        out_shape=[
            jax.ShapeDtypeStruct((B, T, D), BF16),
            jax.ShapeDtypeStruct((depth, B, cwidth - 1, cdim), F32),
            jax.ShapeDtypeStruct((depth, B, heads, hd, hd), F32),
            *[jax.ShapeDtypeStruct(w.shape[1:], BF16) for w in post_weights],
        ],
        input_output_aliases=aliases,
        scratch_shapes=[
            pltpu.VMEM((hist + tq, cdim), F32),
            pltpu.VMEM((SUBLANES - 1, hist + tq, cdim), F32),
            pltpu.VMEM((tq, cdim), F32),
            pltpu.VMEM((tq, rdim), BF16),
            pltpu.VMEM((tq, rdim), F32),
            pltpu.VMEM((tq, rdim), BF16),
            pltpu.VMEM((tq, rdim), F32),
            pltpu.VMEM((heads, hd, hd), F32),
        ],
        compiler_params=pltpu.CompilerParams(
            dimension_semantics=("arbitrary", "arbitrary"), vmem_limit_bytes=VMEM_LIMIT),
        name="prompt_mix",
    )(*args)


def _post_kernel(xa_ref, mixa_ref, xb_ref, mixb_ref, wout_hbm, g2_ref, wup_hbm, wdn_hbm, gf_ref,
                 oa_ref, ob_ref, wout_ref, wup_ref, wdn_ref, w_sem, *, final, steps_a):
    i = pl.program_id(0)
    copies = [pltpu.make_async_copy(src, dst, w_sem.at[n])
              for n, (src, dst) in enumerate(((wout_hbm, wout_ref), (wup_hbm, wup_ref), (wdn_hbm, wdn_ref)))]

    @pl.when(i == 0)
    def _():
        for cp in copies:
            cp.start()

    def arrive(n, may_be_first):
        if may_be_first:
            @pl.when(i == 0)
            def _():
                copies[n].wait()

    def tile(x_ref, mix_ref, o_ref, may_be_first):
        arrive(0, may_be_first)
        y = x_ref[...] + jnp.dot(mix_ref[...], wout_ref[...], preferred_element_type=F32)
        h2 = _rmsnorm(y, g2_ref[...]).astype(BF16)
        dff = wup_ref.shape[1]
        hc = dff // MLP_CHUNKS
        out = y
        arrive(1, may_be_first)
        for k in range(MLP_CHUNKS):
            hf = jnp.dot(h2, wup_ref[:, k * hc:(k + 1) * hc], preferred_element_type=F32)
            act = jnp.square(jnp.maximum(hf, 0.0)).astype(BF16)
            if k == 0:
                arrive(2, may_be_first)
            out = out + jnp.dot(act, wdn_ref[k * hc:(k + 1) * hc, :], preferred_element_type=F32)
        if final:
            out = _rmsnorm(out, gf_ref[...])
        o_ref[...] = out

    @pl.when(i < steps_a)
    def _():
        tile(xa_ref, mixa_ref, oa_ref, True)

    @pl.when(i >= steps_a)
    def _():
        tile(xb_ref, mixb_ref, ob_ref, False)


def _post(xa, mixa, xb, mixb, w_out, g2, w_up, w_down, gf, final):
    Ra, D = xa.shape
    Rb = xb.shape[0]
    tra, trb = min(POST_ROWS, Ra), min(POST_ROWS, Rb)
    assert Ra % tra == 0 and Rb % trb == 0 and Ra >= tra
    na, nb = Ra // tra, Rb // trb
    rows_a = pl.BlockSpec((tra, D), lambda i: (jnp.minimum(i, na - 1), 0))
    rows_b = pl.BlockSpec((trb, D), lambda i: (jnp.maximum(i - na, 0), 0))

    return pl.pallas_call(
        functools.partial(_post_kernel, final=final, steps_a=na),
        grid=(na + nb,),
        in_specs=[
            rows_a, rows_a, rows_b, rows_b,
            pl.BlockSpec(memory_space=pl.ANY),
            _const_spec((1, D)),
            pl.BlockSpec(memory_space=pl.ANY),
            pl.BlockSpec(memory_space=pl.ANY),
            _const_spec((1, D)),
        ],
        out_specs=[rows_a, rows_b],
        out_shape=[jax.ShapeDtypeStruct((Ra, D), F32), jax.ShapeDtypeStruct((Rb, D), F32)],
        scratch_shapes=[
            pltpu.VMEM(w_out.shape, BF16), pltpu.VMEM(w_up.shape, BF16), pltpu.VMEM(w_down.shape, BF16),
            pltpu.SemaphoreType.DMA((3,)),
        ],
        compiler_params=pltpu.CompilerParams(
            dimension_semantics=("arbitrary",), vmem_limit_bytes=VMEM_LIMIT),
        name="post",
    )(xa, mixa, xb, mixb, w_out, g2, w_up, w_down, gf)


def _in_proj_kernel(x_ref, g1_ref, w_ref, o_ref, wb_ref, h_s):
    @pl.when(pl.program_id(0) == 0)
    def _():
        h_s[...] = _rmsnorm(x_ref[...], g1_ref[...]).astype(BF16)

    wb_ref[...] = w_ref[...].astype(BF16)
    o_ref[...] = jnp.dot(h_s[...], wb_ref[...], preferred_element_type=F32)


def _in_proj(x, g1, w_in, layer, ncol):
    R, D = x.shape
    C = w_in.shape[2]
    return pl.pallas_call(
        _in_proj_kernel,
        grid=(C // ncol,),
        in_specs=[
            _const_spec((R, D)),
            _const_spec((1, D)),
            pl.BlockSpec((None, D, ncol), lambda j: (layer, 0, j)),
        ],
        out_specs=[pl.BlockSpec((R, ncol), lambda j: (0, j)), pl.BlockSpec((D, ncol), lambda j: (0, j))],
        out_shape=[jax.ShapeDtypeStruct((R, C), F32), jax.ShapeDtypeStruct((D, C), BF16)],
        scratch_shapes=[pltpu.VMEM((R, D), BF16)],
        compiler_params=pltpu.CompilerParams(
            dimension_semantics=("arbitrary",), vmem_limit_bytes=VMEM_LIMIT),
        name="sample_in_proj",
    )(x, g1, w_in)


def _sample_mix_kernel(proj_ref, cache_ref, st_ref, cw_ref, cvec_ref, gn_ref, cos_ref, sin_ref,
                       dec_ref, rdec_ref, kdec_ref, *rest, layer, ts, nb, heads, hd, cdim, cwidth):
    mix_ref, ncache_ref, nst_ref, st_buf, st_sem = rest[-5:]
    rdim = heads * hd
    i = pl.program_id(0)
    steps = pl.num_programs(0)

    def fetch(step):
        slot = step % STATE_BUFS
        return pltpu.make_async_copy(st_ref.at[layer, pl.ds(step * nb, nb)], st_buf.at[slot], st_sem.at[slot])

    @pl.when(i == 0)
    def _():
        for s in range(STATE_BUFS - 1):
            fetch(s).start()

    @pl.when(i + (STATE_BUFS - 1) < steps)
    def _():
        fetch(i + (STATE_BUFS - 1)).start()

    fetch(i).wait()
    st_cur = st_buf.at[i % STATE_BUFS]

    glu = [proj_ref[t, :, 0:cdim] * jax.nn.sigmoid(proj_ref[t, :, cdim:2 * cdim]) for t in range(ts)]

    def window(r):
        return cache_ref[r] if r < cwidth - 1 else glu[r - (cwidth - 1)]

    acc = [None] * ts
    for r in range(cwidth - 1 + ts):
        x = window(r)
        for t in range(max(0, r - cwidth + 1), min(ts, r + 1)):
            term = x * cw_ref[r - t:r - t + 1, :]
            acc[t] = term if acc[t] is None else acc[t] + term
    for t in range(ts):
        cn = _layernorm(acc[t] + cvec_ref[0:1, :]) * cvec_ref[1:2, :] + cvec_ref[2:3, :]
        mix_ref[t, :, 0:cdim] = _silu(cn).astype(BF16)
    for r in range(cwidth - 1):
        ncache_ref[r] = window(r + ts)

    c0 = 2 * cdim
    cos2 = cos_ref[...]
    sin2 = sin_ref[...]
    scale = hd ** -0.5
    for hh in range(heads):
        lo = hh * hd

        def per_seq(base):
            return jnp.stack([proj_ref[:, s, base + lo:base + lo + hd] for s in range(nb)], axis=0)

        q = _rotary(per_seq(c0), cos2, sin2)
        k = _rotary(per_seq(c0 + rdim), cos2, sin2) * scale
        v = per_seq(c0 + 2 * rdim)
        g = per_seq(c0 + 3 * rdim)
        st = st_cur[:, hh]
        scores = jnp.einsum('btd,bjd->btj', q, k, preferred_element_type=F32) * dec_ref[hh]
        inner = jnp.einsum('btj,bjv->btv', scores, v, preferred_element_type=F32)
        cross = jnp.einsum('btd,bdv->btv', q, st, preferred_element_type=F32) * rdec_ref[hh]
        upd = jnp.einsum('bjd,bjv->bdv', k * kdec_ref[hh], v, preferred_element_type=F32)
        nst_ref[:, hh] = st * math.exp(_log_gamma(hh) * ts) + upd
        on = _layernorm(inner + cross)
        out = (on * gn_ref[0:1, lo:lo + hd] * _silu(g)).astype(BF16)
        for s in range(nb):
            mix_ref[:, s, cdim + lo:cdim + lo + hd] = out[s]


def _sample_mix(proj, cache, state, layer, prev, conv_w, cvec, gn_g, cos2, sin2):
    ts, S, C = proj.shape
    depth, _, heads, hd, _ = state.shape
    cwidth, cdim = conv_w.shape
    rdim = heads * hd
    D = cdim + rdim
    nb = SAMPLE_SEQS
    assert S % nb == 0 and S // nb >= STATE_BUFS - 1 and hd == LANES

    dec, rdec, kdec = (jnp.asarray(t, F32) for t in _decay_tables(ts, heads))

    conv_spec = pl.BlockSpec((None, cwidth - 1, nb, cdim), lambda i: (layer, 0, i, 0))
    st_spec = pl.BlockSpec((None, nb, heads, hd, hd), lambda i: (layer, i, 0, 0, 0))
    args = [proj, cache, state, conv_w, cvec, gn_g, cos2, sin2, dec, rdec, kdec]
    in_specs = [
        pl.BlockSpec((ts, nb, C), lambda i: (0, i, 0)),
        conv_spec,
        pl.BlockSpec(memory_space=pl.ANY),
        _const_spec(conv_w.shape),
        _const_spec(cvec.shape),
        _const_spec((1, rdim)),
        _const_spec((ts, hd)),
        _const_spec((ts, hd)),
        _const_spec(dec.shape),
        _const_spec(rdec.shape),
        _const_spec(kdec.shape),
    ]
    aliases = {}
    if prev is not None:
        aliases = {len(args): 1, len(args) + 1: 2}
        args += list(prev)
        in_specs += [pl.BlockSpec(memory_space=pl.ANY)] * 2

    kern = functools.partial(_sample_mix_kernel, layer=layer, ts=ts, nb=nb, heads=heads, hd=hd, cdim=cdim,
                             cwidth=cwidth)
    return pl.pallas_call(
        kern,
        grid=(S // nb,),
        in_specs=in_specs,
        out_specs=[pl.BlockSpec((ts, nb, D), lambda i: (0, i, 0)), conv_spec, st_spec],
        out_shape=[
            jax.ShapeDtypeStruct((ts, S, D), BF16),
            jax.ShapeDtypeStruct((depth, cwidth - 1, S, cdim), F32),
            jax.ShapeDtypeStruct((depth, S, heads, hd, hd), F32),
        ],
        scratch_shapes=[
            pltpu.VMEM((STATE_BUFS, nb, heads, hd, hd), F32),
            pltpu.SemaphoreType.DMA((STATE_BUFS,)),
        ],
        input_output_aliases=aliases,
        compiler_params=pltpu.CompilerParams(
            dimension_semantics=("arbitrary",), vmem_limit_bytes=VMEM_LIMIT),
        name="sample_mix",
    )(*args)


def _rope_tables(pos0, T, hd):
    half = hd // 2
    pos = pos0 + np.arange(T, dtype=np.float64)
    inv = ROPE_BASE ** (-np.arange(half, dtype=np.float64) / half)
    ang = pos[:, None] * inv[None, :]
    cos, sin = np.cos(ang), np.sin(ang)
    return (jnp.asarray(np.concatenate([cos, cos], axis=-1), F32),
            jnp.asarray(np.concatenate([-sin, sin], axis=-1), F32))


def kernel(x_prompt, x_sample, cache_conv, state_ret, norm1_g, w_in, conv_w, conv_b, conv_ln_g,
           conv_ln_b, ret_gn_g, w_out, norm2_g, w_up, w_down, final_norm_g):
    depth = w_in.shape[0]
    B, T, D = x_prompt.shape
    S, ts, _ = x_sample.shape
    heads, hd = state_ret.shape[2], state_ret.shape[3]

    cvec = jnp.stack([conv_b, conv_ln_g, conv_ln_b], axis=1)
    gf = final_norm_g[None, :]
    cos_p, sin_p = _rope_tables(0.0, T, hd)
    cos_s, sin_s = _rope_tables(float(PAST_LEN), ts, hd)

    xp = x_prompt.reshape(B * T, D)
    xs = x_sample.transpose(1, 0, 2).reshape(ts * S, D)
    cache_t = cache_conv.transpose(0, 2, 1, 3)
    prompt_state, sample_state = None, None
    for l in range(depth):
        final = l == depth - 1
        g1, g2, gn = norm1_g[l][None, :], norm2_g[l][None, :], ret_gn_g[l][None, :]

        proj, w_in_b = _in_proj(xs, g1, w_in, l, IN_PROJ_COLS)
        mix_s, *sample_state = _sample_mix(proj.reshape(ts, S, -1), cache_t, state_ret, l, sample_state,
                                           conv_w[l], cvec[l], gn, cos_s, sin_s)

        mix_p, *rest = _prompt_mix(xp.reshape(B, T, D), g1, w_in_b, l, prompt_state, conv_w[l], cvec[l], gn,
                                   cos_p, sin_p, heads, hd, (w_out, w_up, w_down))
        prompt_state, post_w = rest[:2], rest[2:]

        w_out_b, w_up_b, w_dn_b = post_w
        xp, xs = _post(xp, mix_p.reshape(B * T, D), xs, mix_s.reshape(ts * S, D), w_out_b, g2,
                       w_up_b, w_dn_b, gf, final)

    new_cache_t, new_state = sample_state
    return (xp.reshape(B, T, D), xs.reshape(ts, S, D).transpose(1, 0, 2), *prompt_state,
            new_cache_t.transpose(0, 2, 1, 3), new_state)
```

```python
import functools
import math

import numpy as np
import jax
import jax.numpy as jnp
from jax import lax
from jax.experimental import pallas as pl
from jax.experimental.pallas import tpu as pltpu

F32 = jnp.float32
BF16 = jnp.bfloat16

EPS = 1e-6
ROPE_BASE = 10000.0
PAST_LEN = 16384
LANES = 128
SUBLANES = 8
VMEM_LIMIT = 62 * 1024 * 1024

PROMPT_CHUNK = 1024
RET_CHUNK = 512
CAUSAL_ROWS = 256
CONV_ROWS = 128
GLU_LANE_BLOCKS = 2
POST_ROWS = 1024
MLP_CHUNKS = 4
SAMPLE_SEQS = 16
STATE_BUFS = 4
IN_PROJ_COLS = 1536


def _log_gamma(h):
    return math.log(1.0 - 2.0 ** (-5.0 - h))


def _rmsnorm(x, g):
    ms = jnp.mean(x * x, axis=-1, keepdims=True)
    return x * lax.rsqrt(ms + EPS) * g


def _layernorm(x):
    mu = jnp.mean(x, axis=-1, keepdims=True)
    xc = x - mu
    var = jnp.mean(xc * xc, axis=-1, keepdims=True)
    return xc * lax.rsqrt(var + EPS)


def _silu(x):
    return x * jax.nn.sigmoid(x)


def _rotary(t, cos2, sin2):
    half = t.shape[-1] // 2
    return t * cos2 + pltpu.roll(t, half, t.ndim - 1) * sin2


def _const_spec(shape):
    nd = len(shape)
    return pl.BlockSpec(shape, lambda *_: (0,) * nd, pipeline_mode=pl.Buffered(1))


def _decay_tables(n, heads):
    idx = np.arange(n, dtype=np.float64)
    diff = idx[:, None] - idx[None, :]
    lg = np.array([_log_gamma(h) for h in range(heads)])
    dec = np.where(diff[None] >= 0, np.exp(lg[:, None, None] * np.maximum(diff, 0.0)[None]), 0.0)
    rdec = np.exp(lg[:, None] * (idx + 1.0))[:, :, None]
    kdec = np.exp(lg[:, None] * (n - 1.0 - idx))[:, :, None]
    return dec, rdec, kdec


def _prompt_mix_kernel(x_ref, g1_ref, win_ref, cw_ref, cvec_ref, gn_ref, cos_ref, sin_ref,
                       dec_ref, rdec_ref, kdec_ref, wo_ref, wu_ref, wd_ref, *rest,
                       tq, sub, heads, hd, cdim, cwidth, hist):
    n_out, n_scratch = 6, 8
    mix_ref, nconv_ref, nret_ref, wob_ref, wub_ref, wdb_ref = rest[-(n_out + n_scratch):-n_scratch]
    glu_s, shift_s, conv_s, q_s, k_s, v_s, g_s, state_s = rest[-n_scratch:]
    c = pl.program_id(1)
    last = pl.num_programs(1) - 1
    rdim = heads * hd

    @pl.when(c == 0)
    def _():
        glu_s[0:hist, :] = jnp.zeros((hist, cdim), F32)
        shift_s[:, 0:hist, :] = jnp.zeros((SUBLANES - 1, hist, cdim), F32)
        shift_s[:, tq:tq + hist, :] = jnp.zeros((SUBLANES - 1, hist, cdim), F32)
        state_s[...] = jnp.zeros_like(state_s)

    @pl.when(c > 0)
    def _():
        glu_s[0:hist, :] = glu_s[tq:tq + hist, :]
        shift_s[:, 0:hist, :] = shift_s[:, tq:tq + hist, :]

    wob_ref[...] = wo_ref[...].astype(BF16)
    wub_ref[...] = wu_ref[...].astype(BF16)
    wdb_ref[...] = wd_ref[...].astype(BF16)

    h = _rmsnorm(x_ref[0], g1_ref[...]).astype(BF16)

    def proj(lo, n):
        return jnp.dot(h, win_ref[:, lo:lo + n], preferred_element_type=F32)

    off = hist - (cwidth - 1)
    for lb in range(cdim // LANES):
        ls = slice(lb * LANES, (lb + 1) * LANES)
        if lb % GLU_LANE_BLOCKS == 0:
            n = GLU_LANE_BLOCKS * LANES
            cs = slice(lb * LANES, lb * LANES + n)
            glu = proj(lb * LANES, n) * jax.nn.sigmoid(proj(cdim + lb * LANES, n))
            glu_s[hist:hist + tq, cs] = glu
            for ph in range(1, SUBLANES):
                shift_s[ph - 1, hist - ph:hist - ph + tq, cs] = glu
        for r0 in range(0, tq, CONV_ROWS):
            acc = None
            for w in range(cwidth):
                ph = (off + w) % SUBLANES
                a = r0 + (off + w) // SUBLANES * SUBLANES
                xs = glu_s[a:a + CONV_ROWS, ls] if ph == 0 else shift_s[ph - 1, a:a + CONV_ROWS, ls]
                term = xs * cw_ref[w:w + 1, ls]
                acc = term if acc is None else acc + term
            conv_s[r0:r0 + CONV_ROWS, ls] = acc + cvec_ref[0:1, ls]
    for r0 in range(0, tq, CONV_ROWS):
        cn = _layernorm(conv_s[r0:r0 + CONV_ROWS, :]) * cvec_ref[1:2, :] + cvec_ref[2:3, :]
        mix_ref[0, r0:r0 + CONV_ROWS, 0:cdim] = _silu(cn).astype(BF16)

    c0 = 2 * cdim
    qf = proj(c0, rdim)
    kf = proj(c0 + rdim, rdim)
    cos2 = cos_ref[...]
    sin2 = sin_ref[...]
    scale = hd ** -0.5
    for hh in range(heads):
        hs = slice(hh * hd, (hh + 1) * hd)
        q_s[:, hs] = _rotary(qf[:, hs], cos2, sin2).astype(BF16)
        k_s[:, hs] = _rotary(kf[:, hs], cos2, sin2) * scale
    v_s[...] = proj(c0 + 2 * rdim, rdim).astype(BF16)
    g_s[...] = _silu(proj(c0 + 3 * rdim, rdim))

    for hh in range(heads):
        hs = slice(hh * hd, (hh + 1) * hd)
        st = state_s[hh]
        for r0 in range(0, tq, sub):
            rs = slice(r0, r0 + sub)
            qh = q_s[rs, hs]
            kh = k_s[rs, hs]
            vh = v_s[rs, hs]
            kb = kh.astype(BF16)
            parts = []
            for a in range(0, sub, CAUSAL_ROWS):
                e = a + CAUSAL_ROWS
                sc = lax.dot_general(qh[a:e], kb[0:e], (((1,), (1,)), ((), ())),
                                     preferred_element_type=F32) * dec_ref[hh, a:e, 0:e]
                parts.append(jnp.dot(sc.astype(BF16), vh[0:e], preferred_element_type=F32))
            inner = jnp.concatenate(parts, axis=0)
            cross = jnp.dot(qh, st.astype(BF16), preferred_element_type=F32) * rdec_ref[hh]
            kd = (kh * kdec_ref[hh]).astype(BF16)
            upd = lax.dot_general(kd, vh, (((0,), (0,)), ((), ())), preferred_element_type=F32)
            st = st * math.exp(_log_gamma(hh) * sub) + upd
            on = _layernorm(inner + cross)
            mix_ref[0, rs, cdim + hh * hd:cdim + (hh + 1) * hd] = (
                on * gn_ref[0:1, hs] * g_s[rs, hs]).astype(BF16)
        state_s[hh] = st

    @pl.when(c == last)
    def _():
        nconv_ref[0] = glu_s[hist + tq - (cwidth - 1):hist + tq, :]
        nret_ref[0] = state_s[...]


def _prompt_mix(x, g1, w_in, layer, prev, conv_w, cvec, gn_g, cos2, sin2, heads, hd, post_weights):
    B, T, D = x.shape
    depth = post_weights[0].shape[0]
    cwidth, cdim = conv_w.shape
    rdim = heads * hd
    tq = PROMPT_CHUNK
    sub = RET_CHUNK
    hist = -(-(cwidth - 1) // SUBLANES) * SUBLANES
    assert T % tq == 0 and tq >= hist and tq % CONV_ROWS == 0 and tq % sub == 0 and hd == LANES
    assert sub % CAUSAL_ROWS == 0

    dec, rdec, kdec = _decay_tables(sub, heads)
    dec = jnp.asarray(dec, F32)
    rdec = jnp.asarray(np.broadcast_to(rdec, (heads, sub, hd)), F32)
    kdec = jnp.asarray(np.broadcast_to(kdec, (heads, sub, hd)), F32)

    chunks = T // tq
    steps = B * chunks
    for w in post_weights:
        assert w.shape[1] % (steps * 2 * SUBLANES) == 0
    slab_in = [pl.BlockSpec((None, w.shape[1] // steps, w.shape[2]), lambda b, c: (layer, b * chunks + c, 0))
               for w in post_weights]
    slab_out = [pl.BlockSpec((w.shape[1] // steps, w.shape[2]), lambda b, c: (b * chunks + c, 0))
                for w in post_weights]

    args = [x, g1, w_in, conv_w, cvec, gn_g, cos2, sin2, dec, rdec, kdec, *post_weights]
    in_specs = [
        pl.BlockSpec((1, tq, D), lambda b, c: (b, c, 0)),
        _const_spec((1, D)),
        _const_spec(w_in.shape),
        _const_spec(conv_w.shape),
        _const_spec(cvec.shape),
        _const_spec((1, rdim)),
        pl.BlockSpec((tq, hd), lambda b, c: (c, 0)),
        pl.BlockSpec((tq, hd), lambda b, c: (c, 0)),
        _const_spec(dec.shape),
        _const_spec(rdec.shape),
        _const_spec(kdec.shape),
        *slab_in,
    ]
    aliases = {}
    if prev is not None:
        aliases = {len(args): 1, len(args) + 1: 2}
        args += list(prev)
        in_specs += [pl.BlockSpec(memory_space=pl.ANY)] * 2

    kern = functools.partial(_prompt_mix_kernel, tq=tq, sub=sub, heads=heads, hd=hd, cdim=cdim,
                             cwidth=cwidth, hist=hist)
    return pl.pallas_call(
        kern,
        grid=(B, chunks),
        in_specs=in_specs,
        out_specs=[
            pl.BlockSpec((1, tq, D), lambda b, c: (b, c, 0)),
            pl.BlockSpec((None, 1, cwidth - 1, cdim), lambda b, c: (layer, b, 0, 0)),
            pl.BlockSpec((None, 1, heads, hd, hd), lambda b, c: (layer, b, 0, 0, 0)),
            *slab_out,
        ],
        out_shape=[
            jax.ShapeDtypeStruct((B, T, D), BF16),
            jax.ShapeDtypeStruct((depth, B, cwidth - 1, cdim), F32),
            jax.ShapeDtypeStruct((depth, B, heads, hd, hd), F32),
            *[jax.ShapeDtypeStruct(w.shape[1:], BF16) for w in post_weights],
        ],
        input_output_aliases=aliases,
        scratch_shapes=[
            pltpu.VMEM((hist + tq, cdim), F32),
            pltpu.VMEM((SUBLANES - 1, hist + tq, cdim), F32),
            pltpu.VMEM((tq, cdim), F32),
            pltpu.VMEM((tq, rdim), BF16),
            pltpu.VMEM((tq, rdim), F32),
            pltpu.VMEM((tq, rdim), BF16),
            pltpu.VMEM((tq, rdim), F32),
            pltpu.VMEM((heads, hd, hd), F32),
        ],
        compiler_params=pltpu.CompilerParams(
            dimension_semantics=("arbitrary", "arbitrary"), vmem_limit_bytes=VMEM_LIMIT),
        name="prompt_mix",
    )(*args)


def _post_kernel(xa_ref, mixa_ref, xb_ref, mixb_ref, wout_ref, g2_ref, wup_ref, wdn_ref, gf_ref,
                 oa_ref, ob_ref, *, final, steps_a):
    def tile(x_ref, mix_ref, o_ref):
        y = x_ref[...] + jnp.dot(mix_ref[...], wout_ref[...], preferred_element_type=F32)
        h2 = _rmsnorm(y, g2_ref[...]).astype(BF16)
        dff = wup_ref.shape[1]
        hc = dff // MLP_CHUNKS
        out = y
        for k in range(MLP_CHUNKS):
            hf = jnp.dot(h2, wup_ref[:, k * hc:(k + 1) * hc], preferred_element_type=F32)
            act = jnp.square(jnp.maximum(hf, 0.0)).astype(BF16)
            out = out + jnp.dot(act, wdn_ref[k * hc:(k + 1) * hc, :], preferred_element_type=F32)
        if final:
            out = _rmsnorm(out, gf_ref[...])
        o_ref[...] = out

    i = pl.program_id(0)

    @pl.when(i < steps_a)
    def _():
        tile(xa_ref, mixa_ref, oa_ref)

    @pl.when(i >= steps_a)
    def _():
        tile(xb_ref, mixb_ref, ob_ref)


def _post(xa, mixa, xb, mixb, w_out, g2, w_up, w_down, gf, final):
    Ra, D = xa.shape
    Rb = xb.shape[0]
    tra, trb = min(POST_ROWS, Ra), min(POST_ROWS, Rb)
    assert Ra % tra == 0 and Rb % trb == 0
    na, nb = Ra // tra, Rb // trb
    rows_a = pl.BlockSpec((tra, D), lambda i: (jnp.minimum(i, na - 1), 0))
    rows_b = pl.BlockSpec((trb, D), lambda i: (jnp.maximum(i - na, 0), 0))

    return pl.pallas_call(
        functools.partial(_post_kernel, final=final, steps_a=na),
        grid=(na + nb,),
        in_specs=[
            rows_a, rows_a, rows_b, rows_b,
            _const_spec(w_out.shape),
            _const_spec((1, D)),
            _const_spec(w_up.shape),
            _const_spec(w_down.shape),
            _const_spec((1, D)),
        ],
        out_specs=[rows_a, rows_b],
        out_shape=[jax.ShapeDtypeStruct((Ra, D), F32), jax.ShapeDtypeStruct((Rb, D), F32)],
        compiler_params=pltpu.CompilerParams(
            dimension_semantics=("arbitrary",), vmem_limit_bytes=VMEM_LIMIT),
        name="post",
    )(xa, mixa, xb, mixb, w_out, g2, w_up, w_down, gf)


def _in_proj_kernel(x_ref, g1_ref, w_ref, o_ref, wb_ref, h_s):
    @pl.when(pl.program_id(0) == 0)
    def _():
        h_s[...] = _rmsnorm(x_ref[...], g1_ref[...]).astype(BF16)

    wb_ref[...] = w_ref[...].astype(BF16)
    o_ref[...] = jnp.dot(h_s[...], wb_ref[...], preferred_element_type=F32)


def _in_proj(x, g1, w_in, layer, ncol):
    R, D = x.shape
    C = w_in.shape[2]
    return pl.pallas_call(
        _in_proj_kernel,
        grid=(C // ncol,),
        in_specs=[
            _const_spec((R, D)),
            _const_spec((1, D)),
            pl.BlockSpec((None, D, ncol), lambda j: (layer, 0, j)),
        ],
        out_specs=[pl.BlockSpec((R, ncol), lambda j: (0, j)), pl.BlockSpec((D, ncol), lambda j: (0, j))],
        out_shape=[jax.ShapeDtypeStruct((R, C), F32), jax.ShapeDtypeStruct((D, C), BF16)],
        scratch_shapes=[pltpu.VMEM((R, D), BF16)],
        compiler_params=pltpu.CompilerParams(
            dimension_semantics=("arbitrary",), vmem_limit_bytes=VMEM_LIMIT),
        name="sample_in_proj",
    )(x, g1, w_in)


def _sample_mix_kernel(proj_ref, cache_ref, st_ref, cw_ref, cvec_ref, gn_ref, cos_ref, sin_ref,
                       dec_ref, rdec_ref, kdec_ref, *rest, layer, ts, nb, heads, hd, cdim, cwidth):
    mix_ref, ncache_ref, nst_ref, st_buf, st_sem = rest[-5:]
    rdim = heads * hd
    i = pl.program_id(0)
    steps = pl.num_programs(0)

    def fetch(step):
        slot = step % STATE_BUFS
        return pltpu.make_async_copy(st_ref.at[layer, pl.ds(step * nb, nb)], st_buf.at[slot], st_sem.at[slot])

    @pl.when(i == 0)
    def _():
        for s in range(STATE_BUFS - 1):
            fetch(s).start()

    @pl.when(i + (STATE_BUFS - 1) < steps)
    def _():
        fetch(i + (STATE_BUFS - 1)).start()

    fetch(i).wait()
    st_cur = st_buf.at[i % STATE_BUFS]

    glu = [proj_ref[t, :, 0:cdim] * jax.nn.sigmoid(proj_ref[t, :, cdim:2 * cdim]) for t in range(ts)]

    def window(r):
        return cache_ref[r] if r < cwidth - 1 else glu[r - (cwidth - 1)]

    acc = [None] * ts
    for r in range(cwidth - 1 + ts):
        x = window(r)
        for t in range(max(0, r - cwidth + 1), min(ts, r + 1)):
            term = x * cw_ref[r - t:r - t + 1, :]
            acc[t] = term if acc[t] is None else acc[t] + term
    for t in range(ts):
        cn = _layernorm(acc[t] + cvec_ref[0:1, :]) * cvec_ref[1:2, :] + cvec_ref[2:3, :]
        mix_ref[t, :, 0:cdim] = _silu(cn).astype(BF16)
    for r in range(cwidth - 1):
        ncache_ref[r] = window(r + ts)

    c0 = 2 * cdim
    cos2 = cos_ref[...]
    sin2 = sin_ref[...]
    scale = hd ** -0.5
    for hh in range(heads):
        lo = hh * hd

        def per_seq(base):
            return jnp.stack([proj_ref[:, s, base + lo:base + lo + hd] for s in range(nb)], axis=0)

        q = _rotary(per_seq(c0), cos2, sin2)
        k = _rotary(per_seq(c0 + rdim), cos2, sin2) * scale
        v = per_seq(c0 + 2 * rdim)
        g = per_seq(c0 + 3 * rdim)
        st = st_cur[:, hh]
        scores = jnp.einsum('btd,bjd->btj', q, k, preferred_element_type=F32) * dec_ref[hh]
        inner = jnp.einsum('btj,bjv->btv', scores, v, preferred_element_type=F32)
        cross = jnp.einsum('btd,bdv->btv', q, st, preferred_element_type=F32) * rdec_ref[hh]
        upd = jnp.einsum('bjd,bjv->bdv', k * kdec_ref[hh], v, preferred_element_type=F32)
        nst_ref[:, hh] = st * math.exp(_log_gamma(hh) * ts) + upd
        on = _layernorm(inner + cross)
        out = (on * gn_ref[0:1, lo:lo + hd] * _silu(g)).astype(BF16)
        for s in range(nb):
            mix_ref[:, s, cdim + lo:cdim + lo + hd] = out[s]


def _sample_mix(proj, cache, state, layer, prev, conv_w, cvec, gn_g, cos2, sin2):
    ts, S, C = proj.shape
    depth, _, heads, hd, _ = state.shape
    cwidth, cdim = conv_w.shape
    rdim = heads * hd
    D = cdim + rdim
    nb = SAMPLE_SEQS
    assert S % nb == 0 and S // nb >= STATE_BUFS - 1 and hd == LANES

    dec, rdec, kdec = (jnp.asarray(t, F32) for t in _decay_tables(ts, heads))

    conv_spec = pl.BlockSpec((None, cwidth - 1, nb, cdim), lambda i: (layer, 0, i, 0))
    st_spec = pl.BlockSpec((None, nb, heads, hd, hd), lambda i: (layer, i, 0, 0, 0))
    args = [proj, cache, state, conv_w, cvec, gn_g, cos2, sin2, dec, rdec, kdec]
    in_specs = [
        pl.BlockSpec((ts, nb, C), lambda i: (0, i, 0)),
        conv_spec,
        pl.BlockSpec(memory_space=pl.ANY),
        _const_spec(conv_w.shape),
        _const_spec(cvec.shape),
        _const_spec((1, rdim)),
        _const_spec((ts, hd)),
        _const_spec((ts, hd)),
        _const_spec(dec.shape),
        _const_spec(rdec.shape),
        _const_spec(kdec.shape),
    ]
    aliases = {}
    if prev is not None:
        aliases = {len(args): 1, len(args) + 1: 2}
        args += list(prev)
        in_specs += [pl.BlockSpec(memory_space=pl.ANY)] * 2

    kern = functools.partial(_sample_mix_kernel, layer=layer, ts=ts, nb=nb, heads=heads, hd=hd, cdim=cdim,
                             cwidth=cwidth)
    return pl.pallas_call(
        kern,
        grid=(S // nb,),
        in_specs=in_specs,
        out_specs=[pl.BlockSpec((ts, nb, D), lambda i: (0, i, 0)), conv_spec, st_spec],
        out_shape=[
            jax.ShapeDtypeStruct((ts, S, D), BF16),
            jax.ShapeDtypeStruct((depth, cwidth - 1, S, cdim), F32),
            jax.ShapeDtypeStruct((depth, S, heads, hd, hd), F32),
        ],
        scratch_shapes=[
            pltpu.VMEM((STATE_BUFS, nb, heads, hd, hd), F32),
            pltpu.SemaphoreType.DMA((STATE_BUFS,)),
        ],
        input_output_aliases=aliases,
        compiler_params=pltpu.CompilerParams(
            dimension_semantics=("arbitrary",), vmem_limit_bytes=VMEM_LIMIT),
        name="sample_mix",
    )(*args)


def _rope_tables(pos0, T, hd):
    half = hd // 2
    pos = pos0 + np.arange(T, dtype=np.float64)
    inv = ROPE_BASE ** (-np.arange(half, dtype=np.float64) / half)
    ang = pos[:, None] * inv[None, :]
    cos, sin = np.cos(ang), np.sin(ang)
    return (jnp.asarray(np.concatenate([cos, cos], axis=-1), F32),
            jnp.asarray(np.concatenate([-sin, sin], axis=-1), F32))


def kernel(x_prompt, x_sample, cache_conv, state_ret, norm1_g, w_in, conv_w, conv_b, conv_ln_g,
           conv_ln_b, ret_gn_g, w_out, norm2_g, w_up, w_down, final_norm_g):
    depth = w_in.shape[0]
    B, T, D = x_prompt.shape
    S, ts, _ = x_sample.shape
    heads, hd = state_ret.shape[2], state_ret.shape[3]

    cvec = jnp.stack([conv_b, conv_ln_g, conv_ln_b], axis=1)
    gf = final_norm_g[None, :]
    cos_p, sin_p = _rope_tables(0.0, T, hd)
    cos_s, sin_s = _rope_tables(float(PAST_LEN), ts, hd)

    xp = x_prompt.reshape(B * T, D)
    xs = x_sample.transpose(1, 0, 2).reshape(ts * S, D)
    cache_t = cache_conv.transpose(0, 2, 1, 3)
    prompt_state, sample_state = None, None
    for l in range(depth):
        final = l == depth - 1
        g1, g2, gn = norm1_g[l][None, :], norm2_g[l][None, :], ret_gn_g[l][None, :]

        proj, w_in_b = _in_proj(xs, g1, w_in, l, IN_PROJ_COLS)
        mix_s, *sample_state = _sample_mix(proj.reshape(ts, S, -1), cache_t, state_ret, l, sample_state,
                                           conv_w[l], cvec[l], gn, cos_s, sin_s)

        mix_p, *rest = _prompt_mix(xp.reshape(B, T, D), g1, w_in_b, l, prompt_state, conv_w[l], cvec[l], gn,
                                   cos_p, sin_p, heads, hd, (w_out, w_up, w_down))
        prompt_state, post_w = rest[:2], rest[2:]

        w_out_b, w_up_b, w_dn_b = post_w
        xp, xs = _post(xp, mix_p.reshape(B * T, D), xs, mix_s.reshape(ts * S, D), w_out_b, g2,
                       w_up_b, w_dn_b, gf, final)

    new_cache_t, new_state = sample_state
    return (xp.reshape(B, T, D), xs.reshape(ts, S, D).transpose(1, 0, 2), *prompt_state,
            new_cache_t.transpose(0, 2, 1, 3), new_state)
```

```python
import functools
import math

import numpy as np
import jax
import jax.numpy as jnp
from jax import lax
from jax.experimental import pallas as pl
from jax.experimental.pallas import tpu as pltpu

F32 = jnp.float32
BF16 = jnp.bfloat16

EPS = 1e-6
ROPE_BASE = 10000.0
PAST_LEN = 16384
LANES = 128
SUBLANES = 8
VMEM_LIMIT = 62 * 1024 * 1024

PROMPT_CHUNK = 1024
RET_CHUNK = 512
CAUSAL_ROWS = 256
CONV_ROWS = 128
GLU_LANE_BLOCKS = 2
POST_ROWS = 1024
MLP_CHUNKS = 4
SAMPLE_SEQS = 8
STATE_BUFS = 4
IN_PROJ_COLS = 1536


def _log_gamma(h):
    return math.log(1.0 - 2.0 ** (-5.0 - h))


def _rmsnorm(x, g):
    ms = jnp.mean(x * x, axis=-1, keepdims=True)
    return x * lax.rsqrt(ms + EPS) * g


def _layernorm(x):
    mu = jnp.mean(x, axis=-1, keepdims=True)
    xc = x - mu
    var = jnp.mean(xc * xc, axis=-1, keepdims=True)
    return xc * lax.rsqrt(var + EPS)


def _silu(x):
    return x * jax.nn.sigmoid(x)


def _rotary(t, cos2, sin2):
    half = t.shape[-1] // 2
    return t * cos2 + pltpu.roll(t, half, t.ndim - 1) * sin2


def _const_spec(shape):
    nd = len(shape)
    return pl.BlockSpec(shape, lambda *_: (0,) * nd, pipeline_mode=pl.Buffered(1))


def _decay_tables(n, heads):
    idx = np.arange(n, dtype=np.float64)
    diff = idx[:, None] - idx[None, :]
    lg = np.array([_log_gamma(h) for h in range(heads)])
    dec = np.where(diff[None] >= 0, np.exp(lg[:, None, None] * np.maximum(diff, 0.0)[None]), 0.0)
    rdec = np.exp(lg[:, None] * (idx + 1.0))[:, :, None]
    kdec = np.exp(lg[:, None] * (n - 1.0 - idx))[:, :, None]
    return dec, rdec, kdec


def _prompt_mix_kernel(x_ref, g1_ref, win_ref, cw_ref, cvec_ref, gn_ref, cos_ref, sin_ref,
                       dec_ref, rdec_ref, kdec_ref, wo_ref, wu_ref, wd_ref, *rest,
                       tq, sub, heads, hd, cdim, cwidth, hist):
    n_out, n_scratch = 6, 8
    mix_ref, nconv_ref, nret_ref, wob_ref, wub_ref, wdb_ref = rest[-(n_out + n_scratch):-n_scratch]
    glu_s, shift_s, conv_s, q_s, k_s, v_s, g_s, state_s = rest[-n_scratch:]
    c = pl.program_id(1)
    last = pl.num_programs(1) - 1
    rdim = heads * hd

    @pl.when(c == 0)
    def _():
        glu_s[0:hist, :] = jnp.zeros((hist, cdim), F32)
        shift_s[:, 0:hist, :] = jnp.zeros((SUBLANES - 1, hist, cdim), F32)
        shift_s[:, tq:tq + hist, :] = jnp.zeros((SUBLANES - 1, hist, cdim), F32)
        state_s[...] = jnp.zeros_like(state_s)

    @pl.when(c > 0)
    def _():
        glu_s[0:hist, :] = glu_s[tq:tq + hist, :]
        shift_s[:, 0:hist, :] = shift_s[:, tq:tq + hist, :]

    wob_ref[...] = wo_ref[...].astype(BF16)
    wub_ref[...] = wu_ref[...].astype(BF16)
    wdb_ref[...] = wd_ref[...].astype(BF16)

    h = _rmsnorm(x_ref[0], g1_ref[...]).astype(BF16)

    def proj(lo, n):
        return jnp.dot(h, win_ref[:, lo:lo + n], preferred_element_type=F32)

    off = hist - (cwidth - 1)
    for lb in range(cdim // LANES):
        ls = slice(lb * LANES, (lb + 1) * LANES)
        if lb % GLU_LANE_BLOCKS == 0:
            n = GLU_LANE_BLOCKS * LANES
            cs = slice(lb * LANES, lb * LANES + n)
            glu = proj(lb * LANES, n) * jax.nn.sigmoid(proj(cdim + lb * LANES, n))
            glu_s[hist:hist + tq, cs] = glu
            for ph in range(1, SUBLANES):
                shift_s[ph - 1, hist - ph:hist - ph + tq, cs] = glu
        for r0 in range(0, tq, CONV_ROWS):
            acc = None
            for w in range(cwidth):
                ph = (off + w) % SUBLANES
                a = r0 + (off + w) // SUBLANES * SUBLANES
                xs = glu_s[a:a + CONV_ROWS, ls] if ph == 0 else shift_s[ph - 1, a:a + CONV_ROWS, ls]
                term = xs * cw_ref[w:w + 1, ls]
                acc = term if acc is None else acc + term
            conv_s[r0:r0 + CONV_ROWS, ls] = acc + cvec_ref[0:1, ls]
    for r0 in range(0, tq, CONV_ROWS):
        cn = _layernorm(conv_s[r0:r0 + CONV_ROWS, :]) * cvec_ref[1:2, :] + cvec_ref[2:3, :]
        mix_ref[0, r0:r0 + CONV_ROWS, 0:cdim] = _silu(cn).astype(BF16)

    c0 = 2 * cdim
    qf = proj(c0, rdim)
    kf = proj(c0 + rdim, rdim)
    cos2 = cos_ref[...]
    sin2 = sin_ref[...]
    scale = hd ** -0.5
    for hh in range(heads):
        hs = slice(hh * hd, (hh + 1) * hd)
        q_s[:, hs] = _rotary(qf[:, hs], cos2, sin2).astype(BF16)
        k_s[:, hs] = _rotary(kf[:, hs], cos2, sin2) * scale
    v_s[...] = proj(c0 + 2 * rdim, rdim).astype(BF16)
    g_s[...] = _silu(proj(c0 + 3 * rdim, rdim))

    for hh in range(heads):
        hs = slice(hh * hd, (hh + 1) * hd)
        st = state_s[hh]
        for r0 in range(0, tq, sub):
            rs = slice(r0, r0 + sub)
            qh = q_s[rs, hs]
            kh = k_s[rs, hs]
            vh = v_s[rs, hs]
            kb = kh.astype(BF16)
            parts = []
            for a in range(0, sub, CAUSAL_ROWS):
                e = a + CAUSAL_ROWS
                sc = lax.dot_general(qh[a:e], kb[0:e], (((1,), (1,)), ((), ())),
                                     preferred_element_type=F32) * dec_ref[hh, a:e, 0:e]
                parts.append(jnp.dot(sc.astype(BF16), vh[0:e], preferred_element_type=F32))
            inner = jnp.concatenate(parts, axis=0)
            cross = jnp.dot(qh, st.astype(BF16), preferred_element_type=F32) * rdec_ref[hh]
            kd = (kh * kdec_ref[hh]).astype(BF16)
            upd = lax.dot_general(kd, vh, (((0,), (0,)), ((), ())), preferred_element_type=F32)
            st = st * math.exp(_log_gamma(hh) * sub) + upd
            on = _layernorm(inner + cross)
            mix_ref[0, rs, cdim + hh * hd:cdim + (hh + 1) * hd] = (
                on * gn_ref[0:1, hs] * g_s[rs, hs]).astype(BF16)
        state_s[hh] = st

    @pl.when(c == last)
    def _():
        nconv_ref[0] = glu_s[hist + tq - (cwidth - 1):hist + tq, :]
        nret_ref[0] = state_s[...]


def _prompt_mix(x, g1, w_in, layer, prev, conv_w, cvec, gn_g, cos2, sin2, heads, hd, post_weights):
    B, T, D = x.shape
    depth = post_weights[0].shape[0]
    cwidth, cdim = conv_w.shape
    rdim = heads * hd
    tq = PROMPT_CHUNK
    sub = RET_CHUNK
    hist = -(-(cwidth - 1) // SUBLANES) * SUBLANES
    assert T % tq == 0 and tq >= hist and tq % CONV_ROWS == 0 and tq % sub == 0 and hd == LANES
    assert sub % CAUSAL_ROWS == 0

    dec, rdec, kdec = _decay_tables(sub, heads)
    dec = jnp.asarray(dec, F32)
    rdec = jnp.asarray(np.broadcast_to(rdec, (heads, sub, hd)), F32)
    kdec = jnp.asarray(np.broadcast_to(kdec, (heads, sub, hd)), F32)

    chunks = T // tq
    steps = B * chunks
    for w in post_weights:
        assert w.shape[1] % (steps * 2 * SUBLANES) == 0
    slab_in = [pl.BlockSpec((None, w.shape[1] // steps, w.shape[2]), lambda b, c: (layer, b * chunks + c, 0))
               for w in post_weights]
    slab_out = [pl.BlockSpec((w.shape[1] // steps, w.shape[2]), lambda b, c: (b * chunks + c, 0))
                for w in post_weights]

    args = [x, g1, w_in, conv_w, cvec, gn_g, cos2, sin2, dec, rdec, kdec, *post_weights]
    in_specs = [
        pl.BlockSpec((1, tq, D), lambda b, c: (b, c, 0)),
        _const_spec((1, D)),
        _const_spec(w_in.shape),
        _const_spec(conv_w.shape),
        _const_spec(cvec.shape),
        _const_spec((1, rdim)),
        pl.BlockSpec((tq, hd), lambda b, c: (c, 0)),
        pl.BlockSpec((tq, hd), lambda b, c: (c, 0)),
        _const_spec(dec.shape),
        _const_spec(rdec.shape),
        _const_spec(kdec.shape),
        *slab_in,
    ]
    aliases = {}
    if prev is not None:
        aliases = {len(args): 1, len(args) + 1: 2}
        args += list(prev)
        in_specs += [pl.BlockSpec(memory_space=pl.ANY)] * 2

    kern = functools.partial(_prompt_mix_kernel, tq=tq, sub=sub, heads=heads, hd=hd, cdim=cdim,
                             cwidth=cwidth, hist=hist)
    return pl.pallas_call(
        kern,
        grid=(B, chunks),
        in_specs=in_specs,
        out_specs=[
            pl.BlockSpec((1, tq, D), lambda b, c: (b, c, 0)),
            pl.BlockSpec((None, 1, cwidth - 1, cdim), lambda b, c: (layer, b, 0, 0)),
            pl.BlockSpec((None, 1, heads, hd, hd), lambda b, c: (layer, b, 0, 0, 0)),
            *slab_out,
        ],
        out_shape=[
            jax.ShapeDtypeStruct((B, T, D), BF16),
            jax.ShapeDtypeStruct((depth, B, cwidth - 1, cdim), F32),
            jax.ShapeDtypeStruct((depth, B, heads, hd, hd), F32),
            *[jax.ShapeDtypeStruct(w.shape[1:], BF16) for w in post_weights],
        ],
        input_output_aliases=aliases,
        scratch_shapes=[
            pltpu.VMEM((hist + tq, cdim), F32),
            pltpu.VMEM((SUBLANES - 1, hist + tq, cdim), F32),
            pltpu.VMEM((tq, cdim), F32),
            pltpu.VMEM((tq, rdim), BF16),
            pltpu.VMEM((tq, rdim), F32),
            pltpu.VMEM((tq, rdim), BF16),
            pltpu.VMEM((tq, rdim), F32),
            pltpu.VMEM((heads, hd, hd), F32),
        ],
        compiler_params=pltpu.CompilerParams(
            dimension_semantics=("arbitrary", "arbitrary"), vmem_limit_bytes=VMEM_LIMIT),
        name="prompt_mix",
    )(*args)


def _post_kernel(xa_ref, mixa_ref, xb_ref, mixb_ref, wout_ref, g2_ref, wup_ref, wdn_ref, gf_ref,
                 oa_ref, ob_ref, *, final, steps_a):
    def tile(x_ref, mix_ref, o_ref):
        y = x_ref[...] + jnp.dot(mix_ref[...], wout_ref[...], preferred_element_type=F32)
        h2 = _rmsnorm(y, g2_ref[...]).astype(BF16)
        dff = wup_ref.shape[1]
        hc = dff // MLP_CHUNKS
        out = y
        for k in range(MLP_CHUNKS):
            hf = jnp.dot(h2, wup_ref[:, k * hc:(k + 1) * hc], preferred_element_type=F32)
            act = jnp.square(jnp.maximum(hf, 0.0)).astype(BF16)
            out = out + jnp.dot(act, wdn_ref[k * hc:(k + 1) * hc, :], preferred_element_type=F32)
        if final:
            out = _rmsnorm(out, gf_ref[...])
        o_ref[...] = out

    i = pl.program_id(0)

    @pl.when(i < steps_a)
    def _():
        tile(xa_ref, mixa_ref, oa_ref)

    @pl.when(i >= steps_a)
    def _():
        tile(xb_ref, mixb_ref, ob_ref)


def _post(xa, mixa, xb, mixb, w_out, g2, w_up, w_down, gf, final):
    Ra, D = xa.shape
    Rb = xb.shape[0]
    tra, trb = min(POST_ROWS, Ra), min(POST_ROWS, Rb)
    assert Ra % tra == 0 and Rb % trb == 0
    na, nb = Ra // tra, Rb // trb
    rows_a = pl.BlockSpec((tra, D), lambda i: (jnp.minimum(i, na - 1), 0))
    rows_b = pl.BlockSpec((trb, D), lambda i: (jnp.maximum(i - na, 0), 0))

    return pl.pallas_call(
        functools.partial(_post_kernel, final=final, steps_a=na),
        grid=(na + nb,),
        in_specs=[
            rows_a, rows_a, rows_b, rows_b,
            _const_spec(w_out.shape),
            _const_spec((1, D)),
            _const_spec(w_up.shape),
            _const_spec(w_down.shape),
            _const_spec((1, D)),
        ],
        out_specs=[rows_a, rows_b],
        out_shape=[jax.ShapeDtypeStruct((Ra, D), F32), jax.ShapeDtypeStruct((Rb, D), F32)],
        compiler_params=pltpu.CompilerParams(
            dimension_semantics=("arbitrary",), vmem_limit_bytes=VMEM_LIMIT),
        name="post",
    )(xa, mixa, xb, mixb, w_out, g2, w_up, w_down, gf)


def _in_proj_kernel(x_ref, g1_ref, w_ref, o_ref, wb_ref, h_s):
    @pl.when(pl.program_id(0) == 0)
    def _():
        h_s[...] = _rmsnorm(x_ref[...], g1_ref[...]).astype(BF16)

    wb_ref[...] = w_ref[...].astype(BF16)
    o_ref[...] = jnp.dot(h_s[...], wb_ref[...], preferred_element_type=F32)


def _in_proj(x, g1, w_in, layer, ncol):
    R, D = x.shape
    C = w_in.shape[2]
    return pl.pallas_call(
        _in_proj_kernel,
        grid=(C // ncol,),
        in_specs=[
            _const_spec((R, D)),
            _const_spec((1, D)),
            pl.BlockSpec((None, D, ncol), lambda j: (layer, 0, j)),
        ],
        out_specs=[pl.BlockSpec((R, ncol), lambda j: (0, j)), pl.BlockSpec((D, ncol), lambda j: (0, j))],
        out_shape=[jax.ShapeDtypeStruct((R, C), F32), jax.ShapeDtypeStruct((D, C), BF16)],
        scratch_shapes=[pltpu.VMEM((R, D), BF16)],
        compiler_params=pltpu.CompilerParams(
            dimension_semantics=("arbitrary",), vmem_limit_bytes=VMEM_LIMIT),
        name="sample_in_proj",
    )(x, g1, w_in)


def _sample_mix_kernel(proj_ref, cache_ref, st_ref, cw_ref, cvec_ref, gn_ref, cos_ref, sin_ref,
                       dec_ref, rdec_ref, kdec_ref, *rest, layer, ts, nb, heads, hd, cdim, cwidth):
    mix_ref, ncache_ref, nst_ref, st_buf, st_sem = rest[-5:]
    rdim = heads * hd
    i = pl.program_id(0)
    steps = pl.num_programs(0)

    def fetch(step):
        slot = step % STATE_BUFS
        return pltpu.make_async_copy(st_ref.at[layer, pl.ds(step * nb, nb)], st_buf.at[slot], st_sem.at[slot])

    @pl.when(i == 0)
    def _():
        for s in range(STATE_BUFS - 1):
            fetch(s).start()

    @pl.when(i + (STATE_BUFS - 1) < steps)
    def _():
        fetch(i + (STATE_BUFS - 1)).start()

    fetch(i).wait()
    st_cur = st_buf.at[i % STATE_BUFS]

    glu = [proj_ref[t, :, 0:cdim] * jax.nn.sigmoid(proj_ref[t, :, cdim:2 * cdim]) for t in range(ts)]

    def window(r):
        return cache_ref[r] if r < cwidth - 1 else glu[r - (cwidth - 1)]

    acc = [None] * ts
    for r in range(cwidth - 1 + ts):
        x = window(r)
        for t in range(max(0, r - cwidth + 1), min(ts, r + 1)):
            term = x * cw_ref[r - t:r - t + 1, :]
            acc[t] = term if acc[t] is None else acc[t] + term
    for t in range(ts):
        cn = _layernorm(acc[t] + cvec_ref[0:1, :]) * cvec_ref[1:2, :] + cvec_ref[2:3, :]
        mix_ref[t, :, 0:cdim] = _silu(cn).astype(BF16)
    for r in range(cwidth - 1):
        ncache_ref[r] = window(r + ts)

    c0 = 2 * cdim
    cos2 = cos_ref[...]
    sin2 = sin_ref[...]
    scale = hd ** -0.5
    for hh in range(heads):
        lo = hh * hd

        def per_seq(base):
            return jnp.stack([proj_ref[:, s, base + lo:base + lo + hd] for s in range(nb)], axis=0)

        q = _rotary(per_seq(c0), cos2, sin2)
        k = _rotary(per_seq(c0 + rdim), cos2, sin2) * scale
        v = per_seq(c0 + 2 * rdim)
        g = per_seq(c0 + 3 * rdim)
        st = st_cur[:, hh]
        scores = jnp.einsum('btd,bjd->btj', q, k, preferred_element_type=F32) * dec_ref[hh]
        inner = jnp.einsum('btj,bjv->btv', scores, v, preferred_element_type=F32)
        cross = jnp.einsum('btd,bdv->btv', q, st, preferred_element_type=F32) * rdec_ref[hh]
        upd = jnp.einsum('bjd,bjv->bdv', k * kdec_ref[hh], v, preferred_element_type=F32)
        nst_ref[:, hh] = st * math.exp(_log_gamma(hh) * ts) + upd
        on = _layernorm(inner + cross)
        out = (on * gn_ref[0:1, lo:lo + hd] * _silu(g)).astype(BF16)
        for s in range(nb):
            mix_ref[:, s, cdim + lo:cdim + lo + hd] = out[s]


def _sample_mix(proj, cache, state, layer, prev, conv_w, cvec, gn_g, cos2, sin2):
    ts, S, C = proj.shape
    depth, _, heads, hd, _ = state.shape
    cwidth, cdim = conv_w.shape
    rdim = heads * hd
    D = cdim + rdim
    nb = SAMPLE_SEQS
    assert S % nb == 0 and S // nb >= STATE_BUFS - 1 and hd == LANES

    dec, rdec, kdec = (jnp.asarray(t, F32) for t in _decay_tables(ts, heads))

    conv_spec = pl.BlockSpec((None, cwidth - 1, nb, cdim), lambda i: (layer, 0, i, 0))
    st_spec = pl.BlockSpec((None, nb, heads, hd, hd), lambda i: (layer, i, 0, 0, 0))
    args = [proj, cache, state, conv_w, cvec, gn_g, cos2, sin2, dec, rdec, kdec]
    in_specs = [
        pl.BlockSpec((ts, nb, C), lambda i: (0, i, 0)),
        conv_spec,
        pl.BlockSpec(memory_space=pl.ANY),
        _const_spec(conv_w.shape),
        _const_spec(cvec.shape),
        _const_spec((1, rdim)),
        _const_spec((ts, hd)),
        _const_spec((ts, hd)),
        _const_spec(dec.shape),
        _const_spec(rdec.shape),
        _const_spec(kdec.shape),
    ]
    aliases = {}
    if prev is not None:
        aliases = {len(args): 1, len(args) + 1: 2}
        args += list(prev)
        in_specs += [pl.BlockSpec(memory_space=pl.ANY)] * 2

    kern = functools.partial(_sample_mix_kernel, layer=layer, ts=ts, nb=nb, heads=heads, hd=hd, cdim=cdim,
                             cwidth=cwidth)
    return pl.pallas_call(
        kern,
        grid=(S // nb,),
        in_specs=in_specs,
        out_specs=[pl.BlockSpec((ts, nb, D), lambda i: (0, i, 0)), conv_spec, st_spec],
        out_shape=[
            jax.ShapeDtypeStruct((ts, S, D), BF16),
            jax.ShapeDtypeStruct((depth, cwidth - 1, S, cdim), F32),
            jax.ShapeDtypeStruct((depth, S, heads, hd, hd), F32),
        ],
        scratch_shapes=[
            pltpu.VMEM((STATE_BUFS, nb, heads, hd, hd), F32),
            pltpu.SemaphoreType.DMA((STATE_BUFS,)),
        ],
        input_output_aliases=aliases,
        compiler_params=pltpu.CompilerParams(
            dimension_semantics=("arbitrary",), vmem_limit_bytes=VMEM_LIMIT),
        name="sample_mix",
    )(*args)


def _rope_tables(pos0, T, hd):
    half = hd // 2
    pos = pos0 + np.arange(T, dtype=np.float64)
    inv = ROPE_BASE ** (-np.arange(half, dtype=np.float64) / half)
    ang = pos[:, None] * inv[None, :]
    cos, sin = np.cos(ang), np.sin(ang)
    return (jnp.asarray(np.concatenate([cos, cos], axis=-1), F32),
            jnp.asarray(np.concatenate([-sin, sin], axis=-1), F32))


def kernel(x_prompt, x_sample, cache_conv, state_ret, norm1_g, w_in, conv_w, conv_b, conv_ln_g,
           conv_ln_b, ret_gn_g, w_out, norm2_g, w_up, w_down, final_norm_g):
    depth = w_in.shape[0]
    B, T, D = x_prompt.shape
    S, ts, _ = x_sample.shape
    heads, hd = state_ret.shape[2], state_ret.shape[3]

    cvec = jnp.stack([conv_b, conv_ln_g, conv_ln_b], axis=1)
    gf = final_norm_g[None, :]
    cos_p, sin_p = _rope_tables(0.0, T, hd)
    cos_s, sin_s = _rope_tables(float(PAST_LEN), ts, hd)

    xp = x_prompt.reshape(B * T, D)
    xs = x_sample.transpose(1, 0, 2).reshape(ts * S, D)
    cache_t = cache_conv.transpose(0, 2, 1, 3)
    prompt_state, sample_state = None, None
    for l in range(depth):
        final = l == depth - 1
        g1, g2, gn = norm1_g[l][None, :], norm2_g[l][None, :], ret_gn_g[l][None, :]

        proj, w_in_b = _in_proj(xs, g1, w_in, l, IN_PROJ_COLS)
        mix_s, *sample_state = _sample_mix(proj.reshape(ts, S, -1), cache_t, state_ret, l, sample_state,
                                           conv_w[l], cvec[l], gn, cos_s, sin_s)

        mix_p, *rest = _prompt_mix(xp.reshape(B, T, D), g1, w_in_b, l, prompt_state, conv_w[l], cvec[l], gn,
                                   cos_p, sin_p, heads, hd, (w_out, w_up, w_down))
        prompt_state, post_w = rest[:2], rest[2:]

        w_out_b, w_up_b, w_dn_b = post_w
        xp, xs = _post(xp, mix_p.reshape(B * T, D), xs, mix_s.reshape(ts * S, D), w_out_b, g2,
                       w_up_b, w_dn_b, gf, final)

    new_cache_t, new_state = sample_state
    return (xp.reshape(B, T, D), xs.reshape(ts, S, D).transpose(1, 0, 2), *prompt_state,
            new_cache_t.transpose(0, 2, 1, 3), new_state)
```

```python
import functools
import math

import numpy as np
import jax
import jax.numpy as jnp
from jax import lax
from jax.experimental import pallas as pl
from jax.experimental.pallas import tpu as pltpu

F32 = jnp.float32
BF16 = jnp.bfloat16

EPS = 1e-6
ROPE_BASE = 10000.0
PAST_LEN = 16384
LANES = 128
SUBLANES = 8
VMEM_LIMIT = 62 * 1024 * 1024

PROMPT_CHUNK = 1024
RET_CHUNK = 512
CAUSAL_ROWS = 256
CONV_ROWS = 128
GLU_LANE_BLOCKS = 2
POST_ROWS = 1024
MLP_CHUNKS = 4
SAMPLE_SEQS = 16
STATE_BUFS = 4
IN_PROJ_COLS = 1536


def _log_gamma(h):
    return math.log(1.0 - 2.0 ** (-5.0 - h))


def _rmsnorm(x, g):
    ms = jnp.mean(x * x, axis=-1, keepdims=True)
    return x * lax.rsqrt(ms + EPS) * g


def _layernorm(x):
    mu = jnp.mean(x, axis=-1, keepdims=True)
    xc = x - mu
    var = jnp.mean(xc * xc, axis=-1, keepdims=True)
    return xc * lax.rsqrt(var + EPS)


def _silu(x):
    return x * jax.nn.sigmoid(x)


def _rotary(t, cos2, sin2):
    half = t.shape[-1] // 2
    return t * cos2 + pltpu.roll(t, half, t.ndim - 1) * sin2


def _const_spec(shape):
    nd = len(shape)
    return pl.BlockSpec(shape, lambda *_: (0,) * nd, pipeline_mode=pl.Buffered(1))


def _decay_tables(n, heads):
    idx = np.arange(n, dtype=np.float64)
    diff = idx[:, None] - idx[None, :]
    lg = np.array([_log_gamma(h) for h in range(heads)])
    dec = np.where(diff[None] >= 0, np.exp(lg[:, None, None] * np.maximum(diff, 0.0)[None]), 0.0)
    rdec = np.exp(lg[:, None] * (idx + 1.0))[:, :, None]
    kdec = np.exp(lg[:, None] * (n - 1.0 - idx))[:, :, None]
    return dec, rdec, kdec


def _prompt_mix_kernel(x_ref, g1_ref, win_ref, cw_ref, cvec_ref, gn_ref, cos_ref, sin_ref,
                       dec_ref, rdec_ref, kdec_ref, wo_ref, wu_ref, wd_ref, *rest,
                       tq, sub, heads, hd, cdim, cwidth, hist):
    n_out, n_scratch = 6, 8
    mix_ref, nconv_ref, nret_ref, wob_ref, wub_ref, wdb_ref = rest[-(n_out + n_scratch):-n_scratch]
    glu_s, shift_s, conv_s, q_s, k_s, v_s, g_s, state_s = rest[-n_scratch:]
    c = pl.program_id(1)
    last = pl.num_programs(1) - 1
    rdim = heads * hd

    @pl.when(c == 0)
    def _():
        glu_s[0:hist, :] = jnp.zeros((hist, cdim), F32)
        shift_s[:, 0:hist, :] = jnp.zeros((SUBLANES - 1, hist, cdim), F32)
        shift_s[:, tq:tq + hist, :] = jnp.zeros((SUBLANES - 1, hist, cdim), F32)
        state_s[...] = jnp.zeros_like(state_s)

    @pl.when(c > 0)
    def _():
        glu_s[0:hist, :] = glu_s[tq:tq + hist, :]
        shift_s[:, 0:hist, :] = shift_s[:, tq:tq + hist, :]

    wob_ref[...] = wo_ref[...].astype(BF16)
    wub_ref[...] = wu_ref[...].astype(BF16)
    wdb_ref[...] = wd_ref[...].astype(BF16)

    h = _rmsnorm(x_ref[0], g1_ref[...]).astype(BF16)

    def proj(lo, n):
        return jnp.dot(h, win_ref[:, lo:lo + n], preferred_element_type=F32)

    off = hist - (cwidth - 1)
    for lb in range(cdim // LANES):
        ls = slice(lb * LANES, (lb + 1) * LANES)
        if lb % GLU_LANE_BLOCKS == 0:
            n = GLU_LANE_BLOCKS * LANES
            cs = slice(lb * LANES, lb * LANES + n)
            glu = proj(lb * LANES, n) * jax.nn.sigmoid(proj(cdim + lb * LANES, n))
            glu_s[hist:hist + tq, cs] = glu
            for ph in range(1, SUBLANES):
                shift_s[ph - 1, hist - ph:hist - ph + tq, cs] = glu
        for r0 in range(0, tq, CONV_ROWS):
            acc = None
            for w in range(cwidth):
                ph = (off + w) % SUBLANES
                a = r0 + (off + w) // SUBLANES * SUBLANES
                xs = glu_s[a:a + CONV_ROWS, ls] if ph == 0 else shift_s[ph - 1, a:a + CONV_ROWS, ls]
                term = xs * cw_ref[w:w + 1, ls]
                acc = term if acc is None else acc + term
            conv_s[r0:r0 + CONV_ROWS, ls] = acc + cvec_ref[0:1, ls]
    for r0 in range(0, tq, CONV_ROWS):
        cn = _layernorm(conv_s[r0:r0 + CONV_ROWS, :]) * cvec_ref[1:2, :] + cvec_ref[2:3, :]
        mix_ref[0, r0:r0 + CONV_ROWS, 0:cdim] = _silu(cn).astype(BF16)

    c0 = 2 * cdim
    qf = proj(c0, rdim)
    kf = proj(c0 + rdim, rdim)
    cos2 = cos_ref[...]
    sin2 = sin_ref[...]
    scale = hd ** -0.5
    for hh in range(heads):
        hs = slice(hh * hd, (hh + 1) * hd)
        q_s[:, hs] = _rotary(qf[:, hs], cos2, sin2).astype(BF16)
        k_s[:, hs] = _rotary(kf[:, hs], cos2, sin2) * scale
    v_s[...] = proj(c0 + 2 * rdim, rdim).astype(BF16)
    g_s[...] = _silu(proj(c0 + 3 * rdim, rdim))

    for hh in range(heads):
        hs = slice(hh * hd, (hh + 1) * hd)
        st = state_s[hh]
        for r0 in range(0, tq, sub):
            rs = slice(r0, r0 + sub)
            qh = q_s[rs, hs]
            kh = k_s[rs, hs]
            vh = v_s[rs, hs]
            kb = kh.astype(BF16)
            parts = []
            for a in range(0, sub, CAUSAL_ROWS):
                e = a + CAUSAL_ROWS
                sc = lax.dot_general(qh[a:e], kb[0:e], (((1,), (1,)), ((), ())),
                                     preferred_element_type=F32) * dec_ref[hh, a:e, 0:e]
                parts.append(jnp.dot(sc.astype(BF16), vh[0:e], preferred_element_type=F32))
            inner = jnp.concatenate(parts, axis=0)
            cross = jnp.dot(qh, st.astype(BF16), preferred_element_type=F32) * rdec_ref[hh]
            kd = (kh * kdec_ref[hh]).astype(BF16)
            upd = lax.dot_general(kd, vh, (((0,), (0,)), ((), ())), preferred_element_type=F32)
            st = st * math.exp(_log_gamma(hh) * sub) + upd
            on = _layernorm(inner + cross)
            mix_ref[0, rs, cdim + hh * hd:cdim + (hh + 1) * hd] = (
                on * gn_ref[0:1, hs] * g_s[rs, hs]).astype(BF16)
        state_s[hh] = st

    @pl.when(c == last)
    def _():
        nconv_ref[0] = glu_s[hist + tq - (cwidth - 1):hist + tq, :]
        nret_ref[0] = state_s[...]


def _prompt_mix(x, g1, w_in, layer, prev, conv_w, cvec, gn_g, cos2, sin2, heads, hd, post_weights):
    B, T, D = x.shape
    depth = post_weights[0].shape[0]
    cwidth, cdim = conv_w.shape
    rdim = heads * hd
    tq = PROMPT_CHUNK
    sub = RET_CHUNK
    hist = -(-(cwidth - 1) // SUBLANES) * SUBLANES
    assert T % tq == 0 and tq >= hist and tq % CONV_ROWS == 0 and tq % sub == 0 and hd == LANES
    assert sub % CAUSAL_ROWS == 0

    dec, rdec, kdec = _decay_tables(sub, heads)
    dec = jnp.asarray(dec, F32)
    rdec = jnp.asarray(np.broadcast_to(rdec, (heads, sub, hd)), F32)
    kdec = jnp.asarray(np.broadcast_to(kdec, (heads, sub, hd)), F32)

    chunks = T // tq
    steps = B * chunks
    for w in post_weights:
        assert w.shape[1] % (steps * 2 * SUBLANES) == 0
    slab_in = [pl.BlockSpec((None, w.shape[1] // steps, w.shape[2]), lambda b, c: (layer, b * chunks + c, 0))
               for w in post_weights]
    slab_out = [pl.BlockSpec((w.shape[1] // steps, w.shape[2]), lambda b, c: (b * chunks + c, 0))
                for w in post_weights]

    args = [x, g1, w_in, conv_w, cvec, gn_g, cos2, sin2, dec, rdec, kdec, *post_weights]
    in_specs = [
        pl.BlockSpec((1, tq, D), lambda b, c: (b, c, 0)),
        _const_spec((1, D)),
        _const_spec(w_in.shape),
        _const_spec(conv_w.shape),
        _const_spec(cvec.shape),
        _const_spec((1, rdim)),
        pl.BlockSpec((tq, hd), lambda b, c: (c, 0)),
        pl.BlockSpec((tq, hd), lambda b, c: (c, 0)),
        _const_spec(dec.shape),
        _const_spec(rdec.shape),
        _const_spec(kdec.shape),
        *slab_in,
    ]
    aliases = {}
    if prev is not None:
        aliases = {len(args): 1, len(args) + 1: 2}
        args += list(prev)
        in_specs += [pl.BlockSpec(memory_space=pl.ANY)] * 2

    kern = functools.partial(_prompt_mix_kernel, tq=tq, sub=sub, heads=heads, hd=hd, cdim=cdim,
                             cwidth=cwidth, hist=hist)
    return pl.pallas_call(
        kern,
        grid=(B, chunks),
        in_specs=in_specs,
        out_specs=[
            pl.BlockSpec((1, tq, D), lambda b, c: (b, c, 0)),
            pl.BlockSpec((None, 1, cwidth - 1, cdim), lambda b, c: (layer, b, 0, 0)),
            pl.BlockSpec((None, 1, heads, hd, hd), lambda b, c: (layer, b, 0, 0, 0)),
            *slab_out,
        ],
        out_shape=[
            jax.ShapeDtypeStruct((B, T, D), BF16),
            jax.ShapeDtypeStruct((depth, B, cwidth - 1, cdim), F32),
            jax.ShapeDtypeStruct((depth, B, heads, hd, hd), F32),
            *[jax.ShapeDtypeStruct(w.shape[1:], BF16) for w in post_weights],
        ],
        input_output_aliases=aliases,
        scratch_shapes=[
            pltpu.VMEM((hist + tq, cdim), F32),
            pltpu.VMEM((SUBLANES - 1, hist + tq, cdim), F32),
            pltpu.VMEM((tq, cdim), F32),
            pltpu.VMEM((tq, rdim), BF16),
            pltpu.VMEM((tq, rdim), F32),
            pltpu.VMEM((tq, rdim), BF16),
            pltpu.VMEM((tq, rdim), F32),
            pltpu.VMEM((heads, hd, hd), F32),
        ],
        compiler_params=pltpu.CompilerParams(
            dimension_semantics=("arbitrary", "arbitrary"), vmem_limit_bytes=VMEM_LIMIT),
        name="prompt_mix",
    )(*args)


def _post_kernel(xa_ref, mixa_ref, xb_ref, mixb_ref, wout_ref, g2_ref, wup_ref, wdn_ref, gf_ref,
                 oa_ref, ob_ref, *, final, steps_a):
    def tile(x_ref, mix_ref, o_ref):
        y = x_ref[...] + jnp.dot(mix_ref[...], wout_ref[...], preferred_element_type=F32)
        h2 = _rmsnorm(y, g2_ref[...]).astype(BF16)
        dff = wup_ref.shape[1]
        hc = dff // MLP_CHUNKS
        out = y
        for k in range(MLP_CHUNKS):
            hf = jnp.dot(h2, wup_ref[:, k * hc:(k + 1) * hc], preferred_element_type=F32)
            act = jnp.square(jnp.maximum(hf, 0.0)).astype(BF16)
            out = out + jnp.dot(act, wdn_ref[k * hc:(k + 1) * hc, :], preferred_element_type=F32)
        if final:
            out = _rmsnorm(out, gf_ref[...])
        o_ref[...] = out

    i = pl.program_id(0)

    @pl.when(i < steps_a)
    def _():
        tile(xa_ref, mixa_ref, oa_ref)

    @pl.when(i >= steps_a)
    def _():
        tile(xb_ref, mixb_ref, ob_ref)


def _post(xa, mixa, xb, mixb, w_out, g2, w_up, w_down, gf, final):
    Ra, D = xa.shape
    Rb = xb.shape[0]
    tra, trb = min(POST_ROWS, Ra), min(POST_ROWS, Rb)
    assert Ra % tra == 0 and Rb % trb == 0
    na, nb = Ra // tra, Rb // trb
    rows_a = pl.BlockSpec((tra, D), lambda i: (jnp.minimum(i, na - 1), 0))
    rows_b = pl.BlockSpec((trb, D), lambda i: (jnp.maximum(i - na, 0), 0))

    return pl.pallas_call(
        functools.partial(_post_kernel, final=final, steps_a=na),
        grid=(na + nb,),
        in_specs=[
            rows_a, rows_a, rows_b, rows_b,
            _const_spec(w_out.shape),
            _const_spec((1, D)),
            _const_spec(w_up.shape),
            _const_spec(w_down.shape),
            _const_spec((1, D)),
        ],
        out_specs=[rows_a, rows_b],
        out_shape=[jax.ShapeDtypeStruct((Ra, D), F32), jax.ShapeDtypeStruct((Rb, D), F32)],
        compiler_params=pltpu.CompilerParams(
            dimension_semantics=("arbitrary",), vmem_limit_bytes=VMEM_LIMIT),
        name="post",
    )(xa, mixa, xb, mixb, w_out, g2, w_up, w_down, gf)


def _in_proj_kernel(x_ref, g1_ref, w_ref, o_ref, wb_ref, h_s):
    @pl.when(pl.program_id(0) == 0)
    def _():
        h_s[...] = _rmsnorm(x_ref[...], g1_ref[...]).astype(BF16)

    wb_ref[...] = w_ref[...].astype(BF16)
    o_ref[...] = jnp.dot(h_s[...], wb_ref[...], preferred_element_type=F32)


def _in_proj(x, g1, w_in, layer, ncol):
    R, D = x.shape
    C = w_in.shape[2]
    return pl.pallas_call(
        _in_proj_kernel,
        grid=(C // ncol,),
        in_specs=[
            _const_spec((R, D)),
            _const_spec((1, D)),
            pl.BlockSpec((None, D, ncol), lambda j: (layer, 0, j)),
        ],
        out_specs=[pl.BlockSpec((R, ncol), lambda j: (0, j)), pl.BlockSpec((D, ncol), lambda j: (0, j))],
        out_shape=[jax.ShapeDtypeStruct((R, C), F32), jax.ShapeDtypeStruct((D, C), BF16)],
        scratch_shapes=[pltpu.VMEM((R, D), BF16)],
        compiler_params=pltpu.CompilerParams(
            dimension_semantics=("arbitrary",), vmem_limit_bytes=VMEM_LIMIT),
        name="sample_in_proj",
    )(x, g1, w_in)


def _sample_mix_kernel(proj_ref, cache_ref, st_ref, cw_ref, cvec_ref, gn_ref, cos_ref, sin_ref,
                       dec_ref, rdec_ref, kdec_ref, *rest, layer, ts, nb, heads, hd, cdim, cwidth):
    mix_ref, ncache_ref, nst_ref, st_buf, st_sem = rest[-5:]
    rdim = heads * hd
    i = pl.program_id(0)
    steps = pl.num_programs(0)

    def fetch(step):
        slot = step % STATE_BUFS
        return pltpu.make_async_copy(st_ref.at[layer, pl.ds(step * nb, nb)], st_buf.at[slot], st_sem.at[slot])

    @pl.when(i == 0)
    def _():
        for s in range(STATE_BUFS - 1):
            fetch(s).start()

    @pl.when(i + (STATE_BUFS - 1) < steps)
    def _():
        fetch(i + (STATE_BUFS - 1)).start()

    fetch(i).wait()
    st_cur = st_buf.at[i % STATE_BUFS]

    glu = [proj_ref[t, :, 0:cdim] * jax.nn.sigmoid(proj_ref[t, :, cdim:2 * cdim]) for t in range(ts)]

    def window(r):
        return cache_ref[r] if r < cwidth - 1 else glu[r - (cwidth - 1)]

    acc = [None] * ts
    for r in range(cwidth - 1 + ts):
        x = window(r)
        for t in range(max(0, r - cwidth + 1), min(ts, r + 1)):
            term = x * cw_ref[r - t:r - t + 1, :]
            acc[t] = term if acc[t] is None else acc[t] + term
    for t in range(ts):
        cn = _layernorm(acc[t] + cvec_ref[0:1, :]) * cvec_ref[1:2, :] + cvec_ref[2:3, :]
        mix_ref[t, :, 0:cdim] = _silu(cn).astype(BF16)
    for r in range(cwidth - 1):
        ncache_ref[r] = window(r + ts)

    c0 = 2 * cdim
    cos2 = cos_ref[...]
    sin2 = sin_ref[...]
    scale = hd ** -0.5
    for hh in range(heads):
        lo = hh * hd

        def per_seq(base):
            return jnp.stack([proj_ref[:, s, base + lo:base + lo + hd] for s in range(nb)], axis=0)

        q = _rotary(per_seq(c0), cos2, sin2)
        k = _rotary(per_seq(c0 + rdim), cos2, sin2) * scale
        v = per_seq(c0 + 2 * rdim)
        g = per_seq(c0 + 3 * rdim)
        st = st_cur[:, hh]
        scores = jnp.einsum('btd,bjd->btj', q, k, preferred_element_type=F32) * dec_ref[hh]
        inner = jnp.einsum('btj,bjv->btv', scores, v, preferred_element_type=F32)
        cross = jnp.einsum('btd,bdv->btv', q.astype(BF16), st.astype(BF16),
                           preferred_element_type=F32) * rdec_ref[hh]
        upd = jnp.einsum('bjd,bjv->bdv', k * kdec_ref[hh], v, preferred_element_type=F32)
        nst_ref[:, hh] = st * math.exp(_log_gamma(hh) * ts) + upd
        on = _layernorm(inner + cross)
        out = (on * gn_ref[0:1, lo:lo + hd] * _silu(g)).astype(BF16)
        for s in range(nb):
            mix_ref[:, s, cdim + lo:cdim + lo + hd] = out[s]


def _sample_mix(proj, cache, state, layer, prev, conv_w, cvec, gn_g, cos2, sin2):
    ts, S, C = proj.shape
    depth, _, heads, hd, _ = state.shape
    cwidth, cdim = conv_w.shape
    rdim = heads * hd
    D = cdim + rdim
    nb = SAMPLE_SEQS
    assert S % nb == 0 and S // nb >= STATE_BUFS - 1 and hd == LANES

    dec, rdec, kdec = (jnp.asarray(t, F32) for t in _decay_tables(ts, heads))

    conv_spec = pl.BlockSpec((None, cwidth - 1, nb, cdim), lambda i: (layer, 0, i, 0))
    st_spec = pl.BlockSpec((None, nb, heads, hd, hd), lambda i: (layer, i, 0, 0, 0))
    args = [proj, cache, state, conv_w, cvec, gn_g, cos2, sin2, dec, rdec, kdec]
    in_specs = [
        pl.BlockSpec((ts, nb, C), lambda i: (0, i, 0)),
        conv_spec,
        pl.BlockSpec(memory_space=pl.ANY),
        _const_spec(conv_w.shape),
        _const_spec(cvec.shape),
        _const_spec((1, rdim)),
        _const_spec((ts, hd)),
        _const_spec((ts, hd)),
        _const_spec(dec.shape),
        _const_spec(rdec.shape),
        _const_spec(kdec.shape),
    ]
    aliases = {}
    if prev is not None:
        aliases = {len(args): 1, len(args) + 1: 2}
        args += list(prev)
        in_specs += [pl.BlockSpec(memory_space=pl.ANY)] * 2

    kern = functools.partial(_sample_mix_kernel, layer=layer, ts=ts, nb=nb, heads=heads, hd=hd, cdim=cdim,
                             cwidth=cwidth)
    return pl.pallas_call(
        kern,
        grid=(S // nb,),
        in_specs=in_specs,
        out_specs=[pl.BlockSpec((ts, nb, D), lambda i: (0, i, 0)), conv_spec, st_spec],
        out_shape=[
            jax.ShapeDtypeStruct((ts, S, D), BF16),
            jax.ShapeDtypeStruct((depth, cwidth - 1, S, cdim), F32),
            jax.ShapeDtypeStruct((depth, S, heads, hd, hd), F32),
        ],
        scratch_shapes=[
            pltpu.VMEM((STATE_BUFS, nb, heads, hd, hd), F32),
            pltpu.SemaphoreType.DMA((STATE_BUFS,)),
        ],
        input_output_aliases=aliases,
        compiler_params=pltpu.CompilerParams(
            dimension_semantics=("arbitrary",), vmem_limit_bytes=VMEM_LIMIT),
        name="sample_mix",
    )(*args)


def _rope_tables(pos0, T, hd):
    half = hd // 2
    pos = pos0 + np.arange(T, dtype=np.float64)
    inv = ROPE_BASE ** (-np.arange(half, dtype=np.float64) / half)
    ang = pos[:, None] * inv[None, :]
    cos, sin = np.cos(ang), np.sin(ang)
    return (jnp.asarray(np.concatenate([cos, cos], axis=-1), F32),
            jnp.asarray(np.concatenate([-sin, sin], axis=-1), F32))


def kernel(x_prompt, x_sample, cache_conv, state_ret, norm1_g, w_in, conv_w, conv_b, conv_ln_g,
           conv_ln_b, ret_gn_g, w_out, norm2_g, w_up, w_down, final_norm_g):
    depth = w_in.shape[0]
    B, T, D = x_prompt.shape
    S, ts, _ = x_sample.shape
    heads, hd = state_ret.shape[2], state_ret.shape[3]

    cvec = jnp.stack([conv_b, conv_ln_g, conv_ln_b], axis=1)
    gf = final_norm_g[None, :]
    cos_p, sin_p = _rope_tables(0.0, T, hd)
    cos_s, sin_s = _rope_tables(float(PAST_LEN), ts, hd)

    xp = x_prompt.reshape(B * T, D)
    xs = x_sample.transpose(1, 0, 2).reshape(ts * S, D)
    cache_t = cache_conv.transpose(0, 2, 1, 3)
    prompt_state, sample_state = None, None
    for l in range(depth):
        final = l == depth - 1
        g1, g2, gn = norm1_g[l][None, :], norm2_g[l][None, :], ret_gn_g[l][None, :]

        proj, w_in_b = _in_proj(xs, g1, w_in, l, IN_PROJ_COLS)
        mix_s, *sample_state = _sample_mix(proj.reshape(ts, S, -1), cache_t, state_ret, l, sample_state,
                                           conv_w[l], cvec[l], gn, cos_s, sin_s)

        mix_p, *rest = _prompt_mix(xp.reshape(B, T, D), g1, w_in_b, l, prompt_state, conv_w[l], cvec[l], gn,
                                   cos_p, sin_p, heads, hd, (w_out, w_up, w_down))
        prompt_state, post_w = rest[:2], rest[2:]

        w_out_b, w_up_b, w_dn_b = post_w
        xp, xs = _post(xp, mix_p.reshape(B * T, D), xs, mix_s.reshape(ts * S, D), w_out_b, g2,
                       w_up_b, w_dn_b, gf, final)

    new_cache_t, new_state = sample_state
    return (xp.reshape(B, T, D), xs.reshape(ts, S, D).transpose(1, 0, 2), *prompt_state,
            new_cache_t.transpose(0, 2, 1, 3), new_state)
```
